```python
import math
import jax, jax.numpy as jnp
from jax import lax
import numpy as np

D_MODEL = 1024
BATCH = 2
SEQ = 8192
DEPTH = 1

N_MOD = 6
NSA_HEADS = 8
NSA_KV_GROUPS = 2
NSA_HPG = NSA_HEADS // NSA_KV_GROUPS
NSA_HEAD_DIM = 64
CMP_BLOCK = 32
CMP_STRIDE = 16
CMP_HIDDEN = 128
SLC_BLOCK = 64
SLC_TOPN = 16
WINDOW = 512
FORCE_BONUS = 1e4
DSA_HEADS = 8
DSA_HEAD_DIM = 64
DSA_KV_LATENT = 128
IDX_HEADS = 8
IDX_DIM = 64
IDX_TOPK_MAX = 256
N_HEADS_TOTAL = NSA_HEADS + DSA_HEADS
REL_BUCKETS = 32
REL_MAX_DISTANCE = 128
N_EXPERTS = 256
N_EXPERT_GROUPS = 8
TOPK_GROUPS = 4
TOP_K = 8
D_EXPERT = 256
ROUTED_SCALE = 2.5
MOE_BLOCK = 128
Q_BLOCK = 128
RMS_EPS = 1e-6
NEG_INF = -1e30

PROJ_WIDTHS = (
    NSA_HEADS * NSA_HEAD_DIM,
    6 * NSA_KV_GROUPS * NSA_HEAD_DIM,
    3 * NSA_HEADS,
    DSA_HEADS * DSA_HEAD_DIM,
    DSA_KV_LATENT,
    IDX_HEADS * IDX_DIM,
    IDX_DIM,
    IDX_HEADS,
    2 * D_MODEL,
)
D_IN = sum(PROJ_WIDTHS)

kernel_name = 'hybrid_nsa_dsa_moe_block'


def rms_norm(x, g):
    xf = x.astype(jnp.float32)
    y = xf * lax.rsqrt(jnp.mean(xf * xf, axis=-1, keepdims=True) + RMS_EPS)
    return (y * g.astype(jnp.float32)).astype(x.dtype)


def masked_softmax(logits, mask, axes):
    l = jnp.where(mask, logits.astype(jnp.float32), NEG_INF)
    m = jnp.max(l, axis=axes, keepdims=True)
    p = jnp.where(mask, jnp.exp(l - m), 0.0)
    return p / jnp.maximum(jnp.sum(p, axis=axes, keepdims=True), 1e-30)


def t5_bucket(dist):
    n = jnp.maximum(dist, 0)
    exact = REL_BUCKETS // 2
    nf = jnp.maximum(n, 1).astype(jnp.float32)
    large = exact + (jnp.log(nf / exact) / math.log(REL_MAX_DISTANCE / exact)
                     * (REL_BUCKETS - exact)).astype(jnp.int32)
    return jnp.where(n < exact, n, jnp.minimum(large, REL_BUCKETS - 1))


def swiglu(x, wg, wu, wd):
    return (jax.nn.silu(x @ wg) * (x @ wu)) @ wd


def compress_kv(kv, pe, w1, b1, w2):
    T = kv.shape[1]
    n_cmp = (T - CMP_BLOCK) // CMP_STRIDE + 1
    idx = jnp.arange(n_cmp)[:, None] * CMP_STRIDE + jnp.arange(CMP_BLOCK)[None, :]
    blk = kv[:, idx] + pe[None, None, :, None, :]
    z = jax.nn.gelu(jnp.einsum('bnlgd,ldk->bngk', blk, w1) + b1)
    return jnp.einsum('bngk,kd->bngd', z, w2)


def token_mixers(h, w_in, rel_bias, cmp_pe, cmp_w1, cmp_b1, cmp_w2, dsa_kv_norm,
                 dsa_w_uk, dsa_w_uv, w_branch_nsa, w_branch_dsa, w_out):
    B, T, _ = h.shape
    G, HPG, DH = NSA_KV_GROUPS, NSA_HPG, NSA_HEAD_DIM
    n_cmp = (T - CMP_BLOCK) // CMP_STRIDE + 1
    n_slc = T // SLC_BLOCK
    topn = min(SLC_TOPN, n_slc)
    k_dsa = min(IDX_TOPK_MAX, T // 4)
    n_qblk = T // Q_BLOCK
    scale = DH ** -0.5

    proj = jnp.einsum('btd,de->bte', h, w_in)
    splits = [int(s) for s in np.cumsum(PROJ_WIDTHS)[:-1]]
    q_nsa, kv_nsa, g_nsa, q_dsa, ckv, q_idx, k_idx, w_idx, g_merge = jnp.split(proj, splits, axis=-1)

    q_nsa = q_nsa.reshape(B, T, G, HPG, DH)
    kv_nsa = kv_nsa.reshape(B, T, 6, G, DH)
    g_nsa = jax.nn.sigmoid(g_nsa.reshape(B, T, G, HPG, 3))
    k_c = compress_kv(kv_nsa[:, :, 0], cmp_pe[0], cmp_w1[0], cmp_b1[0], cmp_w2[0])
    v_c = compress_kv(kv_nsa[:, :, 1], cmp_pe[1], cmp_w1[1], cmp_b1[1], cmp_w2[1])
    k_sb = kv_nsa[:, :, 2].reshape(B, n_slc, SLC_BLOCK, G, DH).transpose(0, 3, 1, 2, 4)
    v_sb = kv_nsa[:, :, 3].reshape(B, n_slc, SLC_BLOCK, G, DH).transpose(0, 3, 1, 2, 4)
    pad = ((0, 0), (WINDOW, 0), (0, 0), (0, 0))
    k_wp = jnp.pad(kv_nsa[:, :, 4], pad)
    v_wp = jnp.pad(kv_nsa[:, :, 5], pad)
    cmp_start = jnp.arange(n_cmp) * CMP_STRIDE
    cmp_end = cmp_start + (CMP_BLOCK - 1)
    slc_ids = jnp.arange(n_slc)
    slc_start = slc_ids * SLC_BLOCK
    overlap = (jnp.clip(jnp.minimum(cmp_start[:, None] + CMP_BLOCK, slc_start[None, :] + SLC_BLOCK)
                        - jnp.maximum(cmp_start[:, None], slc_start[None, :]), 0, None)
               .astype(jnp.float32) / CMP_BLOCK)
    tab_nsa = rel_bias[:, :NSA_HEADS].reshape(REL_BUCKETS, G, HPG)
    tab_dsa = rel_bias[:, NSA_HEADS:]
    gidx = jnp.arange(G)[None, :, None, None, None]
    gather_bg = jax.vmap(jax.vmap(lambda a, i: a[i]))
    gather_b = jax.vmap(lambda a, i: a[i])

    q_lat = jnp.einsum('bthd,chd->bthc', q_dsa.reshape(B, T, DSA_HEADS, DSA_HEAD_DIM), dsa_w_uk)
    c_kv = rms_norm(ckv, dsa_kv_norm)
    q_idx = q_idx.reshape(B, T, IDX_HEADS, IDX_DIM) * IDX_DIM ** -0.5
    w_idx = w_idx * IDX_HEADS ** -0.5
    key_pos = jnp.arange(T)

    def block(qb):
        qs = qb * Q_BLOCK
        t = qs + jnp.arange(Q_BLOCK)
        sl = lambda a: lax.dynamic_slice_in_dim(a, qs, Q_BLOCK, axis=1)
        q = sl(q_nsa)
        gt = jnp.moveaxis(sl(g_nsa), 1, 3)

        dist_c = t[:, None] - cmp_end[None, :]
        bias_c = jnp.transpose(tab_nsa[t5_bucket(dist_c)], (2, 3, 0, 1))
        s_c = jnp.einsum('bqghd,bngd->bghqn', q, k_c) * scale + bias_c
        p_c = masked_softmax(s_c, dist_c >= 0, -1)
        o_c = jnp.einsum('bghqn,bngd->bghqd', p_c.astype(v_c.dtype), v_c)

        imp = jnp.einsum('bghqn,nj->bgqj', p_c, overlap)
        cur = t // SLC_BLOCK
        admiss = slc_ids[None, :] <= cur[:, None]
        forced = ((slc_ids[None, :] == 0) | (slc_ids[None, :] == cur[:, None])
                  | (slc_ids[None, :] == cur[:, None] - 1))
        score = jnp.where(admiss, imp + jnp.where(forced, FORCE_BONUS, 0.0), NEG_INF)
        _, sel = lax.top_k(score, topn)
        k_sel = gather_bg(k_sb, sel)
        v_sel = gather_bg(v_sb, sel)
        pos = sel[..., None] * SLC_BLOCK + jnp.arange(SLC_BLOCK)
        dist_s = t[None, None, :, None, None] - pos
        bias_s = jnp.moveaxis(tab_nsa[t5_bucket(dist_s), gidx], -1, 2)
        s_s = jnp.einsum('bqghd,bgqnkd->bghqnk', q, k_sel) * scale + bias_s
        p_s = masked_softmax(s_s, (dist_s >= 0)[:, :, None], (-2, -1))
        o_s = jnp.einsum('bghqnk,bgqnkd->bghqd', p_s.astype(v_sel.dtype), v_sel)

        kw = lax.dynamic_slice_in_dim(k_wp, qs, Q_BLOCK + WINDOW, axis=1)
        vw = lax.dynamic_slice_in_dim(v_wp, qs, Q_BLOCK + WINDOW, axis=1)
        s_pos = qs - WINDOW + jnp.arange(Q_BLOCK + WINDOW)
        dist_w = t[:, None] - s_pos[None, :]
        mask_w = (dist_w >= 0) & (dist_w < WINDOW) & (s_pos[None, :] >= 0)
        bias_w = jnp.transpose(tab_nsa[t5_bucket(dist_w)], (2, 3, 0, 1))
        s_w = jnp.einsum('bqghd,bsgd->bghqs', q, kw) * scale + bias_w
        p_w = masked_softmax(s_w, mask_w, -1)
        o_w = jnp.einsum('bghqs,bsgd->bghqd', p_w.astype(vw.dtype), vw)

        o_nsa = gt[..., 0:1] * o_c + gt[..., 1:2] * o_s + gt[..., 2:3] * o_w
        o_nsa = jnp.transpose(o_nsa, (0, 3, 1, 2, 4)).reshape(B, Q_BLOCK, NSA_HEADS, DH)

        rel = jax.nn.relu(jnp.einsum('bqhd,bsd->bqhs', sl(q_idx), k_idx))
        idx_score = jnp.einsum('bqhs,bqh->bqs', rel, sl(w_idx)).astype(jnp.float32)
        idx_score = jnp.where(key_pos[None, None, :] <= t[None, :, None], idx_score, NEG_INF)
        _, sel_k = lax.top_k(idx_score, k_dsa)
        c_sel = gather_b(c_kv, sel_k)
        dist_d = t[None, :, None] - sel_k
        bias_d = jnp.moveaxis(tab_dsa[t5_bucket(dist_d)], -1, 1)
        s_d = jnp.einsum('bqhc,bqkc->bhqk', sl(q_lat), c_sel) * DSA_HEAD_DIM ** -0.5 + bias_d
        p_d = masked_softmax(s_d, (dist_d >= 0)[:, None], -1)
        o_lat = jnp.einsum('bhqk,bqkc->bqhc', p_d.astype(c_sel.dtype), c_sel)
        o_dsa = jnp.einsum('bqhc,chd->bqhd', o_lat, dsa_w_uv)
        return o_nsa, o_dsa

    o_nsa, o_dsa = lax.map(block, jnp.arange(n_qblk))
    o_nsa = jnp.moveaxis(o_nsa, 0, 1).reshape(B, T, NSA_HEADS * DH)
    o_dsa = jnp.moveaxis(o_dsa, 0, 1).reshape(B, T, DSA_HEADS * DSA_HEAD_DIM)
    g_a, g_b = jnp.split(jax.nn.sigmoid(g_merge), 2, axis=-1)
    y = g_a * (o_nsa @ w_branch_nsa) + g_b * (o_dsa @ w_branch_dsa)
    return y @ w_out


def moe_ffn(h, router_w, router_bias, w_g, w_u, w_d, ws_g, ws_u, ws_d):
    B, T, D = h.shape
    N = B * T
    hf = h.reshape(N, D)
    s = jax.nn.sigmoid(jnp.einsum('nd,de->ne', hf, router_w).astype(jnp.float32))
    sel = s + router_bias.astype(jnp.float32)
    grp_score = lax.top_k(sel.reshape(N, N_EXPERT_GROUPS, -1), 2)[0].sum(-1)
    _, gsel = lax.top_k(grp_score, TOPK_GROUPS)
    gmask = jnp.sum(jax.nn.one_hot(gsel, N_EXPERT_GROUPS, dtype=jnp.float32), axis=1) > 0
    emask = jnp.repeat(gmask, N_EXPERTS // N_EXPERT_GROUPS, axis=1)
    _, eidx = lax.top_k(jnp.where(emask, sel, NEG_INF), TOP_K)
    wts = jnp.take_along_axis(s, eidx, axis=1)
    wts = wts / jnp.sum(wts, axis=-1, keepdims=True) * ROUTED_SCALE

    NK = N * TOP_K
    flat_e = eidx.reshape(NK)
    flat_w = wts.reshape(NK)
    flat_tok = jnp.arange(NK, dtype=jnp.int32) // TOP_K
    order = jnp.argsort(flat_e)
    e_sorted = flat_e[order]
    counts = jnp.bincount(flat_e, length=N_EXPERTS)
    padded = (counts + MOE_BLOCK - 1) // MOE_BLOCK * MOE_BLOCK
    start = jnp.cumsum(counts) - counts
    pend = jnp.cumsum(padded)
    pstart = pend - padded
    dest = pstart[e_sorted] + jnp.arange(NK, dtype=jnp.int32) - start[e_sorted]
    n_blocks = (NK + N_EXPERTS * (MOE_BLOCK - 1) + MOE_BLOCK - 1) // MOE_BLOCK
    P = n_blocks * MOE_BLOCK
    slot_tok = jnp.full((P,), N, jnp.int32).at[dest].set(flat_tok[order])
    slot_w = jnp.zeros((P,), jnp.float32).at[dest].set(flat_w[order])
    blk_e = jnp.minimum(jnp.searchsorted(pend, jnp.arange(n_blocks) * MOE_BLOCK, side='right'),
                        N_EXPERTS - 1)
    h_pad = jnp.concatenate([hf, jnp.zeros((1, D), hf.dtype)], axis=0)

    def expert_block(args):
        tok, wt, e = args
        yb = swiglu(h_pad[tok], w_g[e], w_u[e], w_d[e])
        return yb * wt[:, None].astype(yb.dtype)

    yb = lax.map(expert_block, (slot_tok.reshape(n_blocks, MOE_BLOCK),
                                slot_w.reshape(n_blocks, MOE_BLOCK), blk_e))
    routed = jnp.zeros((N + 1, D), h.dtype).at[slot_tok].add(yb.reshape(P, D).astype(h.dtype))[:N]
    shared = swiglu(hf, ws_g, ws_u, ws_d)
    return (routed + shared).reshape(B, T, D)


def setup_inputs(seed: int = 0) -> dict:
    key = jax.random.key(seed)
    ks = jax.random.split(key, 28)
    L, D = DEPTH, D_MODEL
    nrm = lambda i, shape, s: jax.random.normal(ks[i], shape, jnp.float32) * s
    return {
        'x': nrm(0, (BATCH, SEQ, D), 1.0),
        'c': nrm(1, (BATCH, D), 1.0),
        'w_ada': nrm(2, (L, D, N_MOD * D), 0.25 * D ** -0.5),
        'b_ada': nrm(3, (L, N_MOD * D), 0.02),
        'g_pre_mix': 1.0 + nrm(4, (L, D), 0.02),
        'g_post_mix': 1.0 + nrm(5, (L, D), 0.02),
        'g_pre_ffn': 1.0 + nrm(6, (L, D), 0.02),
        'g_post_ffn': 1.0 + nrm(7, (L, D), 0.02),
        'w_in': nrm(8, (L, D, D_IN), D ** -0.5),
        'rel_bias': nrm(9, (REL_BUCKETS, N_HEADS_TOTAL), 0.2),
        'cmp_pe': nrm(10, (L, 2, CMP_BLOCK, NSA_HEAD_DIM), 0.1),
        'cmp_w1': nrm(11, (L, 2, CMP_BLOCK, NSA_HEAD_DIM, CMP_HIDDEN), (CMP_BLOCK * NSA_HEAD_DIM) ** -0.5),
        'cmp_b1': nrm(12, (L, 2, CMP_HIDDEN), 0.02),
        'cmp_w2': nrm(13, (L, 2, CMP_HIDDEN, NSA_HEAD_DIM), CMP_HIDDEN ** -0.5),
        'dsa_kv_norm': 1.0 + nrm(14, (L, DSA_KV_LATENT), 0.02),
        'dsa_w_uk': nrm(15, (L, DSA_KV_LATENT, DSA_HEADS, DSA_HEAD_DIM), DSA_KV_LATENT ** -0.5),
        'dsa_w_uv': nrm(16, (L, DSA_KV_LATENT, DSA_HEADS, DSA_HEAD_DIM), DSA_KV_LATENT ** -0.5),
        'w_branch_nsa': nrm(17, (L, NSA_HEADS * NSA_HEAD_DIM, D), (NSA_HEADS * NSA_HEAD_DIM) ** -0.5),
        'w_branch_dsa': nrm(18, (L, DSA_HEADS * DSA_HEAD_DIM, D), (DSA_HEADS * DSA_HEAD_DIM) ** -0.5),
        'w_out': nrm(19, (L, D, D), D ** -0.5),
        'router_w': nrm(20, (L, D, N_EXPERTS), D ** -0.5),
        'router_bias': nrm(21, (L, N_EXPERTS), 0.01),
        'w_exp_gate': nrm(22, (L, N_EXPERTS, D, D_EXPERT), D ** -0.5),
        'w_exp_up': nrm(23, (L, N_EXPERTS, D, D_EXPERT), D ** -0.5),
        'w_exp_down': nrm(24, (L, N_EXPERTS, D_EXPERT, D), D_EXPERT ** -0.5),
        'w_sh_gate': nrm(25, (L, D, D_EXPERT), D ** -0.5),
        'w_sh_up': nrm(26, (L, D, D_EXPERT), D ** -0.5),
        'w_sh_down': nrm(27, (L, D_EXPERT, D), D_EXPERT ** -0.5),
    }


def reference(x, c, w_ada, b_ada, g_pre_mix, g_post_mix, g_pre_ffn, g_post_ffn, w_in, rel_bias,
              cmp_pe, cmp_w1, cmp_b1, cmp_w2, dsa_kv_norm, dsa_w_uk, dsa_w_uv,
              w_branch_nsa, w_branch_dsa, w_out, router_w, router_bias,
              w_exp_gate, w_exp_up, w_exp_down, w_sh_gate, w_sh_up, w_sh_down):
    for l in range(DEPTH):
        mod = jnp.einsum('bd,de->be', jax.nn.silu(c), w_ada[l]) + b_ada[l]
        sh1, sc1, gt1, sh2, sc2, gt2 = jnp.split(mod[:, None, :], N_MOD, axis=-1)
        h = rms_norm(x, g_pre_mix[l]) * (1.0 + sc1) + sh1
        y = token_mixers(h, w_in[l], rel_bias, cmp_pe[l], cmp_w1[l], cmp_b1[l], cmp_w2[l],
                         dsa_kv_norm[l], dsa_w_uk[l], dsa_w_uv[l],
                         w_branch_nsa[l], w_branch_dsa[l], w_out[l])
        x = x + gt1 * rms_norm(y, g_post_mix[l])
        h = rms_norm(x, g_pre_ffn[l]) * (1.0 + sc2) + sh2
        y = moe_ffn(h, router_w[l], router_bias[l], w_exp_gate[l], w_exp_up[l], w_exp_down[l],
                    w_sh_gate[l], w_sh_up[l], w_sh_down[l])
        x = x + gt2 * rms_norm(y, g_post_ffn[l])
    return x
```

```python
import functools
import math

import jax
import jax.numpy as jnp
import numpy as np
from jax import lax
from jax.experimental import pallas as pl
from jax.experimental.pallas import tpu as pltpu

F32 = jnp.float32
BF16 = jnp.bfloat16
I32 = jnp.int32

D_MODEL = 1024
N_MOD = 6
NSA_HEADS = 8
NSA_KV_GROUPS = 2
NSA_HPG = NSA_HEADS // NSA_KV_GROUPS
NSA_HEAD_DIM = 64
CMP_BLOCK = 32
CMP_STRIDE = 16
CMP_HIDDEN = 128
SLC_BLOCK = 64
SLC_TOPN = 16
WINDOW = 512
FORCE_BONUS = 1e4
DSA_HEADS = 8
DSA_HEAD_DIM = 64
DSA_KV_LATENT = 128
IDX_HEADS = 8
IDX_DIM = 64
IDX_TOPK_MAX = 256
REL_BUCKETS = 32
REL_MAX_DISTANCE = 128
N_EXPERTS = 256
N_EXPERT_GROUPS = 8
TOPK_GROUPS = 4
TOP_K = 8
D_EXPERT = 256
ROUTED_SCALE = 2.5
RMS_EPS = 1e-6
NEG_INF = -1e30

LANE = 128
TQ = 128
NK = 256
M_FLOOR = -1e29
KNOCK = -3e38
CMP_PAD = 16
CMP_NEAR = 24
VMEM_LIMIT = 56 * 1024 * 1024

PROJ_WIDTHS = (512, 768, 24, 512, 128, 512, 64, 8, 2048)
PROJ_OFFS = tuple(int(v) for v in np.cumsum((0,) + PROJ_WIDTHS))


def _cparams(sem):
    return pltpu.CompilerParams(dimension_semantics=sem, vmem_limit_bytes=VMEM_LIMIT)


def _mod_body(c_ref, w_ref, b_ref, o_ref):
    c = c_ref[...]
    s = c * jax.nn.sigmoid(c)
    o_ref[...] = jnp.dot(s, w_ref[...], preferred_element_type=F32) + b_ref[...]


def _adaln(c, w_ada, b_ada):
    B, D = c.shape
    E = w_ada.shape[1]
    cp = jnp.zeros((8, D), F32).at[:B].set(c)
    out = pl.pallas_call(
        _mod_body,
        grid=(E // D,),
        in_specs=[pl.BlockSpec((8, D), lambda j: (0, 0)),
                  pl.BlockSpec((D, D), lambda j: (0, j)),
                  pl.BlockSpec((1, D), lambda j: (0, j))],
        out_specs=pl.BlockSpec((8, D), lambda j: (0, j)),
        out_shape=jax.ShapeDtypeStruct((8, E), F32),
        compiler_params=_cparams(("arbitrary",)),
        name="adaln_mod",
    )(cp, w_ada, b_ada.reshape(1, E))
    return out[:B]


PW = (512, 768, 512, 128, 512, 128, 2048)
PO = tuple(int(v) for v in np.cumsum((0,) + PW))


def _proj_body(x_ref, g_ref, sc_ref, sh_ref, w_ref, kn_ref,
               qn_ref, kv_ref, qd_ref, ckv_ref, qi_ref, sm_ref, gm_ref):
    x = x_ref[...]
    ms = jnp.mean(x * x, axis=-1, keepdims=True)
    h = x * lax.rsqrt(ms + RMS_EPS) * g_ref[...]
    h = h * (1.0 + sc_ref[...]) + sh_ref[...]
    hb = h.astype(BF16)

    def mm(k):
        return jnp.dot(hb, w_ref[:, PO[k]:PO[k + 1]], preferred_element_type=F32)

    qn_ref[...] = (mm(0) * NSA_HEAD_DIM ** -0.5).astype(BF16)
    kv_ref[...] = mm(1).astype(BF16)
    qd_ref[...] = mm(2).astype(BF16)
    c = mm(3)
    cms = jnp.mean(c * c, axis=-1, keepdims=True)
    ckv_ref[...] = (c * lax.rsqrt(cms + RMS_EPS) * kn_ref[...]).astype(BF16)
    qi_ref[...] = (mm(4) * IDX_DIM ** -0.5).astype(BF16)
    sm_ref[...] = mm(5)
    gm_ref[...] = mm(6)


def _input_proj(x, g_pre, sc, sh, w_in, kv_norm):
    B, T, D = x.shape
    TM = 512 if T % 512 == 0 else TQ
    o = PROJ_OFFS
    small = jnp.concatenate([w_in[:, o[2]:o[3]], w_in[:, o[6]:o[7]], w_in[:, o[7]:o[8]],
                             jnp.zeros((D, 32), F32)], axis=1)
    w_r = jnp.concatenate([w_in[:, o[0]:o[1]], w_in[:, o[1]:o[2]], w_in[:, o[3]:o[4]],
                           w_in[:, o[4]:o[5]], w_in[:, o[5]:o[6]], small,
                           w_in[:, o[8]:o[9]]], axis=1).astype(BF16)
    row = lambda w: pl.BlockSpec((None, TM, w), lambda b, i: (b, i, 0))
    vec = pl.BlockSpec((None, 1, D), lambda b, i: (b, 0, 0))
    outs = pl.pallas_call(
        _proj_body,
        grid=(B, T // TM),
        in_specs=[row(D),
                  pl.BlockSpec((1, D), lambda b, i: (0, 0)),
                  vec, vec,
                  pl.BlockSpec((D, PO[-1]), lambda b, i: (0, 0)),
                  pl.BlockSpec((1, DSA_KV_LATENT), lambda b, i: (0, 0))],
        out_specs=[row(w) for w in PW],
        out_shape=[jax.ShapeDtypeStruct((B, T, PW[0]), BF16),
                   jax.ShapeDtypeStruct((B, T, PW[1]), BF16),
                   jax.ShapeDtypeStruct((B, T, PW[2]), BF16),
                   jax.ShapeDtypeStruct((B, T, PW[3]), BF16),
                   jax.ShapeDtypeStruct((B, T, PW[4]), BF16),
                   jax.ShapeDtypeStruct((B, T, PW[5]), F32),
                   jax.ShapeDtypeStruct((B, T, PW[6]), F32)],
        compiler_params=_cparams(("arbitrary", "arbitrary")),
        name="prenorm_proj",
    )(x, g_pre.reshape(1, D), sc.reshape(B, 1, D), sh.reshape(B, 1, D), w_r,
      kv_norm.reshape(1, DSA_KV_LATENT))
    return outs


def _t5_bucket(dist):
    n = jnp.maximum(dist, 0)
    exact = REL_BUCKETS // 2
    nf = jnp.maximum(n, 1).astype(F32)
    large = exact + (jnp.log(nf / exact) / math.log(REL_MAX_DISTANCE / exact)
                     * (REL_BUCKETS - exact)).astype(I32)
    return jnp.where(n < exact, n, jnp.minimum(large, REL_BUCKETS - 1))


def _delta(tab, dist):
    d = tab[_t5_bucket(dist)] - tab[REL_BUCKETS - 1]
    return jnp.where((dist >= 0)[..., None], d, NEG_INF)


def _near_tables(tab):
    H = tab.shape[1]
    c = jnp.arange(TQ)[:, None]
    q = jnp.arange(TQ)[None, :]
    tabs = [jnp.full((TQ, TQ, H), NEG_INF, F32)]
    for diff in (0, 1):
        tabs.append(_delta(tab, diff * TQ + q - c))
    tabs += [jnp.zeros((TQ, TQ, H), F32)] * 2
    t = jnp.stack(tabs)
    return t.transpose(0, 1, 3, 2).reshape(5, TQ, H * TQ)


def _cmp_body(r_ref, w1_ref, c_ref, w2_ref, o_ref):
    y = jnp.dot(r_ref[...], w1_ref[...], preferred_element_type=F32)
    M = y.shape[0]
    z = y[:, :CMP_HIDDEN] + pltpu.roll(y[:, CMP_HIDDEN:], M - 1, 0) + c_ref[...]
    z = jax.nn.gelu(z)
    o_ref[...] = jnp.dot(z.astype(BF16), w2_ref[...], preferred_element_type=F32)


def _compress(kv6, cmp_pe, cmp_w1, cmp_b1, cmp_w2):
    B, T = kv6.shape[:2]
    G, DH = NSA_KV_GROUPS, NSA_HEAD_DIM
    M = T // CMP_STRIDE
    half = CMP_STRIDE * DH
    r = kv6[:, :, 0:2].transpose(0, 2, 3, 1, 4).reshape(B, 2, G, M, half)
    w1 = cmp_w1.reshape(2, 2, half, CMP_HIDDEN)
    w1c = jnp.concatenate([w1[:, 0], w1[:, 1]], axis=-1).astype(BF16)
    cvec = (jnp.einsum('wld,wldk->wk', cmp_pe, cmp_w1, precision=lax.Precision.HIGHEST)
            + cmp_b1).reshape(2, 1, CMP_HIDDEN)
    return pl.pallas_call(
        _cmp_body,
        grid=(B, 2, G),
        in_specs=[pl.BlockSpec((None, None, None, M, half), lambda b, w, g: (b, w, g, 0, 0)),
                  pl.BlockSpec((None, half, 2 * CMP_HIDDEN), lambda b, w, g: (w, 0, 0)),
                  pl.BlockSpec((None, 1, CMP_HIDDEN), lambda b, w, g: (w, 0, 0)),
                  pl.BlockSpec((None, CMP_HIDDEN, DH), lambda b, w, g: (w, 0, 0))],
        out_specs=pl.BlockSpec((None, None, None, M, DH), lambda b, w, g: (b, w, g, 0, 0)),
        out_shape=jax.ShapeDtypeStruct((B, 2, G, M, DH), F32),
        compiler_params=_cparams(("arbitrary",) * 3),
        name="kv_compress",
    )(r, w1c, cvec, cmp_w2.astype(BF16))


def _nsa_cmp_body(qt_ref, kc_ref, vct_ref, ct_ref, ovl_ref, oc_ref, sel_ref, s_ref,
                  *, n_blk, topn):
    i = pl.program_id(2)
    qt = qt_ref[...]
    s_ref[...] = jnp.dot(kc_ref[...], qt, preferred_element_type=F32)
    near = pl.ds(pl.multiple_of(i * 8, 8), CMP_NEAR)
    s_ref[near, :] = s_ref[near, :] + ct_ref[...]
    s = s_ref[...]
    MP = s.shape[0]
    row = lax.broadcasted_iota(I32, (MP, 1), 0)
    s = jnp.where(row < i * 8 + CMP_NEAR, s, NEG_INF)
    m = jnp.maximum(jnp.max(s, axis=0, keepdims=True), M_FLOOR)
    p = jnp.exp(s - m)
    l = jnp.sum(p, axis=0, keepdims=True)
    p = p / jnp.maximum(l, 1e-30)
    pb = p.astype(BF16)
    oc_ref[...] = jnp.dot(vct_ref[...], pb, preferred_element_type=F32)[:NSA_HEAD_DIM]
    psum = p[:, 0:TQ]
    for hh in range(1, NSA_HPG):
        psum = psum + p[:, hh * TQ:(hh + 1) * TQ]
    imp = jnp.dot(ovl_ref[...], psum.astype(BF16), preferred_element_type=F32)

    blk = lax.broadcasted_iota(I32, (LANE, 1), 0)
    t = i * TQ + lax.broadcasted_iota(I32, (1, TQ), 1)
    cur = t >> int(math.log2(SLC_BLOCK))
    admiss = blk <= cur
    bonus = jnp.where(blk == 0, FORCE_BONUS, 0.0)
    bonus = jnp.where(blk == cur, FORCE_BONUS, bonus)
    bonus = jnp.where(blk == cur - 1, FORCE_BONUS, bonus)
    score = jnp.where(admiss, imp + bonus, NEG_INF)
    score = jnp.where(blk < n_blk, score, KNOCK)
    blk_b = jnp.broadcast_to(blk, (LANE, TQ))
    selneg = jnp.full((LANE, TQ), NEG_INF, F32)
    for _ in range(topn):
        mx = jnp.max(score, axis=0, keepdims=True)
        first = jnp.min(jnp.where(score == mx, blk_b, LANE), axis=0, keepdims=True)
        pick = blk_b == first
        selneg = jnp.where(pick, 0.0, selneg)
        score = jnp.where(pick, KNOCK, score)
    sel_ref[...] = jnp.where(admiss, selneg, NEG_INF).astype(BF16)


def _nsa_compressed(qt, kc_aug, vct, ct, ovl, n_blk):
    B, G, nQ = qt.shape[:3]
    MP = kc_aug.shape[2]
    body = functools.partial(_nsa_cmp_body, n_blk=n_blk, topn=min(SLC_TOPN, n_blk))
    return pl.pallas_call(
        body,
        grid=(B, G, nQ),
        in_specs=[pl.BlockSpec((None, None, None, LANE, NSA_HPG * TQ), lambda b, g, i: (b, g, i, 0, 0)),
                  pl.BlockSpec((None, None, MP, LANE), lambda b, g, i: (b, g, 0, 0)),
                  pl.BlockSpec((None, None, LANE, MP), lambda b, g, i: (b, g, 0, 0)),
                  pl.BlockSpec((None, CMP_NEAR, NSA_HPG * TQ), lambda b, g, i: (g, 0, 0)),
                  pl.BlockSpec((LANE, MP), lambda b, g, i: (0, 0))],
        out_specs=[pl.BlockSpec((None, None, None, NSA_HEAD_DIM, NSA_HPG * TQ), lambda b, g, i: (b, g, i, 0, 0)),
                   pl.BlockSpec((None, None, None, LANE, TQ), lambda b, g, i: (b, g, i, 0, 0))],
        out_shape=[jax.ShapeDtypeStruct((B, G, nQ, NSA_HEAD_DIM, NSA_HPG * TQ), F32),
                   jax.ShapeDtypeStruct((B, G, nQ, LANE, TQ), BF16)],
        scratch_shapes=[pltpu.VMEM((MP, NSA_HPG * TQ), F32)],
        compiler_params=_cparams(("arbitrary",) * 3),
        name="nsa_compressed",
    )(qt, kc_aug, vct, ct, ovl)


def _softmax_step(s, vt, m_ref, acc_ref):
    m_old = m_ref[...]
    m_new = jnp.maximum(m_old, jnp.max(s, axis=0, keepdims=True))
    alpha = jnp.exp(m_old - m_new)
    p = jnp.exp(s - m_new).astype(BF16)
    acc_ref[...] = acc_ref[...] * alpha + jnp.dot(vt, p, preferred_element_type=F32)
    m_ref[...] = m_new


def _near_add(s, dt_ref, i, j):
    halves = []
    for sub in range(NK // TQ):
        idx = jnp.clip(i - (j * (NK // TQ) + sub) + 1, 0, 4)
        halves.append(s[sub * TQ:(sub + 1) * TQ] + dt_ref[idx])
    return jnp.concatenate(halves, axis=0)


def _nsa_sel_body(qt_ref, sel_ref, ks_ref, vst_ref, dt_ref, o_ref, qa_ref, m_ref, acc_ref):
    i = pl.program_id(2)
    L = NSA_HPG * TQ
    qa_ref[0:LANE, :] = qt_ref[...]
    selneg = sel_ref[...]
    qa_ref[LANE:2 * LANE, :] = jnp.concatenate([selneg] * NSA_HPG, axis=1)
    m_ref[...] = jnp.full((1, L), M_FLOOR, F32)
    acc_ref[...] = jnp.zeros(acc_ref.shape, F32)
    last = i // (NK // TQ)
    n_far = jnp.maximum(last - 1, 0)

    def step(j, near):
        k0 = pl.multiple_of(j * NK, NK)
        s = jnp.dot(ks_ref[pl.ds(k0, NK), :], qa_ref[...], preferred_element_type=F32)
        if near:
            s = _near_add(s, dt_ref, i, j)
        _softmax_step(s, vst_ref[:, pl.ds(k0, NK)], m_ref, acc_ref)

    def far_body(j, c):
        step(j, False)
        return c

    def near_body(j, c):
        step(j, True)
        return c

    lax.fori_loop(0, n_far, far_body, 0)
    lax.fori_loop(n_far, last + 1, near_body, 0)
    acc = acc_ref[...]
    o_ref[...] = acc[:NSA_HEAD_DIM] / acc[NSA_HEAD_DIM:NSA_HEAD_DIM + 1]


def _nsa_selected(qt, selneg, ks_aug, vst, dt):
    B, G, nQ = qt.shape[:3]
    T = ks_aug.shape[2]
    L = NSA_HPG * TQ
    return pl.pallas_call(
        _nsa_sel_body,
        grid=(B, G, nQ),
        in_specs=[pl.BlockSpec((None, None, None, LANE, L), lambda b, g, i: (b, g, i, 0, 0)),
                  pl.BlockSpec((None, None, None, LANE, TQ), lambda b, g, i: (b, g, i, 0, 0)),
                  pl.BlockSpec((None, None, T, 2 * LANE), lambda b, g, i: (b, g, 0, 0)),
                  pl.BlockSpec((None, None, LANE, T), lambda b, g, i: (b, g, 0, 0)),
                  pl.BlockSpec((None, 5, TQ, L), lambda b, g, i: (g, 0, 0, 0))],
        out_specs=pl.BlockSpec((None, None, None, NSA_HEAD_DIM, L), lambda b, g, i: (b, g, i, 0, 0)),
        out_shape=jax.ShapeDtypeStruct((B, G, nQ, NSA_HEAD_DIM, L), F32),
        scratch_shapes=[pltpu.VMEM((2 * LANE, L), BF16),
                        pltpu.VMEM((1, L), F32),
                        pltpu.VMEM((LANE, L), F32)],
        compiler_params=_cparams(("arbitrary",) * 3),
        name="nsa_selected",
    )(qt, selneg, ks_aug, vst, dt)


def _nsa_win_body(qt_ref, kw_ref, vwt_ref, w0_ref, dt_ref, oc_ref, os_ref, gl_ref, o_ref):
    i = pl.program_id(2)
    span = WINDOW + TQ
    k0 = pl.multiple_of(i * TQ, TQ)
    s = jnp.dot(kw_ref[pl.ds(k0, span), :], qt_ref[...], preferred_element_type=F32)
    s = jnp.concatenate([s[0:TQ] + w0_ref[...], s[TQ:WINDOW - TQ],
                         s[WINDOW - TQ:WINDOW] + dt_ref[2], s[WINDOW:span] + dt_ref[1]], axis=0)
    m = jnp.maximum(jnp.max(s, axis=0, keepdims=True), M_FLOOR)
    p = jnp.exp(s - m).astype(BF16)
    acc = jnp.dot(vwt_ref[:, pl.ds(k0, span)], p, preferred_element_type=F32)
    ow = acc[:NSA_HEAD_DIM] / acc[NSA_HEAD_DIM:NSA_HEAD_DIM + 1]
    g = jax.nn.sigmoid(gl_ref[...])
    o_ref[...] = g[0:1] * oc_ref[...] + g[1:2] * os_ref[...] + g[2:3] * ow


def _nsa_window(qt, kw_aug, vwt, w0, dt, oc, osel, gl):
    B, G, nQ = qt.shape[:3]
    TP = kw_aug.shape[2]
    L = NSA_HPG * TQ
    tile = lambda r: pl.BlockSpec((None, None, None, r, L), lambda b, g, i: (b, g, i, 0, 0))
    return pl.pallas_call(
        _nsa_win_body,
        grid=(B, G, nQ),
        in_specs=[tile(LANE),
                  pl.BlockSpec((None, None, TP, LANE), lambda b, g, i: (b, g, 0, 0)),
                  pl.BlockSpec((None, None, LANE, TP), lambda b, g, i: (b, g, 0, 0)),
                  pl.BlockSpec((TQ, L), lambda b, g, i: (0, 0)),
                  pl.BlockSpec((None, 5, TQ, L), lambda b, g, i: (g, 0, 0, 0)),
                  tile(NSA_HEAD_DIM), tile(NSA_HEAD_DIM), tile(8)],
        out_specs=tile(NSA_HEAD_DIM),
        out_shape=jax.ShapeDtypeStruct((B, G, nQ, NSA_HEAD_DIM, L), F32),
        compiler_params=_cparams(("arbitrary",) * 3),
        name="nsa_window",
    )(qt, kw_aug, vwt, w0, dt, oc, osel, gl)


INT_MIN = -2 ** 31


def _dsa_body(qdt_ref, qit_ref, wt_ref, ki_ref, ckv_ref, ckvt_ref, wuk_ref, wuvt_ref, dt_ref,
              o_ref, sc_ref, qb_ref, m_ref, acc_ref, *, k_top):
    i = pl.program_id(1)
    H = DSA_HEADS
    L = H * TQ
    last = i // (NK // TQ)
    n_steps = last + 1
    t_row = i * TQ + lax.broadcasted_iota(I32, (1, TQ), 1)

    qit = qit_ref[...]
    wt = wt_ref[...]

    def idx_body(j, c):
        k0 = pl.multiple_of(j * NK, NK)
        si = jnp.dot(ki_ref[pl.ds(k0, NK), :], qit, preferred_element_type=F32)
        r = jnp.maximum(si, 0.0) * wt
        sc = r[:, 0:TQ]
        for h in range(1, H):
            sc = sc + r[:, h * TQ:(h + 1) * TQ]
        bits = pltpu.bitcast(sc, I32)
        ik = jnp.where(sc == 0.0, 0, bits ^ ((bits >> 31) & 0x7FFFFFFF))
        key = k0 + lax.broadcasted_iota(I32, (NK, 1), 0)
        sc_ref[pl.ds(k0, NK), :] = jnp.where(key <= t_row, jnp.maximum(ik, INT_MIN + 1), INT_MIN)
        return c

    lax.fori_loop(0, n_steps, idx_body, 0)

    def count_ge(thr):
        def cbody(j, cnt):
            k0 = pl.multiple_of(j * NK, NK)
            blk = sc_ref[pl.ds(k0, NK), :]
            return cnt + jnp.sum(jnp.where(blk >= thr, 1.0, 0.0), axis=0, keepdims=True)
        return lax.fori_loop(0, n_steps, cbody, jnp.zeros((1, TQ), F32))

    def bit_body(b, tu):
        cand = tu | jnp.left_shift(jnp.int32(1), 31 - b)
        cnt = count_ge(cand ^ INT_MIN)
        return jnp.where(cnt >= k_top, cand, tu)

    tu = lax.fori_loop(0, 32, bit_body, jnp.zeros((1, TQ), I32))
    thr = jnp.maximum(tu ^ INT_MIN, INT_MIN + 1)

    qdt = qdt_ref[...]
    for h in range(H):
        ql = jnp.dot(wuk_ref[h], qdt[:, h * TQ:(h + 1) * TQ], preferred_element_type=F32)
        qb_ref[0:LANE, h * TQ:(h + 1) * TQ] = (ql * DSA_HEAD_DIM ** -0.5).astype(BF16)
    eye = (lax.broadcasted_iota(I32, (LANE, TQ), 0) == lax.broadcasted_iota(I32, (LANE, TQ), 1))
    eye = jnp.where(eye, 1.0, 0.0).astype(BF16)
    qb_ref[LANE:2 * LANE, :] = jnp.concatenate([eye] * H, axis=1)
    m_ref[...] = jnp.full((1, L), M_FLOOR, F32)
    acc_ref[...] = jnp.zeros(acc_ref.shape, F32)
    n_far = jnp.maximum(last - 1, 0)

    def step(j, near):
        k0 = pl.multiple_of(j * NK, NK)
        nb = jnp.where(sc_ref[pl.ds(k0, NK), :] >= thr, 0.0, NEG_INF).astype(BF16)
        a = jnp.concatenate([ckv_ref[pl.ds(k0, NK), :], nb], axis=1)
        s = jnp.dot(a, qb_ref[...], preferred_element_type=F32)
        if near:
            s = _near_add(s, dt_ref, i, j)
        _softmax_step(s, ckvt_ref[:, pl.ds(k0, NK)], m_ref, acc_ref)

    def far_body(j, c):
        step(j, False)
        return c

    def near_body(j, c):
        step(j, True)
        return c

    lax.fori_loop(0, n_far, far_body, 0)
    lax.fori_loop(n_far, last + 1, near_body, 0)
    acc = acc_ref[...]
    olat = (acc[:DSA_KV_LATENT] / acc[DSA_KV_LATENT:DSA_KV_LATENT + 1]).astype(BF16)
    for h in range(H):
        o_ref[:, h * TQ:(h + 1) * TQ] = jnp.dot(wuvt_ref[h], olat[:, h * TQ:(h + 1) * TQ],
                                                preferred_element_type=F32)


def _dsa(qdt, qit, wt, ki, ckv, ckvt, wuk, wuvt, dt, k_top):
    B, nQ = qdt.shape[:2]
    T = ki.shape[1]
    H = DSA_HEADS
    L = H * TQ
    R = ckvt.shape[1]
    tile = lambda r: pl.BlockSpec((None, None, r, L), lambda b, i: (b, i, 0, 0))
    return pl.pallas_call(
        functools.partial(_dsa_body, k_top=k_top),
        grid=(B, nQ),
        in_specs=[tile(DSA_HEAD_DIM), tile(LANE), tile(1),
                  pl.BlockSpec((None, T, LANE), lambda b, i: (b, 0, 0)),
                  pl.BlockSpec((None, T, DSA_KV_LATENT), lambda b, i: (b, 0, 0)),
                  pl.BlockSpec((None, R, T), lambda b, i: (b, 0, 0)),
                  pl.BlockSpec((H, DSA_KV_LATENT, DSA_HEAD_DIM), lambda b, i: (0, 0, 0)),
                  pl.BlockSpec((H, DSA_HEAD_DIM, DSA_KV_LATENT), lambda b, i: (0, 0, 0)),
                  pl.BlockSpec((5, TQ, L), lambda b, i: (0, 0, 0))],
        out_specs=tile(DSA_HEAD_DIM),
        out_shape=jax.ShapeDtypeStruct((B, nQ, DSA_HEAD_DIM, L), F32),
        scratch_shapes=[pltpu.VMEM((T, TQ), I32),
                        pltpu.VMEM((2 * LANE, L), BF16),
                        pltpu.VMEM((1, L), F32),
                        pltpu.VMEM((R, L), F32)],
        compiler_params=_cparams(("arbitrary",) * 2),
        name="dsa_attention",
    )(qdt, qit, wt, ki, ckv, ckvt, wuk, wuvt, dt)


def _heads_on_lanes(a, B, nQ, nh, dh):
    return a.reshape(B, nQ, TQ, nh, dh).transpose(0, 1, 4, 3, 2).reshape(B, nQ, dh, nh * TQ)


def _heads_off_lanes(a, B, nQ, nh, dh):
    return a.reshape(B, nQ, dh, nh, TQ).transpose(0, 1, 4, 3, 2).reshape(B, nQ * TQ, nh * dh)


def _token_mixers(qn, kvn, qd, ckv, qi, small, rel_bias, cmp_pe, cmp_w1, cmp_b1, cmp_w2,
                  dsa_w_uk, dsa_w_uv):
    B, T = qn.shape[:2]
    G, HPG, DH = NSA_KV_GROUPS, NSA_HPG, NSA_HEAD_DIM
    nQ = T // TQ
    n_blk = T // SLC_BLOCK
    L = HPG * TQ
    assert T % NK == 0 and n_blk <= LANE
    kv6 = kvn.reshape(B, T, 6, G, DH)

    qt = qn.reshape(B, nQ, TQ, G, HPG, DH).transpose(0, 3, 1, 5, 4, 2).reshape(B, G, nQ, DH, L)
    qt = jnp.concatenate([qt, jnp.ones((B, G, nQ, 1, L), BF16),
                          jnp.zeros((B, G, nQ, LANE - DH - 1, L), BF16)], axis=3)

    tab_nsa = rel_bias[:, :NSA_HEADS]
    dt_nsa = _near_tables(tab_nsa).reshape(5, TQ, G, L).transpose(2, 0, 1, 3)
    dt_dsa = _near_tables(rel_bias[:, NSA_HEADS:])

    kvc = _compress(kv6, cmp_pe, cmp_w1, cmp_b1, cmp_w2)
    M = T // CMP_STRIDE
    MP = M + CMP_PAD
    kc = kvc[:, 0].astype(BF16)
    flag = jnp.concatenate([jnp.full((CMP_PAD, 1), NEG_INF, F32), jnp.zeros((M, 1), F32)]).astype(BF16)
    kc_aug = jnp.concatenate(
        [jnp.pad(kc, ((0, 0), (0, 0), (CMP_PAD, 0), (0, 0))),
         jnp.broadcast_to(flag, (B, G, MP, 1)),
         jnp.zeros((B, G, MP, LANE - DH - 1), BF16)], axis=-1)
    vct = jnp.pad(kvc[:, 1].astype(BF16).transpose(0, 1, 3, 2),
                  ((0, 0), (0, 0), (0, LANE - DH), (CMP_PAD, 0)))
    mrow = jnp.arange(CMP_NEAR)[:, None]
    qcol = jnp.arange(TQ)[None, :]
    ct = _delta(tab_nsa, qcol - CMP_STRIDE * mrow + (2 * TQ - CMP_BLOCK + 1))
    ct = ct.reshape(CMP_NEAR, TQ, G, HPG).transpose(2, 0, 3, 1).reshape(G, CMP_NEAR, L)
    n_cmp = (T - CMP_BLOCK) // CMP_STRIDE + 1
    cs = jnp.arange(M) * CMP_STRIDE
    ss = jnp.arange(LANE) * SLC_BLOCK
    ovl = (jnp.clip(jnp.minimum(cs[None, :] + CMP_BLOCK, ss[:, None] + SLC_BLOCK)
                    - jnp.maximum(cs[None, :], ss[:, None]), 0, None).astype(F32) / CMP_BLOCK)
    ovl = jnp.where((jnp.arange(M)[None, :] < n_cmp) & (jnp.arange(LANE)[:, None] < n_blk), ovl, 0.0)
    ovl = jnp.pad(ovl, ((0, 0), (CMP_PAD, 0))).astype(BF16)
    oc, selneg = _nsa_compressed(qt, kc_aug, vct, ct, ovl, n_blk)

    ks = kv6[:, :, 2].transpose(0, 2, 1, 3)
    et = (jnp.arange(T)[:, None] // SLC_BLOCK == jnp.arange(LANE)[None, :]).astype(BF16)
    ks_aug = jnp.concatenate([ks, jnp.zeros((B, G, T, LANE - DH), BF16),
                              jnp.broadcast_to(et, (B, G, T, LANE))], axis=-1)
    vs = kv6[:, :, 3].transpose(0, 2, 3, 1)
    vst = jnp.concatenate([vs, jnp.ones((B, G, 1, T), BF16),
                           jnp.zeros((B, G, LANE - DH - 1, T), BF16)], axis=2)
    osel = _nsa_selected(qt, selneg, ks_aug, vst, dt_nsa)

    TP = T + WINDOW
    kw = jnp.pad(kv6[:, :, 4].transpose(0, 2, 1, 3), ((0, 0), (0, 0), (WINDOW, 0), (0, 0)))
    wflag = jnp.concatenate([jnp.full((WINDOW, 1), NEG_INF, F32), jnp.zeros((T, 1), F32)]).astype(BF16)
    kw_aug = jnp.concatenate([kw, jnp.broadcast_to(wflag, (B, G, TP, 1)),
                              jnp.zeros((B, G, TP, LANE - DH - 1), BF16)], axis=-1)
    vw = jnp.pad(kv6[:, :, 5].transpose(0, 2, 3, 1), ((0, 0), (0, 0), (0, 0), (WINDOW, 0)))
    vwt = jnp.concatenate([vw, jnp.ones((B, G, 1, TP), BF16),
                           jnp.zeros((B, G, LANE - DH - 1, TP), BF16)], axis=2)
    u = jnp.arange(TQ)[:, None]
    w0 = jnp.tile(jnp.where(u > jnp.arange(TQ)[None, :], 0.0, NEG_INF).astype(F32), (1, HPG))
    gl = small[..., 0:3 * NSA_HEADS].reshape(B, nQ, TQ, G, HPG, 3).transpose(0, 3, 1, 5, 4, 2)
    gl = jnp.pad(gl.reshape(B, G, nQ, 3, L), ((0, 0), (0, 0), (0, 0), (0, 5), (0, 0)))
    o_nsa_t = _nsa_window(qt, kw_aug, vwt, w0, dt_nsa, oc, osel, gl)
    o_nsa = o_nsa_t.reshape(B, G, nQ, DH, HPG, TQ).transpose(0, 2, 5, 1, 4, 3).reshape(B, T, NSA_HEADS * DH)

    H = DSA_HEADS
    qdt = _heads_on_lanes(qd, B, nQ, H, DSA_HEAD_DIM)
    qit = _heads_on_lanes(qi, B, nQ, IDX_HEADS, IDX_DIM)
    qit = jnp.pad(qit, ((0, 0), (0, 0), (0, LANE - IDX_DIM), (0, 0)))
    k_idx = small[..., 24:24 + IDX_DIM]
    w_idx = small[..., 24 + IDX_DIM:24 + IDX_DIM + IDX_HEADS] * IDX_HEADS ** -0.5
    ki = jnp.pad(k_idx.astype(BF16), ((0, 0), (0, 0), (0, LANE - IDX_DIM)))
    wt = w_idx.reshape(B, nQ, TQ, IDX_HEADS).transpose(0, 1, 3, 2).reshape(B, nQ, 1, IDX_HEADS * TQ)
    ckvt = jnp.concatenate([ckv.transpose(0, 2, 1), jnp.ones((B, 1, T), BF16),
                            jnp.zeros((B, 7, T), BF16)], axis=1)
    wuk = dsa_w_uk.transpose(1, 0, 2).astype(BF16)
    wuvt = dsa_w_uv.transpose(1, 2, 0).astype(BF16)
    o_dsa_t = _dsa(qdt, qit, wt, ki, ckv, ckvt, wuk, wuvt, dt_dsa, min(IDX_TOPK_MAX, T // 4))
    o_dsa = _heads_off_lanes(o_dsa_t, B, nQ, H, DSA_HEAD_DIM)
    return o_nsa, o_dsa


def _rms(v, g):
    return v * lax.rsqrt(jnp.mean(v * v, axis=-1, keepdims=True) + RMS_EPS) * g


def _merge_body(on_ref, od_ref, gm_ref, x_ref, wn_ref, wd_ref, wo_ref, gpost_ref, gt_ref,
                gpre_ref, sc_ref, sh_ref, rwt_ref, x1_ref, h2_ref, lt_ref):
    D = D_MODEL
    ya = jnp.dot(on_ref[...], wn_ref[...], preferred_element_type=F32)
    yb = jnp.dot(od_ref[...], wd_ref[...], preferred_element_type=F32)
    y = jax.nn.sigmoid(gm_ref[:, 0:D]) * ya + jax.nn.sigmoid(gm_ref[:, D:2 * D]) * yb
    y2 = jnp.dot(y.astype(BF16), wo_ref[...], preferred_element_type=F32)
    x1 = x_ref[...] + gt_ref[...] * _rms(y2, gpost_ref[...])
    x1_ref[...] = x1
    h2 = (_rms(x1, gpre_ref[...]) * (1.0 + sc_ref[...]) + sh_ref[...]).astype(BF16)
    h2_ref[...] = h2
    lt_ref[...] = lax.dot_general(rwt_ref[...], h2, (((1,), (1,)), ((), ())),
                                  preferred_element_type=F32)


def _merge(o_nsa, o_dsa, gm, x, wn, wd, wo, g_post, gt1, g_pre, sc2, sh2, router_w):
    B, T, D = x.shape
    TM = 256
    E = router_w.shape[1]
    row = lambda w: pl.BlockSpec((None, TM, w), lambda b, i: (b, i, 0))
    full = lambda a: pl.BlockSpec(a.shape, lambda b, i: (0,) * a.ndim)
    vecb = pl.BlockSpec((None, 1, D), lambda b, i: (b, 0, 0))
    vec = pl.BlockSpec((1, D), lambda b, i: (0, 0))
    wn, wd, wo = wn.astype(BF16), wd.astype(BF16), wo.astype(BF16)
    rwt = router_w.T.astype(BF16)
    return pl.pallas_call(
        _merge_body,
        grid=(B, T // TM),
        in_specs=[row(o_nsa.shape[-1]), row(o_dsa.shape[-1]), row(2 * D), row(D),
                  full(wn), full(wd), full(wo), vec, vecb, vec, vecb, vecb, full(rwt)],
        out_specs=[row(D), row(D), pl.BlockSpec((None, E, TM), lambda b, i: (b, 0, i))],
        out_shape=[jax.ShapeDtypeStruct((B, T, D), F32),
                   jax.ShapeDtypeStruct((B, T, D), BF16),
                   jax.ShapeDtypeStruct((B, E, T), F32)],
        compiler_params=_cparams(("arbitrary",) * 2),
        name="merge_postnorm",
    )(o_nsa, o_dsa, gm, x, wn, wd, wo, g_post.reshape(1, D), gt1.reshape(B, 1, D),
      g_pre.reshape(1, D), sc2.reshape(B, 1, D), sh2.reshape(B, 1, D), rwt)


TR = 256
GSZ = N_EXPERTS // N_EXPERT_GROUPS


def _first_max(v, idx, big):
    mx = jnp.max(v, axis=0, keepdims=True)
    return jnp.min(jnp.where(v == mx, idx, big), axis=0, keepdims=True), mx


def _route_body(lt_ref, bias_ref, eid_ref, rk_ref, wt_ref, cnt_ref, carry_ref):
    first_step = (pl.program_id(0) == 0) & (pl.program_id(1) == 0)

    @pl.when(first_step)
    def _():
        carry_ref[...] = jnp.zeros(carry_ref.shape, F32)

    s = jax.nn.sigmoid(lt_ref[...])
    sel = s + bias_ref[:, 0:1]
    grow = lax.broadcasted_iota(I32, (GSZ, TR), 0)
    gs = []
    for g in range(N_EXPERT_GROUPS):
        blk = sel[g * GSZ:(g + 1) * GSZ]
        f1, m1 = _first_max(blk, grow, GSZ)
        m2 = jnp.max(jnp.where(grow == f1, KNOCK, blk), axis=0, keepdims=True)
        gs.append(m1 + m2)
    gsc = jnp.concatenate(gs, axis=0)
    gidx = lax.broadcasted_iota(I32, (N_EXPERT_GROUPS, TR), 0)
    gsel = jnp.zeros((N_EXPERT_GROUPS, TR), F32)
    for _ in range(TOPK_GROUPS):
        f, _mx = _first_max(gsc, gidx, N_EXPERT_GROUPS)
        pick = gidx == f
        gsel = jnp.where(pick, 1.0, gsel)
        gsc = jnp.where(pick, KNOCK, gsc)
    cur = jnp.concatenate(
        [jnp.where(gsel[g:g + 1] > 0.0, sel[g * GSZ:(g + 1) * GSZ], NEG_INF)
         for g in range(N_EXPERT_GROUPS)], axis=0)
    erow = lax.broadcasted_iota(I32, (N_EXPERTS, TR), 0)
    member = jnp.zeros((N_EXPERTS, TR), F32)
    picks, eids, ws = [], [], []
    for _ in range(TOP_K):
        f, _mx = _first_max(cur, erow, N_EXPERTS)
        pick = erow == f
        picks.append(pick)
        eids.append(f)
        ws.append(jnp.sum(jnp.where(pick, s, 0.0), axis=0, keepdims=True))
        member = jnp.where(pick, 1.0, member)
        cur = jnp.where(pick, KNOCK, cur)
    wsum = ws[0]
    for k in range(1, TOP_K):
        wsum = wsum + ws[k]
    mb = member.astype(BF16)
    upper = (lax.broadcasted_iota(I32, (TR, TR), 0) < lax.broadcasted_iota(I32, (TR, TR), 1))
    upper = jnp.where(upper, 1.0, 0.0).astype(BF16)
    rank = jnp.dot(mb, upper, preferred_element_type=F32) + carry_ref[:, 0:1]
    eid_ref[...] = jnp.concatenate(eids, axis=0)
    rk_ref[...] = jnp.concatenate(
        [jnp.sum(jnp.where(p, rank, 0.0), axis=0, keepdims=True) for p in picks], axis=0)
    wt_ref[...] = jnp.concatenate([w / wsum * ROUTED_SCALE for w in ws], axis=0)
    carry_ref[...] = carry_ref[...] + jnp.dot(mb, jnp.ones((TR, LANE), BF16),
                                              preferred_element_type=F32)
    cnt_ref[...] = carry_ref[...]


def _route(lt, router_bias):
    B, E, T = lt.shape
    out8 = lambda: pl.BlockSpec((None, TOP_K, TR), lambda b, i: (b, 0, i))
    return pl.pallas_call(
        _route_body,
        grid=(B, T // TR),
        in_specs=[pl.BlockSpec((None, E, TR), lambda b, i: (b, 0, i)),
                  pl.BlockSpec((E, 1), lambda b, i: (0, 0))],
        out_specs=[out8(), out8(), out8(), pl.BlockSpec((E, LANE), lambda b, i: (0, 0))],
        out_shape=[jax.ShapeDtypeStruct((B, TOP_K, T), I32),
                   jax.ShapeDtypeStruct((B, TOP_K, T), F32),
                   jax.ShapeDtypeStruct((B, TOP_K, T), F32),
                   jax.ShapeDtypeStruct((E, LANE), F32)],
        scratch_shapes=[pltpu.VMEM((E, LANE), F32)],
        compiler_params=_cparams(("arbitrary",) * 2),
        name="moe_router",
    )(lt, router_bias.reshape(E, 1))


BM = 256


def _gmm_body(be_ref, nu_ref, x_ref, wg_ref, wu_ref, wd_ref, y_ref, wgb, wub, wdb):
    b = pl.program_id(0)

    @pl.when(b < nu_ref[0])
    def _():
        prev = be_ref[jnp.maximum(b - 1, 0)]

        @pl.when((b == 0) | (be_ref[b] != prev))
        def _():
            wgb[...] = wg_ref[...].astype(BF16)
            wub[...] = wu_ref[...].astype(BF16)
            wdb[...] = wd_ref[...].astype(BF16)

        xb = x_ref[...]
        gt = jnp.dot(xb, wgb[...], preferred_element_type=F32)
        up = jnp.dot(xb, wub[...], preferred_element_type=F32)
        a = (gt * jax.nn.sigmoid(gt) * up).astype(BF16)
        y_ref[...] = jnp.dot(a, wdb[...], preferred_element_type=F32)


def _grouped_mlp(xs, blk_e, n_used, w_g, w_u, w_d):
    P, D = xs.shape
    nb = P // BM
    De = w_g.shape[-1]
    rows = lambda b, be, nu: (jnp.minimum(b, nu[0] - 1), 0)
    wsel = lambda b, be, nu: (be[b], 0, 0)
    grid_spec = pltpu.PrefetchScalarGridSpec(
        num_scalar_prefetch=2,
        grid=(nb,),
        in_specs=[pl.BlockSpec((BM, D), rows),
                  pl.BlockSpec((None, D, De), wsel),
                  pl.BlockSpec((None, D, De), wsel),
                  pl.BlockSpec((None, De, D), wsel)],
        out_specs=pl.BlockSpec((BM, D), rows),
        scratch_shapes=[pltpu.VMEM((D, De), BF16), pltpu.VMEM((D, De), BF16),
                        pltpu.VMEM((De, D), BF16)])
    return pl.pallas_call(
        _gmm_body,
        grid_spec=grid_spec,
        out_shape=jax.ShapeDtypeStruct((P, D), F32),
        compiler_params=_cparams(("arbitrary",)),
        name="moe_experts",
    )(blk_e, n_used, xs, w_g, w_u, w_d)


def _final_body(r_ref, h_ref, x1_ref, wg_ref, wu_ref, wd_ref, g_ref, gt_ref, o_ref):
    hb = h_ref[...]
    gt = jnp.dot(hb, wg_ref[...], preferred_element_type=F32)
    up = jnp.dot(hb, wu_ref[...], preferred_element_type=F32)
    a = (gt * jax.nn.sigmoid(gt) * up).astype(BF16)
    y = r_ref[...] + jnp.dot(a, wd_ref[...], preferred_element_type=F32)
    o_ref[...] = x1_ref[...] + gt_ref[...] * _rms(y, g_ref[...])


def _final(routed, h2, x1, ws_g, ws_u, ws_d, g_post, gt2):
    B, T, D = x1.shape
    TM = 512 if T % 512 == 0 else TQ
    row = pl.BlockSpec((None, TM, D), lambda b, i: (b, i, 0))
    full = lambda a: pl.BlockSpec(a.shape, lambda b, i: (0,) * a.ndim)
    ws_g, ws_u, ws_d = ws_g.astype(BF16), ws_u.astype(BF16), ws_d.astype(BF16)
    return pl.pallas_call(
        _final_body,
        grid=(B, T // TM),
        in_specs=[row, row, row, full(ws_g), full(ws_u), full(ws_d),
                  pl.BlockSpec((1, D), lambda b, i: (0, 0)),
                  pl.BlockSpec((None, 1, D), lambda b, i: (b, 0, 0))],
        out_specs=row,
        out_shape=jax.ShapeDtypeStruct((B, T, D), F32),
        compiler_params=_cparams(("arbitrary",) * 2),
        name="shared_final",
    )(routed, h2, x1, ws_g, ws_u, ws_d, g_post.reshape(1, D), gt2.reshape(B, 1, D))


def _moe(h2, lt, x1, router_bias, w_g, w_u, w_d, ws_g, ws_u, ws_d, g_post, gt2):
    B, T, D = x1.shape
    N = B * T
    eid, rk, wts, cnt = _route(lt, router_bias)
    counts = cnt[:, 0].astype(I32)
    padded = (counts + BM - 1) // BM * BM
    pend = jnp.cumsum(padded)
    pstart = pend - padded
    nb = (N * TOP_K + N_EXPERTS * (BM - 1) + BM - 1) // BM
    P = nb * BM
    blk_e = jnp.minimum(jnp.searchsorted(pend, jnp.arange(nb, dtype=I32) * BM, side='right'),
                        N_EXPERTS - 1).astype(I32)
    n_used = (pend[-1] // BM).astype(I32).reshape(1)
    eid_f = eid.transpose(0, 2, 1).reshape(N, TOP_K)
    slot = pstart[eid_f] + rk.transpose(0, 2, 1).reshape(N, TOP_K).astype(I32)
    w_f = wts.transpose(0, 2, 1).reshape(N, TOP_K)
    h2f = h2.reshape(N, D)
    xs = jnp.zeros((P, D), BF16).at[slot.reshape(-1)].set(jnp.repeat(h2f, TOP_K, axis=0))
    ys = _grouped_mlp(xs, blk_e, n_used, w_g, w_u, w_d)
    routed = jnp.sum(ys[slot] * w_f[:, :, None], axis=1)
    return _final(routed.reshape(B, T, D), h2, x1, ws_g, ws_u, ws_d, g_post, gt2)


def kernel(x, c, w_ada, b_ada, g_pre_mix, g_post_mix, g_pre_ffn, g_post_ffn, w_in, rel_bias, cmp_pe, cmp_w1, cmp_b1, cmp_w2, dsa_kv_norm, dsa_w_uk, dsa_w_uv, w_branch_nsa, w_branch_dsa, w_out, router_w, router_bias, w_exp_gate, w_exp_up, w_exp_down, w_sh_gate, w_sh_up, w_sh_down):
    depth = w_ada.shape[0]
    D = D_MODEL
    for l in range(depth):
        mod = _adaln(c, w_ada[l], b_ada[l])
        sh1, sc1, gt1, sh2, sc2, gt2 = [mod[:, k * D:(k + 1) * D] for k in range(N_MOD)]
        qn, kvn, qd, ckv, qi, small, gm = _input_proj(x, g_pre_mix[l], sc1, sh1, w_in[l], dsa_kv_norm[l])
        o_nsa, o_dsa = _token_mixers(qn, kvn, qd, ckv, qi, small, rel_bias, cmp_pe[l], cmp_w1[l],
                                     cmp_b1[l], cmp_w2[l], dsa_w_uk[l], dsa_w_uv[l])
        x1, h2, lt = _merge(o_nsa.astype(BF16), o_dsa.astype(BF16), gm, x, w_branch_nsa[l],
                            w_branch_dsa[l], w_out[l], g_post_mix[l], gt1, g_pre_ffn[l], sc2, sh2,
                            router_w[l])
        x = _moe(h2, lt, x1, router_bias[l], w_exp_gate[l], w_exp_up[l], w_exp_down[l],
                 w_sh_gate[l], w_sh_up[l], w_sh_down[l], g_post_ffn[l], gt2)
    return x
```

```python
import functools
import math

import jax
import jax.numpy as jnp
import numpy as np
from jax import lax
from jax.experimental import pallas as pl
from jax.experimental.pallas import tpu as pltpu

F32 = jnp.float32
BF16 = jnp.bfloat16
I32 = jnp.int32

D_MODEL = 1024
N_MOD = 6
NSA_HEADS = 8
NSA_KV_GROUPS = 2
NSA_HPG = NSA_HEADS // NSA_KV_GROUPS
NSA_HEAD_DIM = 64
CMP_BLOCK = 32
CMP_STRIDE = 16
CMP_HIDDEN = 128
SLC_BLOCK = 64
SLC_TOPN = 16
WINDOW = 512
FORCE_BONUS = 1e4
DSA_HEADS = 8
DSA_HEAD_DIM = 64
DSA_KV_LATENT = 128
IDX_HEADS = 8
IDX_DIM = 64
IDX_TOPK_MAX = 256
REL_BUCKETS = 32
REL_MAX_DISTANCE = 128
N_EXPERTS = 256
N_EXPERT_GROUPS = 8
TOPK_GROUPS = 4
TOP_K = 8
D_EXPERT = 256
ROUTED_SCALE = 2.5
RMS_EPS = 1e-6
NEG_INF = -1e30

LANE = 128
TQ = 128
NK = 256
M_FLOOR = -1e29
KNOCK = -3e38
CMP_PAD = 16
CMP_NEAR = 24
VMEM_LIMIT = 56 * 1024 * 1024

PROJ_WIDTHS = (512, 768, 24, 512, 128, 512, 64, 8, 2048)
PROJ_OFFS = tuple(int(v) for v in np.cumsum((0,) + PROJ_WIDTHS))


def _cparams(sem):
    return pltpu.CompilerParams(dimension_semantics=sem, vmem_limit_bytes=VMEM_LIMIT)


def _mod_body(c_ref, w_ref, b_ref, o_ref):
    c = c_ref[...]
    s = c * jax.nn.sigmoid(c)
    o_ref[...] = jnp.dot(s, w_ref[...], preferred_element_type=F32) + b_ref[...]


def _adaln(c, w_ada, b_ada):
    B, D = c.shape
    E = w_ada.shape[1]
    cp = jnp.zeros((8, D), F32).at[:B].set(c)
    out = pl.pallas_call(
        _mod_body,
        grid=(E // D,),
        in_specs=[pl.BlockSpec((8, D), lambda j: (0, 0)),
                  pl.BlockSpec((D, D), lambda j: (0, j)),
                  pl.BlockSpec((1, D), lambda j: (0, j))],
        out_specs=pl.BlockSpec((8, D), lambda j: (0, j)),
        out_shape=jax.ShapeDtypeStruct((8, E), F32),
        compiler_params=_cparams(("arbitrary",)),
        name="adaln_mod",
    )(cp, w_ada, b_ada.reshape(1, E))
    return out[:B]


PW = (512, 768, 512, 128, 512, 128, 2048)
PO = tuple(int(v) for v in np.cumsum((0,) + PW))


def _proj_body(x_ref, g_ref, sc_ref, sh_ref, w_ref, kn_ref,
               qn_ref, kv_ref, qd_ref, ckv_ref, qi_ref, sm_ref, gm_ref):
    x = x_ref[...]
    ms = jnp.mean(x * x, axis=-1, keepdims=True)
    h = x * lax.rsqrt(ms + RMS_EPS) * g_ref[...]
    h = h * (1.0 + sc_ref[...]) + sh_ref[...]
    hb = h.astype(BF16)

    def mm(k):
        return jnp.dot(hb, w_ref[:, PO[k]:PO[k + 1]], preferred_element_type=F32)

    qn_ref[...] = (mm(0) * NSA_HEAD_DIM ** -0.5).astype(BF16)
    kv_ref[...] = mm(1).astype(BF16)
    qd_ref[...] = mm(2).astype(BF16)
    c = mm(3)
    cms = jnp.mean(c * c, axis=-1, keepdims=True)
    ckv_ref[...] = (c * lax.rsqrt(cms + RMS_EPS) * kn_ref[...]).astype(BF16)
    qi_ref[...] = (mm(4) * IDX_DIM ** -0.5).astype(BF16)
    sm_ref[...] = mm(5)
    gm_ref[...] = mm(6)


def _input_proj(x, g_pre, sc, sh, w_in, kv_norm):
    B, T, D = x.shape
    TM = 512 if T % 512 == 0 else TQ
    o = PROJ_OFFS
    small = jnp.concatenate([w_in[:, o[2]:o[3]], w_in[:, o[6]:o[7]], w_in[:, o[7]:o[8]],
                             jnp.zeros((D, 32), F32)], axis=1)
    w_r = jnp.concatenate([w_in[:, o[0]:o[1]], w_in[:, o[1]:o[2]], w_in[:, o[3]:o[4]],
                           w_in[:, o[4]:o[5]], w_in[:, o[5]:o[6]], small,
                           w_in[:, o[8]:o[9]]], axis=1).astype(BF16)
    row = lambda w: pl.BlockSpec((None, TM, w), lambda b, i: (b, i, 0))
    vec = pl.BlockSpec((None, 1, D), lambda b, i: (b, 0, 0))
    outs = pl.pallas_call(
        _proj_body,
        grid=(B, T // TM),
        in_specs=[row(D),
                  pl.BlockSpec((1, D), lambda b, i: (0, 0)),
                  vec, vec,
                  pl.BlockSpec((D, PO[-1]), lambda b, i: (0, 0)),
                  pl.BlockSpec((1, DSA_KV_LATENT), lambda b, i: (0, 0))],
        out_specs=[row(w) for w in PW],
        out_shape=[jax.ShapeDtypeStruct((B, T, PW[0]), BF16),
                   jax.ShapeDtypeStruct((B, T, PW[1]), BF16),
                   jax.ShapeDtypeStruct((B, T, PW[2]), BF16),
                   jax.ShapeDtypeStruct((B, T, PW[3]), BF16),
                   jax.ShapeDtypeStruct((B, T, PW[4]), BF16),
                   jax.ShapeDtypeStruct((B, T, PW[5]), F32),
                   jax.ShapeDtypeStruct((B, T, PW[6]), F32)],
        compiler_params=_cparams(("arbitrary", "arbitrary")),
        name="prenorm_proj",
    )(x, g_pre.reshape(1, D), sc.reshape(B, 1, D), sh.reshape(B, 1, D), w_r,
      kv_norm.reshape(1, DSA_KV_LATENT))
    return outs


def _t5_bucket(dist):
    n = jnp.maximum(dist, 0)
    exact = REL_BUCKETS // 2
    nf = jnp.maximum(n, 1).astype(F32)
    large = exact + (jnp.log(nf / exact) / math.log(REL_MAX_DISTANCE / exact)
                     * (REL_BUCKETS - exact)).astype(I32)
    return jnp.where(n < exact, n, jnp.minimum(large, REL_BUCKETS - 1))


def _delta(tab, dist):
    d = tab[_t5_bucket(dist)] - tab[REL_BUCKETS - 1]
    return jnp.where((dist >= 0)[..., None], d, NEG_INF)


def _near_tables(tab):
    H = tab.shape[1]
    c = jnp.arange(TQ)[:, None]
    q = jnp.arange(TQ)[None, :]
    tabs = [jnp.full((TQ, TQ, H), NEG_INF, F32)]
    for diff in (0, 1):
        tabs.append(_delta(tab, diff * TQ + q - c))
    tabs += [jnp.zeros((TQ, TQ, H), F32)] * 2
    t = jnp.stack(tabs)
    return t.transpose(0, 1, 3, 2).reshape(5, TQ, H * TQ)


def _cmp_body(r_ref, w1_ref, c_ref, w2_ref, o_ref):
    y = jnp.dot(r_ref[...], w1_ref[...], preferred_element_type=F32)
    M = y.shape[0]
    z = y[:, :CMP_HIDDEN] + pltpu.roll(y[:, CMP_HIDDEN:], M - 1, 0) + c_ref[...]
    z = jax.nn.gelu(z)
    o_ref[...] = jnp.dot(z.astype(BF16), w2_ref[...], preferred_element_type=F32)


def _compress(kv6, cmp_pe, cmp_w1, cmp_b1, cmp_w2):
    B, T = kv6.shape[:2]
    G, DH = NSA_KV_GROUPS, NSA_HEAD_DIM
    M = T // CMP_STRIDE
    half = CMP_STRIDE * DH
    r = kv6[:, :, 0:2].transpose(0, 2, 3, 1, 4).reshape(B, 2, G, M, half)
    w1 = cmp_w1.reshape(2, 2, half, CMP_HIDDEN)
    w1c = jnp.concatenate([w1[:, 0], w1[:, 1]], axis=-1).astype(BF16)
    cvec = (jnp.einsum('wld,wldk->wk', cmp_pe, cmp_w1, precision=lax.Precision.HIGHEST)
            + cmp_b1).reshape(2, 1, CMP_HIDDEN)
    return pl.pallas_call(
        _cmp_body,
        grid=(B, 2, G),
        in_specs=[pl.BlockSpec((None, None, None, M, half), lambda b, w, g: (b, w, g, 0, 0)),
                  pl.BlockSpec((None, half, 2 * CMP_HIDDEN), lambda b, w, g: (w, 0, 0)),
                  pl.BlockSpec((None, 1, CMP_HIDDEN), lambda b, w, g: (w, 0, 0)),
                  pl.BlockSpec((None, CMP_HIDDEN, DH), lambda b, w, g: (w, 0, 0))],
        out_specs=pl.BlockSpec((None, None, None, M, DH), lambda b, w, g: (b, w, g, 0, 0)),
        out_shape=jax.ShapeDtypeStruct((B, 2, G, M, DH), F32),
        compiler_params=_cparams(("arbitrary",) * 3),
        name="kv_compress",
    )(r, w1c, cvec, cmp_w2.astype(BF16))


def _nsa_cmp_body(qt_ref, kc_ref, vct_ref, ct_ref, ovl_ref, oc_ref, sel_ref, s_ref,
                  *, n_blk, topn):
    i = pl.program_id(2)
    qt = qt_ref[...]
    s_ref[...] = jnp.dot(kc_ref[...], qt, preferred_element_type=F32)
    near = pl.ds(pl.multiple_of(i * 8, 8), CMP_NEAR)
    s_ref[near, :] = s_ref[near, :] + ct_ref[...]
    s = s_ref[...]
    MP = s.shape[0]
    row = lax.broadcasted_iota(I32, (MP, 1), 0)
    s = jnp.where(row < i * 8 + CMP_NEAR, s, NEG_INF)
    m = jnp.maximum(jnp.max(s, axis=0, keepdims=True), M_FLOOR)
    p = jnp.exp(s - m)
    l = jnp.sum(p, axis=0, keepdims=True)
    p = p / jnp.maximum(l, 1e-30)
    pb = p.astype(BF16)
    oc_ref[...] = jnp.dot(vct_ref[...], pb, preferred_element_type=F32)[:NSA_HEAD_DIM]
    psum = p[:, 0:TQ]
    for hh in range(1, NSA_HPG):
        psum = psum + p[:, hh * TQ:(hh + 1) * TQ]
    imp = jnp.dot(ovl_ref[...], psum.astype(BF16), preferred_element_type=F32)

    blk = lax.broadcasted_iota(I32, (LANE, 1), 0)
    t = i * TQ + lax.broadcasted_iota(I32, (1, TQ), 1)
    cur = t >> int(math.log2(SLC_BLOCK))
    admiss = blk <= cur
    bonus = jnp.where(blk == 0, FORCE_BONUS, 0.0)
    bonus = jnp.where(blk == cur, FORCE_BONUS, bonus)
    bonus = jnp.where(blk == cur - 1, FORCE_BONUS, bonus)
    score = jnp.where(admiss, imp + bonus, NEG_INF)
    score = jnp.where(blk < n_blk, score, KNOCK)
    blk_b = jnp.broadcast_to(blk, (LANE, TQ))
    selneg = jnp.full((LANE, TQ), NEG_INF, F32)
    for _ in range(topn):
        mx = jnp.max(score, axis=0, keepdims=True)
        first = jnp.min(jnp.where(score == mx, blk_b, LANE), axis=0, keepdims=True)
        pick = blk_b == first
        selneg = jnp.where(pick, 0.0, selneg)
        score = jnp.where(pick, KNOCK, score)
    sel_ref[...] = jnp.where(admiss, selneg, NEG_INF).astype(BF16)


def _nsa_compressed(qt, kc_aug, vct, ct, ovl, n_blk):
    B, G, nQ = qt.shape[:3]
    MP = kc_aug.shape[2]
    body = functools.partial(_nsa_cmp_body, n_blk=n_blk, topn=min(SLC_TOPN, n_blk))
    return pl.pallas_call(
        body,
        grid=(B, G, nQ),
        in_specs=[pl.BlockSpec((None, None, None, LANE, NSA_HPG * TQ), lambda b, g, i: (b, g, i, 0, 0)),
                  pl.BlockSpec((None, None, MP, LANE), lambda b, g, i: (b, g, 0, 0)),
                  pl.BlockSpec((None, None, LANE, MP), lambda b, g, i: (b, g, 0, 0)),
                  pl.BlockSpec((None, CMP_NEAR, NSA_HPG * TQ), lambda b, g, i: (g, 0, 0)),
                  pl.BlockSpec((LANE, MP), lambda b, g, i: (0, 0))],
        out_specs=[pl.BlockSpec((None, None, None, NSA_HEAD_DIM, NSA_HPG * TQ), lambda b, g, i: (b, g, i, 0, 0)),
                   pl.BlockSpec((None, None, None, LANE, TQ), lambda b, g, i: (b, g, i, 0, 0))],
        out_shape=[jax.ShapeDtypeStruct((B, G, nQ, NSA_HEAD_DIM, NSA_HPG * TQ), F32),
                   jax.ShapeDtypeStruct((B, G, nQ, LANE, TQ), BF16)],
        scratch_shapes=[pltpu.VMEM((MP, NSA_HPG * TQ), F32)],
        compiler_params=_cparams(("arbitrary",) * 3),
        name="nsa_compressed",
    )(qt, kc_aug, vct, ct, ovl)


def _softmax_step(s, vt, m_old, acc):
    m_new = jnp.maximum(m_old, jnp.max(s, axis=0, keepdims=True))
    alpha = jnp.exp(m_old - m_new)
    p = jnp.exp(s - m_new).astype(BF16)
    return m_new, acc * alpha + jnp.dot(vt, p, preferred_element_type=F32)


AK = 256


def _pipeline(lo, hi, produce, consume, buf_a, buf_b):
    n = hi - lo

    @pl.when(n > 0)
    def _():
        produce(lo, buf_a)

        def body(p, c):
            j = lo + 2 * p
            produce(j + 1, buf_b)
            consume(j, buf_a)
            produce(j + 2, buf_a)
            consume(j + 1, buf_b)
            return c

        lax.fori_loop(0, (n - 1) // 2, body, 0)

        @pl.when(n % 2 == 1)
        def _():
            consume(hi - 1, buf_a)

        @pl.when(n % 2 == 0)
        def _():
            produce(hi - 1, buf_b)
            consume(hi - 2, buf_a)
            consume(hi - 1, buf_b)


def _attend(lo, hi, lhs_fn, rhs_ref, value_fn, m_ref, acc_ref, sa_ref, sb_ref, near_fn=None):
    def scores(j, buf):
        s = jnp.dot(lhs_fn(j), rhs_ref[...], preferred_element_type=F32)
        buf[...] = s if near_fn is None else near_fn(s, j)

    def consume(j, buf):
        m, acc = _softmax_step(buf[...], value_fn(j), m_ref[...], acc_ref[...])
        m_ref[...] = m
        acc_ref[...] = acc

    _pipeline(lo, hi, scores, consume, sa_ref, sb_ref)


def _near_add(dt_ref, i, s, j):
    parts = []
    for sub in range(AK // TQ):
        idx = jnp.clip(i - (j * (AK // TQ) + sub) + 1, 0, 4)
        parts.append(s[sub * TQ:(sub + 1) * TQ] + dt_ref[idx])
    return jnp.concatenate(parts, axis=0)


def _nsa_sel_body(qt_ref, sel_ref, ks_ref, vst_ref, dt_ref, o_ref, qa_ref, m_ref, acc_ref,
                  sa_ref, sb_ref):
    i = pl.program_id(2)
    L = NSA_HPG * TQ
    qa_ref[0:LANE, :] = qt_ref[...]
    selneg = sel_ref[...]
    qa_ref[LANE:2 * LANE, :] = jnp.concatenate([selneg] * NSA_HPG, axis=1)
    m_ref[...] = jnp.full((1, L), M_FLOOR, F32)
    acc_ref[...] = jnp.zeros(acc_ref.shape, F32)
    last = i // (AK // TQ)
    n_far = jnp.maximum(last - 1, 0)

    def keys(j):
        return ks_ref[pl.ds(pl.multiple_of(j * AK, AK), AK), :]

    def value(j):
        return vst_ref[:, pl.ds(pl.multiple_of(j * AK, AK), AK)]

    _attend(0, n_far, keys, qa_ref, value, m_ref, acc_ref, sa_ref, sb_ref)
    _attend(n_far, last + 1, keys, qa_ref, value, m_ref, acc_ref, sa_ref, sb_ref,
            near_fn=functools.partial(_near_add, dt_ref, i))
    acc = acc_ref[...]
    o_ref[...] = acc[:NSA_HEAD_DIM] / acc[NSA_HEAD_DIM:NSA_HEAD_DIM + 1]


def _nsa_selected(qt, selneg, ks_aug, vst, dt):
    B, G, nQ = qt.shape[:3]
    T = ks_aug.shape[2]
    L = NSA_HPG * TQ
    return pl.pallas_call(
        _nsa_sel_body,
        grid=(B, G, nQ),
        in_specs=[pl.BlockSpec((None, None, None, LANE, L), lambda b, g, i: (b, g, i, 0, 0)),
                  pl.BlockSpec((None, None, None, LANE, TQ), lambda b, g, i: (b, g, i, 0, 0)),
                  pl.BlockSpec((None, None, T, 2 * LANE), lambda b, g, i: (b, g, 0, 0)),
                  pl.BlockSpec((None, None, LANE, T), lambda b, g, i: (b, g, 0, 0)),
                  pl.BlockSpec((None, 5, TQ, L), lambda b, g, i: (g, 0, 0, 0))],
        out_specs=pl.BlockSpec((None, None, None, NSA_HEAD_DIM, L), lambda b, g, i: (b, g, i, 0, 0)),
        out_shape=jax.ShapeDtypeStruct((B, G, nQ, NSA_HEAD_DIM, L), F32),
        scratch_shapes=[pltpu.VMEM((2 * LANE, L), BF16),
                        pltpu.VMEM((1, L), F32),
                        pltpu.VMEM((LANE, L), F32),
                        pltpu.VMEM((AK, L), F32),
                        pltpu.VMEM((AK, L), F32)],
        compiler_params=_cparams(("arbitrary",) * 3),
        name="nsa_selected",
    )(qt, selneg, ks_aug, vst, dt)


def _nsa_win_body(qt_ref, kw_ref, vwt_ref, w0_ref, dt_ref, oc_ref, os_ref, gl_ref, o_ref):
    i = pl.program_id(2)
    span = WINDOW + TQ
    k0 = pl.multiple_of(i * TQ, TQ)
    s = jnp.dot(kw_ref[pl.ds(k0, span), :], qt_ref[...], preferred_element_type=F32)
    s = jnp.concatenate([s[0:TQ] + w0_ref[...], s[TQ:WINDOW - TQ],
                         s[WINDOW - TQ:WINDOW] + dt_ref[2], s[WINDOW:span] + dt_ref[1]], axis=0)
    m = jnp.maximum(jnp.max(s, axis=0, keepdims=True), M_FLOOR)
    p = jnp.exp(s - m).astype(BF16)
    acc = jnp.dot(vwt_ref[:, pl.ds(k0, span)], p, preferred_element_type=F32)
    ow = acc[:NSA_HEAD_DIM] / acc[NSA_HEAD_DIM:NSA_HEAD_DIM + 1]
    g = jax.nn.sigmoid(gl_ref[...])
    o_ref[...] = g[0:1] * oc_ref[...] + g[1:2] * os_ref[...] + g[2:3] * ow


def _nsa_window(qt, kw_aug, vwt, w0, dt, oc, osel, gl):
    B, G, nQ = qt.shape[:3]
    TP = kw_aug.shape[2]
    L = NSA_HPG * TQ
    tile = lambda r: pl.BlockSpec((None, None, None, r, L), lambda b, g, i: (b, g, i, 0, 0))
    return pl.pallas_call(
        _nsa_win_body,
        grid=(B, G, nQ),
        in_specs=[tile(LANE),
                  pl.BlockSpec((None, None, TP, LANE), lambda b, g, i: (b, g, 0, 0)),
                  pl.BlockSpec((None, None, LANE, TP), lambda b, g, i: (b, g, 0, 0)),
                  pl.BlockSpec((TQ, L), lambda b, g, i: (0, 0)),
                  pl.BlockSpec((None, 5, TQ, L), lambda b, g, i: (g, 0, 0, 0)),
                  tile(NSA_HEAD_DIM), tile(NSA_HEAD_DIM), tile(8)],
        out_specs=tile(NSA_HEAD_DIM),
        out_shape=jax.ShapeDtypeStruct((B, G, nQ, NSA_HEAD_DIM, L), F32),
        compiler_params=_cparams(("arbitrary",) * 3),
        name="nsa_window",
    )(qt, kw_aug, vwt, w0, dt, oc, osel, gl)


INT_MIN = -2 ** 31
CK = 2 * NK
KEY_BITS = 16


def _dsa_body(qdt_ref, qit_ref, wt_ref, ki_ref, ckv_ref, ckvt_ref, wuk_ref, wuvt_ref, dt_ref,
              o_ref, sc_ref, qb_ref, m_ref, acc_ref, sa_ref, sb_ref, *, k_top):
    i = pl.program_id(1)
    H = DSA_HEADS
    L = H * TQ
    last = i // (NK // TQ)
    n_steps = last + 1
    t_row = i * TQ + lax.broadcasted_iota(I32, (1, TQ), 1)

    qit = qit_ref[...]
    wt = wt_ref[...]

    def idx_scores(j, buf):
        k0 = pl.multiple_of(j * NK, NK)
        buf[...] = jnp.dot(ki_ref[pl.ds(k0, NK), :], qit, preferred_element_type=F32)

    def idx_reduce(j, buf):
        k0 = pl.multiple_of(j * NK, NK)
        r = jnp.maximum(buf[...], 0.0) * wt
        sc = r[:, 0:TQ]
        for h in range(1, H):
            sc = sc + r[:, h * TQ:(h + 1) * TQ]
        bits = pltpu.bitcast(sc, I32)
        ik = jnp.where(sc == 0.0, 0, bits ^ ((bits >> 31) & 0x7FFFFFFF))
        key = k0 + lax.broadcasted_iota(I32, (NK, 1), 0)
        sc_ref[pl.ds(k0, NK), :] = jnp.where(key <= t_row, jnp.maximum(ik, INT_MIN + 1), INT_MIN)

    _pipeline(0, n_steps, idx_scores, idx_reduce, sa_ref, sb_ref)

    @pl.when(n_steps % 2 == 1)
    def _():
        sc_ref[pl.ds(pl.multiple_of(n_steps * NK, NK), NK), :] = jnp.full((NK, TQ), INT_MIN, I32)

    n_chunks = (n_steps + 1) // 2

    def count(pred):
        def cbody(j, cnt):
            k0 = pl.multiple_of(j * CK, CK)
            ind = jnp.where(pred(sc_ref[pl.ds(k0, CK), :], k0), 1.0, 0.0)
            parts = [ind[r * 8:(r + 1) * 8] for r in range(CK // 8)]
            while len(parts) > 1:
                parts = [a + b for a, b in zip(parts[0::2], parts[1::2])]
            return cnt + parts[0]
        cnt8 = lax.fori_loop(0, n_chunks, cbody, jnp.zeros((8, TQ), F32))
        return jnp.sum(cnt8, axis=0, keepdims=True)

    def any_lane(cond):
        return jnp.max(jnp.where(cond, 1.0, 0.0)) > 0.0

    kf = float(k_top)
    need = t_row + 1 > k_top

    def bit_cond(c):
        b, _tu, ct = c
        return (b < 32) & any_lane(need & (ct != kf))

    def bit_body(c):
        b, tu, ct = c
        cand = tu | jnp.left_shift(jnp.int32(1), 31 - b)
        thr_c = cand ^ INT_MIN
        cnt = count(lambda blk, k0: blk >= thr_c)
        take = cnt >= kf
        return b + 1, jnp.where(take, cand, tu), jnp.where(take, cnt, ct)

    _, tu, ct = lax.while_loop(bit_cond, bit_body,
                               (jnp.int32(0), jnp.zeros((1, TQ), I32), (t_row + 1).astype(F32)))
    thr = jnp.maximum(tu ^ INT_MIN, INT_MIN + 1)

    @pl.when(any_lane(need & (ct > kf)))
    def _():
        keep = kf - count(lambda blk, k0: blk > thr)

        def tie_count(bound):
            def pred(blk, k0):
                key = k0 + lax.broadcasted_iota(I32, (CK, 1), 0)
                return jnp.where(key < bound, blk, INT_MIN) == thr
            return count(pred)

        def pos_body(b, pos):
            cand = pos | jnp.left_shift(jnp.int32(1), KEY_BITS - 1 - b)
            return jnp.where(tie_count(cand) < keep, cand, pos)

        pos = lax.fori_loop(0, KEY_BITS, pos_body, jnp.zeros((1, TQ), I32))

        def fix(j, c):
            k0 = pl.multiple_of(j * CK, CK)
            blk = sc_ref[pl.ds(k0, CK), :]
            key = k0 + lax.broadcasted_iota(I32, (CK, 1), 0)
            drop = jnp.where(key > pos, blk, INT_MIN) == thr
            sc_ref[pl.ds(k0, CK), :] = jnp.where(drop, INT_MIN, blk)
            return c

        lax.fori_loop(0, n_chunks, fix, 0)

    qdt = qdt_ref[...]
    for h in range(H):
        ql = jnp.dot(wuk_ref[h], qdt[:, h * TQ:(h + 1) * TQ], preferred_element_type=F32)
        qb_ref[0:LANE, h * TQ:(h + 1) * TQ] = (ql * DSA_HEAD_DIM ** -0.5).astype(BF16)
    eye = (lax.broadcasted_iota(I32, (LANE, TQ), 0) == lax.broadcasted_iota(I32, (LANE, TQ), 1))
    eye = jnp.where(eye, 1.0, 0.0).astype(BF16)
    qb_ref[LANE:2 * LANE, :] = jnp.concatenate([eye] * H, axis=1)
    m_ref[...] = jnp.full((1, L), M_FLOOR, F32)
    acc_ref[...] = jnp.zeros(acc_ref.shape, F32)
    alast = i // (AK // TQ)
    n_far = jnp.maximum(alast - 1, 0)

    def keys(j):
        k0 = pl.multiple_of(j * AK, AK)
        nb = jnp.where(sc_ref[pl.ds(k0, AK), :] >= thr, 0.0, NEG_INF).astype(BF16)
        return jnp.concatenate([ckv_ref[pl.ds(k0, AK), :], nb], axis=1)

    def value(j):
        return ckvt_ref[:, pl.ds(pl.multiple_of(j * AK, AK), AK)]

    _attend(0, n_far, keys, qb_ref, value, m_ref, acc_ref, sa_ref, sb_ref)
    _attend(n_far, alast + 1, keys, qb_ref, value, m_ref, acc_ref, sa_ref, sb_ref,
            near_fn=functools.partial(_near_add, dt_ref, i))
    acc = acc_ref[...]
    olat = (acc[:DSA_KV_LATENT] / acc[DSA_KV_LATENT:DSA_KV_LATENT + 1]).astype(BF16)
    for h in range(H):
        o_ref[:, h * TQ:(h + 1) * TQ] = jnp.dot(wuvt_ref[h], olat[:, h * TQ:(h + 1) * TQ],
                                                preferred_element_type=F32)


def _dsa(qdt, qit, wt, ki, ckv, ckvt, wuk, wuvt, dt, k_top):
    B, nQ = qdt.shape[:2]
    T = ki.shape[1]
    H = DSA_HEADS
    L = H * TQ
    R = ckvt.shape[1]
    tile = lambda r: pl.BlockSpec((None, None, r, L), lambda b, i: (b, i, 0, 0))
    return pl.pallas_call(
        functools.partial(_dsa_body, k_top=k_top),
        grid=(B, nQ),
        in_specs=[tile(DSA_HEAD_DIM), tile(LANE), tile(1),
                  pl.BlockSpec((None, T, LANE), lambda b, i: (b, 0, 0)),
                  pl.BlockSpec((None, T, DSA_KV_LATENT), lambda b, i: (b, 0, 0)),
                  pl.BlockSpec((None, R, T), lambda b, i: (b, 0, 0)),
                  pl.BlockSpec((H, DSA_KV_LATENT, DSA_HEAD_DIM), lambda b, i: (0, 0, 0)),
                  pl.BlockSpec((H, DSA_HEAD_DIM, DSA_KV_LATENT), lambda b, i: (0, 0, 0)),
                  pl.BlockSpec((5, TQ, L), lambda b, i: (0, 0, 0))],
        out_specs=tile(DSA_HEAD_DIM),
        out_shape=jax.ShapeDtypeStruct((B, nQ, DSA_HEAD_DIM, L), F32),
        scratch_shapes=[pltpu.VMEM((T, TQ), I32),
                        pltpu.VMEM((2 * LANE, L), BF16),
                        pltpu.VMEM((1, L), F32),
                        pltpu.VMEM((R, L), F32),
                        pltpu.VMEM((AK, L), F32),
                        pltpu.VMEM((AK, L), F32)],
        compiler_params=_cparams(("arbitrary",) * 2),
        name="dsa_attention",
    )(qdt, qit, wt, ki, ckv, ckvt, wuk, wuvt, dt)


def _heads_on_lanes(a, B, nQ, nh, dh):
    return a.reshape(B, nQ, TQ, nh, dh).transpose(0, 1, 4, 3, 2).reshape(B, nQ, dh, nh * TQ)


def _heads_off_lanes(a, B, nQ, nh, dh):
    return a.reshape(B, nQ, dh, nh, TQ).transpose(0, 1, 4, 3, 2).reshape(B, nQ * TQ, nh * dh)


def _token_mixers(qn, kvn, qd, ckv, qi, small, rel_bias, cmp_pe, cmp_w1, cmp_b1, cmp_w2,
                  dsa_w_uk, dsa_w_uv):
    B, T = qn.shape[:2]
    G, HPG, DH = NSA_KV_GROUPS, NSA_HPG, NSA_HEAD_DIM
    nQ = T // TQ
    n_blk = T // SLC_BLOCK
    L = HPG * TQ
    assert T % NK == 0 and n_blk <= LANE
    kv6 = kvn.reshape(B, T, 6, G, DH)

    qt = qn.reshape(B, nQ, TQ, G, HPG, DH).transpose(0, 3, 1, 5, 4, 2).reshape(B, G, nQ, DH, L)
    qt = jnp.concatenate([qt, jnp.ones((B, G, nQ, 1, L), BF16),
                          jnp.zeros((B, G, nQ, LANE - DH - 1, L), BF16)], axis=3)

    tab_nsa = rel_bias[:, :NSA_HEADS]
    dt_nsa = _near_tables(tab_nsa).reshape(5, TQ, G, L).transpose(2, 0, 1, 3)
    dt_dsa = _near_tables(rel_bias[:, NSA_HEADS:])

    kvc = _compress(kv6, cmp_pe, cmp_w1, cmp_b1, cmp_w2)
    M = T // CMP_STRIDE
    MP = M + CMP_PAD
    kc = kvc[:, 0].astype(BF16)
    flag = jnp.concatenate([jnp.full((CMP_PAD, 1), NEG_INF, F32), jnp.zeros((M, 1), F32)]).astype(BF16)
    kc_aug = jnp.concatenate(
        [jnp.pad(kc, ((0, 0), (0, 0), (CMP_PAD, 0), (0, 0))),
         jnp.broadcast_to(flag, (B, G, MP, 1)),
         jnp.zeros((B, G, MP, LANE - DH - 1), BF16)], axis=-1)
    vct = jnp.pad(kvc[:, 1].astype(BF16).transpose(0, 1, 3, 2),
                  ((0, 0), (0, 0), (0, LANE - DH), (CMP_PAD, 0)))
    mrow = jnp.arange(CMP_NEAR)[:, None]
    qcol = jnp.arange(TQ)[None, :]
    ct = _delta(tab_nsa, qcol - CMP_STRIDE * mrow + (2 * TQ - CMP_BLOCK + 1))
    ct = ct.reshape(CMP_NEAR, TQ, G, HPG).transpose(2, 0, 3, 1).reshape(G, CMP_NEAR, L)
    n_cmp = (T - CMP_BLOCK) // CMP_STRIDE + 1
    cs = jnp.arange(M) * CMP_STRIDE
    ss = jnp.arange(LANE) * SLC_BLOCK
    ovl = (jnp.clip(jnp.minimum(cs[None, :] + CMP_BLOCK, ss[:, None] + SLC_BLOCK)
                    - jnp.maximum(cs[None, :], ss[:, None]), 0, None).astype(F32) / CMP_BLOCK)
    ovl = jnp.where((jnp.arange(M)[None, :] < n_cmp) & (jnp.arange(LANE)[:, None] < n_blk), ovl, 0.0)
    ovl = jnp.pad(ovl, ((0, 0), (CMP_PAD, 0))).astype(BF16)
    oc, selneg = _nsa_compressed(qt, kc_aug, vct, ct, ovl, n_blk)

    ks = kv6[:, :, 2].transpose(0, 2, 1, 3)
    et = (jnp.arange(T)[:, None] // SLC_BLOCK == jnp.arange(LANE)[None, :]).astype(BF16)
    ks_aug = jnp.concatenate([ks, jnp.zeros((B, G, T, LANE - DH), BF16),
                              jnp.broadcast_to(et, (B, G, T, LANE))], axis=-1)
    vs = kv6[:, :, 3].transpose(0, 2, 3, 1)
    vst = jnp.concatenate([vs, jnp.ones((B, G, 1, T), BF16),
                           jnp.zeros((B, G, LANE - DH - 1, T), BF16)], axis=2)
    osel = _nsa_selected(qt, selneg, ks_aug, vst, dt_nsa)

    TP = T + WINDOW
    kw = jnp.pad(kv6[:, :, 4].transpose(0, 2, 1, 3), ((0, 0), (0, 0), (WINDOW, 0), (0, 0)))
    wflag = jnp.concatenate([jnp.full((WINDOW, 1), NEG_INF, F32), jnp.zeros((T, 1), F32)]).astype(BF16)
    kw_aug = jnp.concatenate([kw, jnp.broadcast_to(wflag, (B, G, TP, 1)),
                              jnp.zeros((B, G, TP, LANE - DH - 1), BF16)], axis=-1)
    vw = jnp.pad(kv6[:, :, 5].transpose(0, 2, 3, 1), ((0, 0), (0, 0), (0, 0), (WINDOW, 0)))
    vwt = jnp.concatenate([vw, jnp.ones((B, G, 1, TP), BF16),
                           jnp.zeros((B, G, LANE - DH - 1, TP), BF16)], axis=2)
    u = jnp.arange(TQ)[:, None]
    w0 = jnp.tile(jnp.where(u > jnp.arange(TQ)[None, :], 0.0, NEG_INF).astype(F32), (1, HPG))
    gl = small[..., 0:3 * NSA_HEADS].reshape(B, nQ, TQ, G, HPG, 3).transpose(0, 3, 1, 5, 4, 2)
    gl = jnp.pad(gl.reshape(B, G, nQ, 3, L), ((0, 0), (0, 0), (0, 0), (0, 5), (0, 0)))
    o_nsa_t = _nsa_window(qt, kw_aug, vwt, w0, dt_nsa, oc, osel, gl)
    o_nsa = o_nsa_t.reshape(B, G, nQ, DH, HPG, TQ).transpose(0, 2, 5, 1, 4, 3).reshape(B, T, NSA_HEADS * DH)

    H = DSA_HEADS
    qdt = _heads_on_lanes(qd, B, nQ, H, DSA_HEAD_DIM)
    qit = _heads_on_lanes(qi, B, nQ, IDX_HEADS, IDX_DIM)
    qit = jnp.pad(qit, ((0, 0), (0, 0), (0, LANE - IDX_DIM), (0, 0)))
    k_idx = small[..., 24:24 + IDX_DIM]
    w_idx = small[..., 24 + IDX_DIM:24 + IDX_DIM + IDX_HEADS] * IDX_HEADS ** -0.5
    ki = jnp.pad(k_idx.astype(BF16), ((0, 0), (0, 0), (0, LANE - IDX_DIM)))
    wt = w_idx.reshape(B, nQ, TQ, IDX_HEADS).transpose(0, 1, 3, 2).reshape(B, nQ, 1, IDX_HEADS * TQ)
    ckvt = jnp.concatenate([ckv.transpose(0, 2, 1), jnp.ones((B, 1, T), BF16),
                            jnp.zeros((B, 7, T), BF16)], axis=1)
    wuk = dsa_w_uk.transpose(1, 0, 2).astype(BF16)
    wuvt = dsa_w_uv.transpose(1, 2, 0).astype(BF16)
    o_dsa_t = _dsa(qdt, qit, wt, ki, ckv, ckvt, wuk, wuvt, dt_dsa, min(IDX_TOPK_MAX, T // 4))
    o_dsa = _heads_off_lanes(o_dsa_t, B, nQ, H, DSA_HEAD_DIM)
    return o_nsa, o_dsa


def _rms(v, g):
    return v * lax.rsqrt(jnp.mean(v * v, axis=-1, keepdims=True) + RMS_EPS) * g


def _merge_body(on_ref, od_ref, gm_ref, x_ref, wn_ref, wd_ref, wo_ref, gpost_ref, gt_ref,
                gpre_ref, sc_ref, sh_ref, rwt_ref, x1_ref, h2_ref, lt_ref):
    D = D_MODEL
    ya = jnp.dot(on_ref[...], wn_ref[...], preferred_element_type=F32)
    yb = jnp.dot(od_ref[...], wd_ref[...], preferred_element_type=F32)
    y = jax.nn.sigmoid(gm_ref[:, 0:D]) * ya + jax.nn.sigmoid(gm_ref[:, D:2 * D]) * yb
    y2 = jnp.dot(y.astype(BF16), wo_ref[...], preferred_element_type=F32)
    x1 = x_ref[...] + gt_ref[...] * _rms(y2, gpost_ref[...])
    x1_ref[...] = x1
    h2 = (_rms(x1, gpre_ref[...]) * (1.0 + sc_ref[...]) + sh_ref[...]).astype(BF16)
    h2_ref[...] = h2
    lt_ref[...] = lax.dot_general(rwt_ref[...], h2, (((1,), (1,)), ((), ())),
                                  preferred_element_type=F32)


def _merge(o_nsa, o_dsa, gm, x, wn, wd, wo, g_post, gt1, g_pre, sc2, sh2, router_w):
    B, T, D = x.shape
    TM = 256
    E = router_w.shape[1]
    row = lambda w: pl.BlockSpec((None, TM, w), lambda b, i: (b, i, 0))
    full = lambda a: pl.BlockSpec(a.shape, lambda b, i: (0,) * a.ndim)
    vecb = pl.BlockSpec((None, 1, D), lambda b, i: (b, 0, 0))
    vec = pl.BlockSpec((1, D), lambda b, i: (0, 0))
    wn, wd, wo = wn.astype(BF16), wd.astype(BF16), wo.astype(BF16)
    rwt = router_w.T.astype(BF16)
    return pl.pallas_call(
        _merge_body,
        grid=(B, T // TM),
        in_specs=[row(o_nsa.shape[-1]), row(o_dsa.shape[-1]), row(2 * D), row(D),
                  full(wn), full(wd), full(wo), vec, vecb, vec, vecb, vecb, full(rwt)],
        out_specs=[row(D), row(D), pl.BlockSpec((None, E, TM), lambda b, i: (b, 0, i))],
        out_shape=[jax.ShapeDtypeStruct((B, T, D), F32),
                   jax.ShapeDtypeStruct((B, T, D), BF16),
                   jax.ShapeDtypeStruct((B, E, T), F32)],
        compiler_params=_cparams(("arbitrary",) * 2),
        name="merge_postnorm",
    )(o_nsa, o_dsa, gm, x, wn, wd, wo, g_post.reshape(1, D), gt1.reshape(B, 1, D),
      g_pre.reshape(1, D), sc2.reshape(B, 1, D), sh2.reshape(B, 1, D), rwt)


TR = 256
GSZ = N_EXPERTS // N_EXPERT_GROUPS


def _first_max(v, idx, big):
    mx = jnp.max(v, axis=0, keepdims=True)
    return jnp.min(jnp.where(v == mx, idx, big), axis=0, keepdims=True), mx


def _route_body(lt_ref, bias_ref, eid_ref, rk_ref, wt_ref, cnt_ref, carry_ref):
    first_step = (pl.program_id(0) == 0) & (pl.program_id(1) == 0)

    @pl.when(first_step)
    def _():
        carry_ref[...] = jnp.zeros(carry_ref.shape, F32)

    s = jax.nn.sigmoid(lt_ref[...])
    sel = s + bias_ref[:, 0:1]
    grow = lax.broadcasted_iota(I32, (GSZ, TR), 0)
    gs = []
    for g in range(N_EXPERT_GROUPS):
        blk = sel[g * GSZ:(g + 1) * GSZ]
        f1, m1 = _first_max(blk, grow, GSZ)
        m2 = jnp.max(jnp.where(grow == f1, KNOCK, blk), axis=0, keepdims=True)
        gs.append(m1 + m2)
    gsc = jnp.concatenate(gs, axis=0)
    gidx = lax.broadcasted_iota(I32, (N_EXPERT_GROUPS, TR), 0)
    gsel = jnp.zeros((N_EXPERT_GROUPS, TR), F32)
    for _ in range(TOPK_GROUPS):
        f, _mx = _first_max(gsc, gidx, N_EXPERT_GROUPS)
        pick = gidx == f
        gsel = jnp.where(pick, 1.0, gsel)
        gsc = jnp.where(pick, KNOCK, gsc)
    cur = jnp.concatenate(
        [jnp.where(gsel[g:g + 1] > 0.0, sel[g * GSZ:(g + 1) * GSZ], NEG_INF)
         for g in range(N_EXPERT_GROUPS)], axis=0)
    erow = lax.broadcasted_iota(I32, (N_EXPERTS, TR), 0)
    member = jnp.zeros((N_EXPERTS, TR), F32)
    picks, eids, ws = [], [], []
    for _ in range(TOP_K):
        f, _mx = _first_max(cur, erow, N_EXPERTS)
        pick = erow == f
        picks.append(pick)
        eids.append(f)
        ws.append(jnp.sum(jnp.where(pick, s, 0.0), axis=0, keepdims=True))
        member = jnp.where(pick, 1.0, member)
        cur = jnp.where(pick, KNOCK, cur)
    wsum = ws[0]
    for k in range(1, TOP_K):
        wsum = wsum + ws[k]
    mb = member.astype(BF16)
    upper = (lax.broadcasted_iota(I32, (TR, TR), 0) < lax.broadcasted_iota(I32, (TR, TR), 1))
    upper = jnp.where(upper, 1.0, 0.0).astype(BF16)
    rank = jnp.dot(mb, upper, preferred_element_type=F32) + carry_ref[:, 0:1]
    eid_ref[...] = jnp.concatenate(eids, axis=0)
    rk_ref[...] = jnp.concatenate(
        [jnp.sum(jnp.where(p, rank, 0.0), axis=0, keepdims=True) for p in picks], axis=0)
    wt_ref[...] = jnp.concatenate([w / wsum * ROUTED_SCALE for w in ws], axis=0)
    carry_ref[...] = carry_ref[...] + jnp.dot(mb, jnp.ones((TR, LANE), BF16),
                                              preferred_element_type=F32)
    cnt_ref[...] = carry_ref[...]


def _route(lt, router_bias):
    B, E, T = lt.shape
    out8 = lambda: pl.BlockSpec((None, TOP_K, TR), lambda b, i: (b, 0, i))
    return pl.pallas_call(
        _route_body,
        grid=(B, T // TR),
        in_specs=[pl.BlockSpec((None, E, TR), lambda b, i: (b, 0, i)),
                  pl.BlockSpec((E, 1), lambda b, i: (0, 0))],
        out_specs=[out8(), out8(), out8(), pl.BlockSpec((E, LANE), lambda b, i: (0, 0))],
        out_shape=[jax.ShapeDtypeStruct((B, TOP_K, T), I32),
                   jax.ShapeDtypeStruct((B, TOP_K, T), F32),
                   jax.ShapeDtypeStruct((B, TOP_K, T), F32),
                   jax.ShapeDtypeStruct((E, LANE), F32)],
        scratch_shapes=[pltpu.VMEM((E, LANE), F32)],
        compiler_params=_cparams(("arbitrary",) * 2),
        name="moe_router",
    )(lt, router_bias.reshape(E, 1))


BM = 256


def _gmm_body(be_ref, nu_ref, x_ref, wg_ref, wu_ref, wd_ref, y_ref, wgb, wub, wdb):
    b = pl.program_id(0)

    @pl.when(b < nu_ref[0])
    def _():
        prev = be_ref[jnp.maximum(b - 1, 0)]

        @pl.when((b == 0) | (be_ref[b] != prev))
        def _():
            wgb[...] = wg_ref[...].astype(BF16)
            wub[...] = wu_ref[...].astype(BF16)
            wdb[...] = wd_ref[...].astype(BF16)

        xb = x_ref[...]
        gt = jnp.dot(xb, wgb[...], preferred_element_type=F32)
        up = jnp.dot(xb, wub[...], preferred_element_type=F32)
        a = (gt * jax.nn.sigmoid(gt) * up).astype(BF16)
        y_ref[...] = jnp.dot(a, wdb[...], preferred_element_type=F32)


def _grouped_mlp(xs, blk_e, n_used, w_g, w_u, w_d):
    P, D = xs.shape
    nb = P // BM
    De = w_g.shape[-1]
    rows = lambda b, be, nu: (jnp.minimum(b, nu[0] - 1), 0)
    wsel = lambda b, be, nu: (be[b], 0, 0)
    grid_spec = pltpu.PrefetchScalarGridSpec(
        num_scalar_prefetch=2,
        grid=(nb,),
        in_specs=[pl.BlockSpec((BM, D), rows),
                  pl.BlockSpec((None, D, De), wsel),
                  pl.BlockSpec((None, D, De), wsel),
                  pl.BlockSpec((None, De, D), wsel)],
        out_specs=pl.BlockSpec((BM, D), rows),
        scratch_shapes=[pltpu.VMEM((D, De), BF16), pltpu.VMEM((D, De), BF16),
                        pltpu.VMEM((De, D), BF16)])
    return pl.pallas_call(
        _gmm_body,
        grid_spec=grid_spec,
        out_shape=jax.ShapeDtypeStruct((P, D), F32),
        compiler_params=_cparams(("arbitrary",)),
        name="moe_experts",
    )(blk_e, n_used, xs, w_g, w_u, w_d)


def _final_body(r_ref, h_ref, x1_ref, wg_ref, wu_ref, wd_ref, g_ref, gt_ref, o_ref):
    hb = h_ref[...]
    gt = jnp.dot(hb, wg_ref[...], preferred_element_type=F32)
    up = jnp.dot(hb, wu_ref[...], preferred_element_type=F32)
    a = (gt * jax.nn.sigmoid(gt) * up).astype(BF16)
    y = r_ref[...] + jnp.dot(a, wd_ref[...], preferred_element_type=F32)
    o_ref[...] = x1_ref[...] + gt_ref[...] * _rms(y, g_ref[...])


def _final(routed, h2, x1, ws_g, ws_u, ws_d, g_post, gt2):
    B, T, D = x1.shape
    TM = 512 if T % 512 == 0 else TQ
    row = pl.BlockSpec((None, TM, D), lambda b, i: (b, i, 0))
    full = lambda a: pl.BlockSpec(a.shape, lambda b, i: (0,) * a.ndim)
    ws_g, ws_u, ws_d = ws_g.astype(BF16), ws_u.astype(BF16), ws_d.astype(BF16)
    return pl.pallas_call(
        _final_body,
        grid=(B, T // TM),
        in_specs=[row, row, row, full(ws_g), full(ws_u), full(ws_d),
                  pl.BlockSpec((1, D), lambda b, i: (0, 0)),
                  pl.BlockSpec((None, 1, D), lambda b, i: (b, 0, 0))],
        out_specs=row,
        out_shape=jax.ShapeDtypeStruct((B, T, D), F32),
        compiler_params=_cparams(("arbitrary",) * 2),
        name="shared_final",
    )(routed, h2, x1, ws_g, ws_u, ws_d, g_post.reshape(1, D), gt2.reshape(B, 1, D))


def _moe(h2, lt, x1, router_bias, w_g, w_u, w_d, ws_g, ws_u, ws_d, g_post, gt2):
    B, T, D = x1.shape
    N = B * T
    eid, rk, wts, cnt = _route(lt, router_bias)
    counts = cnt[:, 0].astype(I32)
    padded = (counts + BM - 1) // BM * BM
    pend = jnp.cumsum(padded)
    pstart = pend - padded
    nb = (N * TOP_K + N_EXPERTS * (BM - 1) + BM - 1) // BM
    P = nb * BM
    blk_e = jnp.minimum(jnp.searchsorted(pend, jnp.arange(nb, dtype=I32) * BM, side='right'),
                        N_EXPERTS - 1).astype(I32)
    n_used = (pend[-1] // BM).astype(I32).reshape(1)
    eid_f = eid.transpose(0, 2, 1).reshape(N, TOP_K)
    slot = pstart[eid_f] + rk.transpose(0, 2, 1).reshape(N, TOP_K).astype(I32)
    w_f = wts.transpose(0, 2, 1).reshape(N, TOP_K)
    h2f = h2.reshape(N, D)
    xs = jnp.zeros((P, D), BF16).at[slot.reshape(-1)].set(jnp.repeat(h2f, TOP_K, axis=0))
    ys = _grouped_mlp(xs, blk_e, n_used, w_g, w_u, w_d)
    routed = jnp.sum(ys[slot] * w_f[:, :, None], axis=1)
    return _final(routed.reshape(B, T, D), h2, x1, ws_g, ws_u, ws_d, g_post, gt2)


def kernel(x, c, w_ada, b_ada, g_pre_mix, g_post_mix, g_pre_ffn, g_post_ffn, w_in, rel_bias, cmp_pe, cmp_w1, cmp_b1, cmp_w2, dsa_kv_norm, dsa_w_uk, dsa_w_uv, w_branch_nsa, w_branch_dsa, w_out, router_w, router_bias, w_exp_gate, w_exp_up, w_exp_down, w_sh_gate, w_sh_up, w_sh_down):
    depth = w_ada.shape[0]
    D = D_MODEL
    for l in range(depth):
        mod = _adaln(c, w_ada[l], b_ada[l])
        sh1, sc1, gt1, sh2, sc2, gt2 = [mod[:, k * D:(k + 1) * D] for k in range(N_MOD)]
        qn, kvn, qd, ckv, qi, small, gm = _input_proj(x, g_pre_mix[l], sc1, sh1, w_in[l], dsa_kv_norm[l])
        o_nsa, o_dsa = _token_mixers(qn, kvn, qd, ckv, qi, small, rel_bias, cmp_pe[l], cmp_w1[l],
                                     cmp_b1[l], cmp_w2[l], dsa_w_uk[l], dsa_w_uv[l])
        x1, h2, lt = _merge(o_nsa.astype(BF16), o_dsa.astype(BF16), gm, x, w_branch_nsa[l],
                            w_branch_dsa[l], w_out[l], g_post_mix[l], gt1, g_pre_ffn[l], sc2, sh2,
                            router_w[l])
        x = _moe(h2, lt, x1, router_bias[l], w_exp_gate[l], w_exp_up[l], w_exp_down[l],
                 w_sh_gate[l], w_sh_up[l], w_sh_down[l], g_post_ffn[l], gt2)
    return x
```

```python
import functools
import math

import jax
import jax.numpy as jnp
import numpy as np
from jax import lax
from jax.experimental import pallas as pl
from jax.experimental.pallas import tpu as pltpu
from jax.experimental.pallas import tpu_sc as plsc

F32 = jnp.float32
BF16 = jnp.bfloat16
I32 = jnp.int32

D_MODEL = 1024
N_MOD = 6
NSA_HEADS = 8
NSA_KV_GROUPS = 2
NSA_HPG = NSA_HEADS // NSA_KV_GROUPS
NSA_HEAD_DIM = 64
CMP_BLOCK = 32
CMP_STRIDE = 16
CMP_HIDDEN = 128
SLC_BLOCK = 64
SLC_TOPN = 16
WINDOW = 512
FORCE_BONUS = 1e4
DSA_HEADS = 8
DSA_HEAD_DIM = 64
DSA_KV_LATENT = 128
IDX_HEADS = 8
IDX_DIM = 64
IDX_TOPK_MAX = 256
REL_BUCKETS = 32
REL_MAX_DISTANCE = 128
N_EXPERTS = 256
N_EXPERT_GROUPS = 8
TOPK_GROUPS = 4
TOP_K = 8
D_EXPERT = 256
ROUTED_SCALE = 2.5
RMS_EPS = 1e-6
NEG_INF = -1e30

LANE = 128
TQ = 128
NK = 256
M_FLOOR = -1e29
KNOCK = -3e38
CMP_PAD = 16
CMP_NEAR = 24
VMEM_LIMIT = 56 * 1024 * 1024

PROJ_WIDTHS = (512, 768, 24, 512, 128, 512, 64, 8, 2048)
PROJ_OFFS = tuple(int(v) for v in np.cumsum((0,) + PROJ_WIDTHS))


def _cparams(sem):
    return pltpu.CompilerParams(dimension_semantics=sem, vmem_limit_bytes=VMEM_LIMIT)


def _mod_body(c_ref, w_ref, b_ref, o_ref):
    c = c_ref[...]
    s = c * jax.nn.sigmoid(c)
    o_ref[...] = jnp.dot(s, w_ref[...], preferred_element_type=F32) + b_ref[...]


def _adaln(c, w_ada, b_ada):
    B, D = c.shape
    E = w_ada.shape[1]
    cp = jnp.zeros((8, D), F32).at[:B].set(c)
    out = pl.pallas_call(
        _mod_body,
        grid=(E // D,),
        in_specs=[pl.BlockSpec((8, D), lambda j: (0, 0)),
                  pl.BlockSpec((D, D), lambda j: (0, j)),
                  pl.BlockSpec((1, D), lambda j: (0, j))],
        out_specs=pl.BlockSpec((8, D), lambda j: (0, j)),
        out_shape=jax.ShapeDtypeStruct((8, E), F32),
        compiler_params=_cparams(("arbitrary",)),
        name="adaln_mod",
    )(cp, w_ada, b_ada.reshape(1, E))
    return out[:B]


PW = (512, 768, 512, 128, 512, 128, 2048)
PO = tuple(int(v) for v in np.cumsum((0,) + PW))


def _proj_body(x_ref, g_ref, sc_ref, sh_ref, w_ref, kn_ref,
               qn_ref, kv_ref, qd_ref, ckv_ref, qi_ref, sm_ref, gm_ref):
    x = x_ref[...]
    ms = jnp.mean(x * x, axis=-1, keepdims=True)
    h = x * lax.rsqrt(ms + RMS_EPS) * g_ref[...]
    h = h * (1.0 + sc_ref[...]) + sh_ref[...]
    hb = h.astype(BF16)

    def mm(k):
        return jnp.dot(hb, w_ref[:, PO[k]:PO[k + 1]], preferred_element_type=F32)

    qn_ref[...] = (mm(0) * NSA_HEAD_DIM ** -0.5).astype(BF16)
    kv_ref[...] = mm(1).astype(BF16)
    qd_ref[...] = mm(2).astype(BF16)
    c = mm(3)
    cms = jnp.mean(c * c, axis=-1, keepdims=True)
    ckv_ref[...] = (c * lax.rsqrt(cms + RMS_EPS) * kn_ref[...]).astype(BF16)
    qi_ref[...] = (mm(4) * IDX_DIM ** -0.5).astype(BF16)
    sm_ref[...] = mm(5)
    gm_ref[...] = mm(6)


def _input_proj(x, g_pre, sc, sh, w_in, kv_norm):
    B, T, D = x.shape
    TM = 512 if T % 512 == 0 else TQ
    o = PROJ_OFFS
    small = jnp.concatenate([w_in[:, o[2]:o[3]], w_in[:, o[6]:o[7]], w_in[:, o[7]:o[8]],
                             jnp.zeros((D, 32), F32)], axis=1)
    w_r = jnp.concatenate([w_in[:, o[0]:o[1]], w_in[:, o[1]:o[2]], w_in[:, o[3]:o[4]],
                           w_in[:, o[4]:o[5]], w_in[:, o[5]:o[6]], small,
                           w_in[:, o[8]:o[9]]], axis=1).astype(BF16)
    row = lambda w: pl.BlockSpec((None, TM, w), lambda b, i: (b, i, 0))
    vec = pl.BlockSpec((None, 1, D), lambda b, i: (b, 0, 0))
    outs = pl.pallas_call(
        _proj_body,
        grid=(B, T // TM),
        in_specs=[row(D),
                  pl.BlockSpec((1, D), lambda b, i: (0, 0)),
                  vec, vec,
                  pl.BlockSpec((D, PO[-1]), lambda b, i: (0, 0)),
                  pl.BlockSpec((1, DSA_KV_LATENT), lambda b, i: (0, 0))],
        out_specs=[row(w) for w in PW],
        out_shape=[jax.ShapeDtypeStruct((B, T, PW[0]), BF16),
                   jax.ShapeDtypeStruct((B, T, PW[1]), BF16),
                   jax.ShapeDtypeStruct((B, T, PW[2]), BF16),
                   jax.ShapeDtypeStruct((B, T, PW[3]), BF16),
                   jax.ShapeDtypeStruct((B, T, PW[4]), BF16),
                   jax.ShapeDtypeStruct((B, T, PW[5]), F32),
                   jax.ShapeDtypeStruct((B, T, PW[6]), F32)],
        compiler_params=_cparams(("arbitrary", "arbitrary")),
        name="prenorm_proj",
    )(x, g_pre.reshape(1, D), sc.reshape(B, 1, D), sh.reshape(B, 1, D), w_r,
      kv_norm.reshape(1, DSA_KV_LATENT))
    return outs


def _t5_bucket(dist):
    n = jnp.maximum(dist, 0)
    exact = REL_BUCKETS // 2
    nf = jnp.maximum(n, 1).astype(F32)
    large = exact + (jnp.log(nf / exact) / math.log(REL_MAX_DISTANCE / exact)
                     * (REL_BUCKETS - exact)).astype(I32)
    return jnp.where(n < exact, n, jnp.minimum(large, REL_BUCKETS - 1))


def _delta(tab, dist):
    d = tab[_t5_bucket(dist)] - tab[REL_BUCKETS - 1]
    return jnp.where((dist >= 0)[..., None], d, NEG_INF)


def _near_tables(tab):
    H = tab.shape[1]
    c = jnp.arange(TQ)[:, None]
    q = jnp.arange(TQ)[None, :]
    tabs = [jnp.full((TQ, TQ, H), NEG_INF, F32)]
    for diff in (0, 1):
        tabs.append(_delta(tab, diff * TQ + q - c))
    tabs += [jnp.zeros((TQ, TQ, H), F32)] * 2
    t = jnp.stack(tabs)
    return t.transpose(0, 1, 3, 2).reshape(5, TQ, H * TQ)


def _cmp_body(r_ref, w1_ref, c_ref, w2_ref, o_ref):
    y = jnp.dot(r_ref[...], w1_ref[...], preferred_element_type=F32)
    M = y.shape[0]
    z = y[:, :CMP_HIDDEN] + pltpu.roll(y[:, CMP_HIDDEN:], M - 1, 0) + c_ref[...]
    z = jax.nn.gelu(z)
    o_ref[...] = jnp.dot(z.astype(BF16), w2_ref[...], preferred_element_type=F32)


def _compress(kv6, cmp_pe, cmp_w1, cmp_b1, cmp_w2):
    B, T = kv6.shape[:2]
    G, DH = NSA_KV_GROUPS, NSA_HEAD_DIM
    M = T // CMP_STRIDE
    half = CMP_STRIDE * DH
    r = kv6[:, :, 0:2].transpose(0, 2, 3, 1, 4).reshape(B, 2, G, M, half)
    w1 = cmp_w1.reshape(2, 2, half, CMP_HIDDEN)
    w1c = jnp.concatenate([w1[:, 0], w1[:, 1]], axis=-1).astype(BF16)
    cvec = (jnp.einsum('wld,wldk->wk', cmp_pe, cmp_w1, precision=lax.Precision.HIGHEST)
            + cmp_b1).reshape(2, 1, CMP_HIDDEN)
    return pl.pallas_call(
        _cmp_body,
        grid=(B, 2, G),
        in_specs=[pl.BlockSpec((None, None, None, M, half), lambda b, w, g: (b, w, g, 0, 0)),
                  pl.BlockSpec((None, half, 2 * CMP_HIDDEN), lambda b, w, g: (w, 0, 0)),
                  pl.BlockSpec((None, 1, CMP_HIDDEN), lambda b, w, g: (w, 0, 0)),
                  pl.BlockSpec((None, CMP_HIDDEN, DH), lambda b, w, g: (w, 0, 0))],
        out_specs=pl.BlockSpec((None, None, None, M, DH), lambda b, w, g: (b, w, g, 0, 0)),
        out_shape=jax.ShapeDtypeStruct((B, 2, G, M, DH), F32),
        compiler_params=_cparams(("arbitrary",) * 3),
        name="kv_compress",
    )(r, w1c, cvec, cmp_w2.astype(BF16))


def _nsa_cmp_body(qt_ref, kc_ref, vct_ref, ct_ref, ovl_ref, oc_ref, sel_ref, s_ref,
                  *, n_blk, topn):
    i = pl.program_id(2)
    qt = qt_ref[...]
    s_ref[...] = jnp.dot(kc_ref[...], qt, preferred_element_type=F32)
    near = pl.ds(pl.multiple_of(i * 8, 8), CMP_NEAR)
    s_ref[near, :] = s_ref[near, :] + ct_ref[...]
    s = s_ref[...]
    MP = s.shape[0]
    row = lax.broadcasted_iota(I32, (MP, 1), 0)
    s = jnp.where(row < i * 8 + CMP_NEAR, s, NEG_INF)
    m = jnp.maximum(jnp.max(s, axis=0, keepdims=True), M_FLOOR)
    p = jnp.exp(s - m)
    l = jnp.sum(p, axis=0, keepdims=True)
    p = p / jnp.maximum(l, 1e-30)
    pb = p.astype(BF16)
    oc_ref[...] = jnp.dot(vct_ref[...], pb, preferred_element_type=F32)[:NSA_HEAD_DIM]
    psum = p[:, 0:TQ]
    for hh in range(1, NSA_HPG):
        psum = psum + p[:, hh * TQ:(hh + 1) * TQ]
    imp = jnp.dot(ovl_ref[...], psum.astype(BF16), preferred_element_type=F32)

    blk = lax.broadcasted_iota(I32, (LANE, 1), 0)
    t = i * TQ + lax.broadcasted_iota(I32, (1, TQ), 1)
    cur = t >> int(math.log2(SLC_BLOCK))
    admiss = blk <= cur
    bonus = jnp.where(blk == 0, FORCE_BONUS, 0.0)
    bonus = jnp.where(blk == cur, FORCE_BONUS, bonus)
    bonus = jnp.where(blk == cur - 1, FORCE_BONUS, bonus)
    score = jnp.where(admiss, imp + bonus, NEG_INF)
    score = jnp.where(blk < n_blk, score, KNOCK)
    blk_b = jnp.broadcast_to(blk, (LANE, TQ))
    selneg = jnp.full((LANE, TQ), NEG_INF, F32)
    for _ in range(topn):
        mx = jnp.max(score, axis=0, keepdims=True)
        first = jnp.min(jnp.where(score == mx, blk_b, LANE), axis=0, keepdims=True)
        pick = blk_b == first
        selneg = jnp.where(pick, 0.0, selneg)
        score = jnp.where(pick, KNOCK, score)
    sel_ref[...] = jnp.where(admiss, selneg, NEG_INF).astype(BF16)


def _nsa_compressed(qt, kc_aug, vct, ct, ovl, n_blk):
    B, G, nQ = qt.shape[:3]
    MP = kc_aug.shape[2]
    body = functools.partial(_nsa_cmp_body, n_blk=n_blk, topn=min(SLC_TOPN, n_blk))
    return pl.pallas_call(
        body,
        grid=(B, G, nQ),
        in_specs=[pl.BlockSpec((None, None, None, LANE, NSA_HPG * TQ), lambda b, g, i: (b, g, i, 0, 0)),
                  pl.BlockSpec((None, None, MP, LANE), lambda b, g, i: (b, g, 0, 0)),
                  pl.BlockSpec((None, None, LANE, MP), lambda b, g, i: (b, g, 0, 0)),
                  pl.BlockSpec((None, CMP_NEAR, NSA_HPG * TQ), lambda b, g, i: (g, 0, 0)),
                  pl.BlockSpec((LANE, MP), lambda b, g, i: (0, 0))],
        out_specs=[pl.BlockSpec((None, None, None, NSA_HEAD_DIM, NSA_HPG * TQ), lambda b, g, i: (b, g, i, 0, 0)),
                   pl.BlockSpec((None, None, None, LANE, TQ), lambda b, g, i: (b, g, i, 0, 0))],
        out_shape=[jax.ShapeDtypeStruct((B, G, nQ, NSA_HEAD_DIM, NSA_HPG * TQ), F32),
                   jax.ShapeDtypeStruct((B, G, nQ, LANE, TQ), BF16)],
        scratch_shapes=[pltpu.VMEM((MP, NSA_HPG * TQ), F32)],
        compiler_params=_cparams(("arbitrary",) * 3),
        name="nsa_compressed",
    )(qt, kc_aug, vct, ct, ovl)


def _softmax_step(s, vt, m_old, acc):
    m_new = jnp.maximum(m_old, jnp.max(s, axis=0, keepdims=True))
    alpha = jnp.exp(m_old - m_new)
    p = jnp.exp(s - m_new).astype(BF16)
    return m_new, acc * alpha + jnp.dot(vt, p, preferred_element_type=F32)


AK = 256


def _pipeline(lo, hi, produce, consume, buf_a, buf_b):
    n = hi - lo

    @pl.when(n > 0)
    def _():
        produce(lo, buf_a)

        def body(p, c):
            j = lo + 2 * p
            produce(j + 1, buf_b)
            consume(j, buf_a)
            produce(j + 2, buf_a)
            consume(j + 1, buf_b)
            return c

        lax.fori_loop(0, (n - 1) // 2, body, 0)

        @pl.when(n % 2 == 1)
        def _():
            consume(hi - 1, buf_a)

        @pl.when(n % 2 == 0)
        def _():
            produce(hi - 1, buf_b)
            consume(hi - 2, buf_a)
            consume(hi - 1, buf_b)


def _attend(lo, hi, lhs_fn, rhs_ref, value_fn, m_ref, acc_ref, sa_ref, sb_ref, near_fn=None):
    def scores(j, buf):
        s = jnp.dot(lhs_fn(j), rhs_ref[...], preferred_element_type=F32)
        buf[...] = s if near_fn is None else near_fn(s, j)

    def consume(j, buf):
        m, acc = _softmax_step(buf[...], value_fn(j), m_ref[...], acc_ref[...])
        m_ref[...] = m
        acc_ref[...] = acc

    _pipeline(lo, hi, scores, consume, sa_ref, sb_ref)


def _near_add(dt_ref, i, s, j):
    parts = []
    for sub in range(AK // TQ):
        idx = jnp.clip(i - (j * (AK // TQ) + sub) + 1, 0, 4)
        parts.append(s[sub * TQ:(sub + 1) * TQ] + dt_ref[idx])
    return jnp.concatenate(parts, axis=0)


def _nsa_sel_body(qt_ref, sel_ref, ks_ref, vst_ref, dt_ref, o_ref, qa_ref, m_ref, acc_ref,
                  sa_ref, sb_ref):
    i = pl.program_id(2)
    L = NSA_HPG * TQ
    qa_ref[0:LANE, :] = qt_ref[...]
    selneg = sel_ref[...]
    qa_ref[LANE:2 * LANE, :] = jnp.concatenate([selneg] * NSA_HPG, axis=1)
    m_ref[...] = jnp.full((1, L), M_FLOOR, F32)
    acc_ref[...] = jnp.zeros(acc_ref.shape, F32)
    last = i // (AK // TQ)
    n_far = jnp.maximum(last - 1, 0)

    def keys(j):
        return ks_ref[pl.ds(pl.multiple_of(j * AK, AK), AK), :]

    def value(j):
        return vst_ref[:, pl.ds(pl.multiple_of(j * AK, AK), AK)]

    _attend(0, n_far, keys, qa_ref, value, m_ref, acc_ref, sa_ref, sb_ref)
    _attend(n_far, last + 1, keys, qa_ref, value, m_ref, acc_ref, sa_ref, sb_ref,
            near_fn=functools.partial(_near_add, dt_ref, i))
    acc = acc_ref[...]
    o_ref[...] = acc[:NSA_HEAD_DIM] / acc[NSA_HEAD_DIM:NSA_HEAD_DIM + 1]


def _nsa_selected(qt, selneg, ks_aug, vst, dt):
    B, G, nQ = qt.shape[:3]
    T = ks_aug.shape[2]
    L = NSA_HPG * TQ
    return pl.pallas_call(
        _nsa_sel_body,
        grid=(B, G, nQ),
        in_specs=[pl.BlockSpec((None, None, None, LANE, L), lambda b, g, i: (b, g, i, 0, 0)),
                  pl.BlockSpec((None, None, None, LANE, TQ), lambda b, g, i: (b, g, i, 0, 0)),
                  pl.BlockSpec((None, None, T, 2 * LANE), lambda b, g, i: (b, g, 0, 0)),
                  pl.BlockSpec((None, None, LANE, T), lambda b, g, i: (b, g, 0, 0)),
                  pl.BlockSpec((None, 5, TQ, L), lambda b, g, i: (g, 0, 0, 0))],
        out_specs=pl.BlockSpec((None, None, None, NSA_HEAD_DIM, L), lambda b, g, i: (b, g, i, 0, 0)),
        out_shape=jax.ShapeDtypeStruct((B, G, nQ, NSA_HEAD_DIM, L), F32),
        scratch_shapes=[pltpu.VMEM((2 * LANE, L), BF16),
                        pltpu.VMEM((1, L), F32),
                        pltpu.VMEM((LANE, L), F32),
                        pltpu.VMEM((AK, L), F32),
                        pltpu.VMEM((AK, L), F32)],
        compiler_params=_cparams(("arbitrary",) * 3),
        name="nsa_selected",
    )(qt, selneg, ks_aug, vst, dt)


def _nsa_win_body(qt_ref, kw_ref, vwt_ref, w0_ref, dt_ref, oc_ref, os_ref, gl_ref, o_ref):
    i = pl.program_id(2)
    span = WINDOW + TQ
    k0 = pl.multiple_of(i * TQ, TQ)
    s = jnp.dot(kw_ref[pl.ds(k0, span), :], qt_ref[...], preferred_element_type=F32)
    s = jnp.concatenate([s[0:TQ] + w0_ref[...], s[TQ:WINDOW - TQ],
                         s[WINDOW - TQ:WINDOW] + dt_ref[2], s[WINDOW:span] + dt_ref[1]], axis=0)
    m = jnp.maximum(jnp.max(s, axis=0, keepdims=True), M_FLOOR)
    p = jnp.exp(s - m).astype(BF16)
    acc = jnp.dot(vwt_ref[:, pl.ds(k0, span)], p, preferred_element_type=F32)
    ow = acc[:NSA_HEAD_DIM] / acc[NSA_HEAD_DIM:NSA_HEAD_DIM + 1]
    g = jax.nn.sigmoid(gl_ref[...])
    o_ref[...] = g[0:1] * oc_ref[...] + g[1:2] * os_ref[...] + g[2:3] * ow


def _nsa_window(qt, kw_aug, vwt, w0, dt, oc, osel, gl):
    B, G, nQ = qt.shape[:3]
    TP = kw_aug.shape[2]
    L = NSA_HPG * TQ
    tile = lambda r: pl.BlockSpec((None, None, None, r, L), lambda b, g, i: (b, g, i, 0, 0))
    return pl.pallas_call(
        _nsa_win_body,
        grid=(B, G, nQ),
        in_specs=[tile(LANE),
                  pl.BlockSpec((None, None, TP, LANE), lambda b, g, i: (b, g, 0, 0)),
                  pl.BlockSpec((None, None, LANE, TP), lambda b, g, i: (b, g, 0, 0)),
                  pl.BlockSpec((TQ, L), lambda b, g, i: (0, 0)),
                  pl.BlockSpec((None, 5, TQ, L), lambda b, g, i: (g, 0, 0, 0)),
                  tile(NSA_HEAD_DIM), tile(NSA_HEAD_DIM), tile(8)],
        out_specs=tile(NSA_HEAD_DIM),
        out_shape=jax.ShapeDtypeStruct((B, G, nQ, NSA_HEAD_DIM, L), F32),
        compiler_params=_cparams(("arbitrary",) * 3),
        name="nsa_window",
    )(qt, kw_aug, vwt, w0, dt, oc, osel, gl)


INT_MIN = -2 ** 31
CK = 2 * NK
KEY_BITS = 16


def _dsa_body(qdt_ref, qit_ref, wt_ref, ki_ref, ckv_ref, ckvt_ref, wuk_ref, wuvt_ref, dt_ref,
              o_ref, sc_ref, qb_ref, m_ref, acc_ref, sa_ref, sb_ref, *, k_top):
    i = pl.program_id(1)
    H = DSA_HEADS
    L = H * TQ
    last = i // (NK // TQ)
    n_steps = last + 1
    t_row = i * TQ + lax.broadcasted_iota(I32, (1, TQ), 1)

    qit = qit_ref[...]
    wt = wt_ref[...]

    def idx_scores(j, buf):
        k0 = pl.multiple_of(j * NK, NK)
        buf[...] = jnp.dot(ki_ref[pl.ds(k0, NK), :], qit, preferred_element_type=F32)

    def idx_reduce(j, buf):
        k0 = pl.multiple_of(j * NK, NK)
        r = jnp.maximum(buf[...], 0.0) * wt
        sc = r[:, 0:TQ]
        for h in range(1, H):
            sc = sc + r[:, h * TQ:(h + 1) * TQ]
        bits = pltpu.bitcast(sc, I32)
        ik = jnp.where(sc == 0.0, 0, bits ^ ((bits >> 31) & 0x7FFFFFFF))
        key = k0 + lax.broadcasted_iota(I32, (NK, 1), 0)
        sc_ref[pl.ds(k0, NK), :] = jnp.where(key <= t_row, jnp.maximum(ik, INT_MIN + 1), INT_MIN)

    _pipeline(0, n_steps, idx_scores, idx_reduce, sa_ref, sb_ref)

    @pl.when(n_steps % 2 == 1)
    def _():
        sc_ref[pl.ds(pl.multiple_of(n_steps * NK, NK), NK), :] = jnp.full((NK, TQ), INT_MIN, I32)

    n_chunks = (n_steps + 1) // 2

    def count(pred):
        def cbody(j, cnt):
            k0 = pl.multiple_of(j * CK, CK)
            ind = jnp.where(pred(sc_ref[pl.ds(k0, CK), :], k0), 1.0, 0.0)
            parts = [ind[r * 8:(r + 1) * 8] for r in range(CK // 8)]
            while len(parts) > 1:
                parts = [a + b for a, b in zip(parts[0::2], parts[1::2])]
            return cnt + parts[0]
        cnt8 = lax.fori_loop(0, n_chunks, cbody, jnp.zeros((8, TQ), F32))
        return jnp.sum(cnt8, axis=0, keepdims=True)

    def any_lane(cond):
        return jnp.max(jnp.where(cond, 1.0, 0.0)) > 0.0

    kf = float(k_top)
    need = t_row + 1 > k_top

    def bit_cond(c):
        b, _tu, ct = c
        return (b < 32) & any_lane(need & (ct != kf))

    def bit_body(c):
        b, tu, ct = c
        cand = tu | jnp.left_shift(jnp.int32(1), 31 - b)
        thr_c = cand ^ INT_MIN
        cnt = count(lambda blk, k0: blk >= thr_c)
        take = cnt >= kf
        return b + 1, jnp.where(take, cand, tu), jnp.where(take, cnt, ct)

    _, tu, ct = lax.while_loop(bit_cond, bit_body,
                               (jnp.int32(0), jnp.zeros((1, TQ), I32), (t_row + 1).astype(F32)))
    thr = jnp.maximum(tu ^ INT_MIN, INT_MIN + 1)

    @pl.when(any_lane(need & (ct > kf)))
    def _():
        keep = kf - count(lambda blk, k0: blk > thr)

        def tie_count(bound):
            def pred(blk, k0):
                key = k0 + lax.broadcasted_iota(I32, (CK, 1), 0)
                return jnp.where(key < bound, blk, INT_MIN) == thr
            return count(pred)

        def pos_body(b, pos):
            cand = pos | jnp.left_shift(jnp.int32(1), KEY_BITS - 1 - b)
            return jnp.where(tie_count(cand) < keep, cand, pos)

        pos = lax.fori_loop(0, KEY_BITS, pos_body, jnp.zeros((1, TQ), I32))

        def fix(j, c):
            k0 = pl.multiple_of(j * CK, CK)
            blk = sc_ref[pl.ds(k0, CK), :]
            key = k0 + lax.broadcasted_iota(I32, (CK, 1), 0)
            drop = jnp.where(key > pos, blk, INT_MIN) == thr
            sc_ref[pl.ds(k0, CK), :] = jnp.where(drop, INT_MIN, blk)
            return c

        lax.fori_loop(0, n_chunks, fix, 0)

    qdt = qdt_ref[...]
    for h in range(H):
        ql = jnp.dot(wuk_ref[h], qdt[:, h * TQ:(h + 1) * TQ], preferred_element_type=F32)
        qb_ref[0:LANE, h * TQ:(h + 1) * TQ] = (ql * DSA_HEAD_DIM ** -0.5).astype(BF16)
    eye = (lax.broadcasted_iota(I32, (LANE, TQ), 0) == lax.broadcasted_iota(I32, (LANE, TQ), 1))
    eye = jnp.where(eye, 1.0, 0.0).astype(BF16)
    qb_ref[LANE:2 * LANE, :] = jnp.concatenate([eye] * H, axis=1)
    m_ref[...] = jnp.full((1, L), M_FLOOR, F32)
    acc_ref[...] = jnp.zeros(acc_ref.shape, F32)
    alast = i // (AK // TQ)
    n_far = jnp.maximum(alast - 1, 0)

    def keys(j):
        k0 = pl.multiple_of(j * AK, AK)
        nb = jnp.where(sc_ref[pl.ds(k0, AK), :] >= thr, 0.0, NEG_INF).astype(BF16)
        return jnp.concatenate([ckv_ref[pl.ds(k0, AK), :], nb], axis=1)

    def value(j):
        return ckvt_ref[:, pl.ds(pl.multiple_of(j * AK, AK), AK)]

    _attend(0, n_far, keys, qb_ref, value, m_ref, acc_ref, sa_ref, sb_ref)
    _attend(n_far, alast + 1, keys, qb_ref, value, m_ref, acc_ref, sa_ref, sb_ref,
            near_fn=functools.partial(_near_add, dt_ref, i))
    acc = acc_ref[...]
    olat = (acc[:DSA_KV_LATENT] / acc[DSA_KV_LATENT:DSA_KV_LATENT + 1]).astype(BF16)
    for h in range(H):
        o_ref[:, h * TQ:(h + 1) * TQ] = jnp.dot(wuvt_ref[h], olat[:, h * TQ:(h + 1) * TQ],
                                                preferred_element_type=F32)


def _dsa(qdt, qit, wt, ki, ckv, ckvt, wuk, wuvt, dt, k_top):
    B, nQ = qdt.shape[:2]
    T = ki.shape[1]
    H = DSA_HEADS
    L = H * TQ
    R = ckvt.shape[1]
    tile = lambda r: pl.BlockSpec((None, None, r, L), lambda b, i: (b, i, 0, 0))
    return pl.pallas_call(
        functools.partial(_dsa_body, k_top=k_top),
        grid=(B, nQ),
        in_specs=[tile(DSA_HEAD_DIM), tile(LANE), tile(1),
                  pl.BlockSpec((None, T, LANE), lambda b, i: (b, 0, 0)),
                  pl.BlockSpec((None, T, DSA_KV_LATENT), lambda b, i: (b, 0, 0)),
                  pl.BlockSpec((None, R, T), lambda b, i: (b, 0, 0)),
                  pl.BlockSpec((H, DSA_KV_LATENT, DSA_HEAD_DIM), lambda b, i: (0, 0, 0)),
                  pl.BlockSpec((H, DSA_HEAD_DIM, DSA_KV_LATENT), lambda b, i: (0, 0, 0)),
                  pl.BlockSpec((5, TQ, L), lambda b, i: (0, 0, 0))],
        out_specs=tile(DSA_HEAD_DIM),
        out_shape=jax.ShapeDtypeStruct((B, nQ, DSA_HEAD_DIM, L), F32),
        scratch_shapes=[pltpu.VMEM((T, TQ), I32),
                        pltpu.VMEM((2 * LANE, L), BF16),
                        pltpu.VMEM((1, L), F32),
                        pltpu.VMEM((R, L), F32),
                        pltpu.VMEM((AK, L), F32),
                        pltpu.VMEM((AK, L), F32)],
        compiler_params=_cparams(("arbitrary",) * 2),
        name="dsa_attention",
    )(qdt, qit, wt, ki, ckv, ckvt, wuk, wuvt, dt)


def _heads_on_lanes(a, B, nQ, nh, dh):
    return a.reshape(B, nQ, TQ, nh, dh).transpose(0, 1, 4, 3, 2).reshape(B, nQ, dh, nh * TQ)


def _heads_off_lanes(a, B, nQ, nh, dh):
    return a.reshape(B, nQ, dh, nh, TQ).transpose(0, 1, 4, 3, 2).reshape(B, nQ * TQ, nh * dh)


def _token_mixers(qn, kvn, qd, ckv, qi, small, rel_bias, cmp_pe, cmp_w1, cmp_b1, cmp_w2,
                  dsa_w_uk, dsa_w_uv):
    B, T = qn.shape[:2]
    G, HPG, DH = NSA_KV_GROUPS, NSA_HPG, NSA_HEAD_DIM
    nQ = T // TQ
    n_blk = T // SLC_BLOCK
    L = HPG * TQ
    assert T % NK == 0 and n_blk <= LANE
    kv6 = kvn.reshape(B, T, 6, G, DH)

    qt = qn.reshape(B, nQ, TQ, G, HPG, DH).transpose(0, 3, 1, 5, 4, 2).reshape(B, G, nQ, DH, L)
    qt = jnp.concatenate([qt, jnp.ones((B, G, nQ, 1, L), BF16),
                          jnp.zeros((B, G, nQ, LANE - DH - 1, L), BF16)], axis=3)

    tab_nsa = rel_bias[:, :NSA_HEADS]
    dt_nsa = _near_tables(tab_nsa).reshape(5, TQ, G, L).transpose(2, 0, 1, 3)
    dt_dsa = _near_tables(rel_bias[:, NSA_HEADS:])

    kvc = _compress(kv6, cmp_pe, cmp_w1, cmp_b1, cmp_w2)
    M = T // CMP_STRIDE
    MP = M + CMP_PAD
    kc = kvc[:, 0].astype(BF16)
    flag = jnp.concatenate([jnp.full((CMP_PAD, 1), NEG_INF, F32), jnp.zeros((M, 1), F32)]).astype(BF16)
    kc_aug = jnp.concatenate(
        [jnp.pad(kc, ((0, 0), (0, 0), (CMP_PAD, 0), (0, 0))),
         jnp.broadcast_to(flag, (B, G, MP, 1)),
         jnp.zeros((B, G, MP, LANE - DH - 1), BF16)], axis=-1)
    vct = jnp.pad(kvc[:, 1].astype(BF16).transpose(0, 1, 3, 2),
                  ((0, 0), (0, 0), (0, LANE - DH), (CMP_PAD, 0)))
    mrow = jnp.arange(CMP_NEAR)[:, None]
    qcol = jnp.arange(TQ)[None, :]
    ct = _delta(tab_nsa, qcol - CMP_STRIDE * mrow + (2 * TQ - CMP_BLOCK + 1))
    ct = ct.reshape(CMP_NEAR, TQ, G, HPG).transpose(2, 0, 3, 1).reshape(G, CMP_NEAR, L)
    n_cmp = (T - CMP_BLOCK) // CMP_STRIDE + 1
    cs = jnp.arange(M) * CMP_STRIDE
    ss = jnp.arange(LANE) * SLC_BLOCK
    ovl = (jnp.clip(jnp.minimum(cs[None, :] + CMP_BLOCK, ss[:, None] + SLC_BLOCK)
                    - jnp.maximum(cs[None, :], ss[:, None]), 0, None).astype(F32) / CMP_BLOCK)
    ovl = jnp.where((jnp.arange(M)[None, :] < n_cmp) & (jnp.arange(LANE)[:, None] < n_blk), ovl, 0.0)
    ovl = jnp.pad(ovl, ((0, 0), (CMP_PAD, 0))).astype(BF16)
    oc, selneg = _nsa_compressed(qt, kc_aug, vct, ct, ovl, n_blk)

    ks = kv6[:, :, 2].transpose(0, 2, 1, 3)
    et = (jnp.arange(T)[:, None] // SLC_BLOCK == jnp.arange(LANE)[None, :]).astype(BF16)
    ks_aug = jnp.concatenate([ks, jnp.zeros((B, G, T, LANE - DH), BF16),
                              jnp.broadcast_to(et, (B, G, T, LANE))], axis=-1)
    vs = kv6[:, :, 3].transpose(0, 2, 3, 1)
    vst = jnp.concatenate([vs, jnp.ones((B, G, 1, T), BF16),
                           jnp.zeros((B, G, LANE - DH - 1, T), BF16)], axis=2)
    osel = _nsa_selected(qt, selneg, ks_aug, vst, dt_nsa)

    TP = T + WINDOW
    kw = jnp.pad(kv6[:, :, 4].transpose(0, 2, 1, 3), ((0, 0), (0, 0), (WINDOW, 0), (0, 0)))
    wflag = jnp.concatenate([jnp.full((WINDOW, 1), NEG_INF, F32), jnp.zeros((T, 1), F32)]).astype(BF16)
    kw_aug = jnp.concatenate([kw, jnp.broadcast_to(wflag, (B, G, TP, 1)),
                              jnp.zeros((B, G, TP, LANE - DH - 1), BF16)], axis=-1)
    vw = jnp.pad(kv6[:, :, 5].transpose(0, 2, 3, 1), ((0, 0), (0, 0), (0, 0), (WINDOW, 0)))
    vwt = jnp.concatenate([vw, jnp.ones((B, G, 1, TP), BF16),
                           jnp.zeros((B, G, LANE - DH - 1, TP), BF16)], axis=2)
    u = jnp.arange(TQ)[:, None]
    w0 = jnp.tile(jnp.where(u > jnp.arange(TQ)[None, :], 0.0, NEG_INF).astype(F32), (1, HPG))
    gl = small[..., 0:3 * NSA_HEADS].reshape(B, nQ, TQ, G, HPG, 3).transpose(0, 3, 1, 5, 4, 2)
    gl = jnp.pad(gl.reshape(B, G, nQ, 3, L), ((0, 0), (0, 0), (0, 0), (0, 5), (0, 0)))
    o_nsa_t = _nsa_window(qt, kw_aug, vwt, w0, dt_nsa, oc, osel, gl)
    o_nsa = o_nsa_t.reshape(B, G, nQ, DH, HPG, TQ).transpose(0, 2, 5, 1, 4, 3).reshape(B, T, NSA_HEADS * DH)

    H = DSA_HEADS
    qdt = _heads_on_lanes(qd, B, nQ, H, DSA_HEAD_DIM)
    qit = _heads_on_lanes(qi, B, nQ, IDX_HEADS, IDX_DIM)
    qit = jnp.pad(qit, ((0, 0), (0, 0), (0, LANE - IDX_DIM), (0, 0)))
    k_idx = small[..., 24:24 + IDX_DIM]
    w_idx = small[..., 24 + IDX_DIM:24 + IDX_DIM + IDX_HEADS] * IDX_HEADS ** -0.5
    ki = jnp.pad(k_idx.astype(BF16), ((0, 0), (0, 0), (0, LANE - IDX_DIM)))
    wt = w_idx.reshape(B, nQ, TQ, IDX_HEADS).transpose(0, 1, 3, 2).reshape(B, nQ, 1, IDX_HEADS * TQ)
    ckvt = jnp.concatenate([ckv.transpose(0, 2, 1), jnp.ones((B, 1, T), BF16),
                            jnp.zeros((B, 7, T), BF16)], axis=1)
    wuk = dsa_w_uk.transpose(1, 0, 2).astype(BF16)
    wuvt = dsa_w_uv.transpose(1, 2, 0).astype(BF16)
    o_dsa_t = _dsa(qdt, qit, wt, ki, ckv, ckvt, wuk, wuvt, dt_dsa, min(IDX_TOPK_MAX, T // 4))
    o_dsa = _heads_off_lanes(o_dsa_t, B, nQ, H, DSA_HEAD_DIM)
    return o_nsa, o_dsa


def _rms(v, g):
    return v * lax.rsqrt(jnp.mean(v * v, axis=-1, keepdims=True) + RMS_EPS) * g


HI_MASK = -65536


def _pack_halves(v):
    w = v.shape[1] // 2
    lo = lax.shift_right_logical(pltpu.bitcast(v[:, :w], I32), 16)
    hi = pltpu.bitcast(v[:, w:], I32) & HI_MASK
    return lo | hi


def _unpack_halves(p):
    return pltpu.bitcast(p << 16, F32), pltpu.bitcast(p & HI_MASK, F32)


def _merge_body(on_ref, od_ref, gm_ref, x_ref, wn_ref, wd_ref, wo_ref, gpost_ref, gt_ref,
                gpre_ref, sc_ref, sh_ref, rwt_ref, x1_ref, h2_ref, h2p_ref, lt_ref):
    D = D_MODEL
    ya = jnp.dot(on_ref[...], wn_ref[...], preferred_element_type=F32)
    yb = jnp.dot(od_ref[...], wd_ref[...], preferred_element_type=F32)
    y = jax.nn.sigmoid(gm_ref[:, 0:D]) * ya + jax.nn.sigmoid(gm_ref[:, D:2 * D]) * yb
    y2 = jnp.dot(y.astype(BF16), wo_ref[...], preferred_element_type=F32)
    x1 = x_ref[...] + gt_ref[...] * _rms(y2, gpost_ref[...])
    x1_ref[...] = x1
    h2 = (_rms(x1, gpre_ref[...]) * (1.0 + sc_ref[...]) + sh_ref[...]).astype(BF16)
    h2_ref[...] = h2
    h2p_ref[...] = _pack_halves(h2.astype(F32))
    lt_ref[...] = lax.dot_general(rwt_ref[...], h2, (((1,), (1,)), ((), ())),
                                  preferred_element_type=F32)


def _merge(o_nsa, o_dsa, gm, x, wn, wd, wo, g_post, gt1, g_pre, sc2, sh2, router_w):
    B, T, D = x.shape
    TM = 256
    E = router_w.shape[1]
    row = lambda w: pl.BlockSpec((None, TM, w), lambda b, i: (b, i, 0))
    full = lambda a: pl.BlockSpec(a.shape, lambda b, i: (0,) * a.ndim)
    vecb = pl.BlockSpec((None, 1, D), lambda b, i: (b, 0, 0))
    vec = pl.BlockSpec((1, D), lambda b, i: (0, 0))
    wn, wd, wo = wn.astype(BF16), wd.astype(BF16), wo.astype(BF16)
    rwt = router_w.T.astype(BF16)
    return pl.pallas_call(
        _merge_body,
        grid=(B, T // TM),
        in_specs=[row(o_nsa.shape[-1]), row(o_dsa.shape[-1]), row(2 * D), row(D),
                  full(wn), full(wd), full(wo), vec, vecb, vec, vecb, vecb, full(rwt)],
        out_specs=[row(D), row(D), row(D // 2), pl.BlockSpec((None, E, TM), lambda b, i: (b, 0, i))],
        out_shape=[jax.ShapeDtypeStruct((B, T, D), F32),
                   jax.ShapeDtypeStruct((B, T, D), BF16),
                   jax.ShapeDtypeStruct((B, T, D // 2), I32),
                   jax.ShapeDtypeStruct((B, E, T), F32)],
        compiler_params=_cparams(("arbitrary",) * 2),
        name="merge_postnorm",
    )(o_nsa, o_dsa, gm, x, wn, wd, wo, g_post.reshape(1, D), gt1.reshape(B, 1, D),
      g_pre.reshape(1, D), sc2.reshape(B, 1, D), sh2.reshape(B, 1, D), rwt)


TR = 256
GSZ = N_EXPERTS // N_EXPERT_GROUPS


def _first_max(v, idx, big):
    mx = jnp.max(v, axis=0, keepdims=True)
    return jnp.min(jnp.where(v == mx, idx, big), axis=0, keepdims=True), mx


def _route_body(lt_ref, bias_ref, eid_ref, rk_ref, wt_ref, cnt_ref, carry_ref):
    first_step = (pl.program_id(0) == 0) & (pl.program_id(1) == 0)

    @pl.when(first_step)
    def _():
        carry_ref[...] = jnp.zeros(carry_ref.shape, F32)

    s = jax.nn.sigmoid(lt_ref[...])
    sel = s + bias_ref[:, 0:1]
    grow = lax.broadcasted_iota(I32, (GSZ, TR), 0)
    gs = []
    for g in range(N_EXPERT_GROUPS):
        blk = sel[g * GSZ:(g + 1) * GSZ]
        f1, m1 = _first_max(blk, grow, GSZ)
        m2 = jnp.max(jnp.where(grow == f1, KNOCK, blk), axis=0, keepdims=True)
        gs.append(m1 + m2)
    gsc = jnp.concatenate(gs, axis=0)
    gidx = lax.broadcasted_iota(I32, (N_EXPERT_GROUPS, TR), 0)
    gsel = jnp.zeros((N_EXPERT_GROUPS, TR), F32)
    for _ in range(TOPK_GROUPS):
        f, _mx = _first_max(gsc, gidx, N_EXPERT_GROUPS)
        pick = gidx == f
        gsel = jnp.where(pick, 1.0, gsel)
        gsc = jnp.where(pick, KNOCK, gsc)
    cur = jnp.concatenate(
        [jnp.where(gsel[g:g + 1] > 0.0, sel[g * GSZ:(g + 1) * GSZ], NEG_INF)
         for g in range(N_EXPERT_GROUPS)], axis=0)
    erow = lax.broadcasted_iota(I32, (N_EXPERTS, TR), 0)
    member = jnp.zeros((N_EXPERTS, TR), F32)
    picks, eids, ws = [], [], []
    for _ in range(TOP_K):
        f, _mx = _first_max(cur, erow, N_EXPERTS)
        pick = erow == f
        picks.append(pick)
        eids.append(f)
        ws.append(jnp.sum(jnp.where(pick, s, 0.0), axis=0, keepdims=True))
        member = jnp.where(pick, 1.0, member)
        cur = jnp.where(pick, KNOCK, cur)
    wsum = ws[0]
    for k in range(1, TOP_K):
        wsum = wsum + ws[k]
    mb = member.astype(BF16)
    upper = (lax.broadcasted_iota(I32, (TR, TR), 0) < lax.broadcasted_iota(I32, (TR, TR), 1))
    upper = jnp.where(upper, 1.0, 0.0).astype(BF16)
    rank = jnp.dot(mb, upper, preferred_element_type=F32) + carry_ref[:, 0:1]
    eid_ref[...] = jnp.concatenate(eids, axis=0)
    rk_ref[...] = jnp.concatenate(
        [jnp.sum(jnp.where(p, rank, 0.0), axis=0, keepdims=True) for p in picks], axis=0)
    wt_ref[...] = jnp.concatenate([w / wsum * ROUTED_SCALE for w in ws], axis=0)
    carry_ref[...] = carry_ref[...] + jnp.dot(mb, jnp.ones((TR, LANE), BF16),
                                              preferred_element_type=F32)
    cnt_ref[...] = carry_ref[...]


def _route(lt, router_bias):
    B, E, T = lt.shape
    out8 = lambda: pl.BlockSpec((None, TOP_K, TR), lambda b, i: (b, 0, i))
    return pl.pallas_call(
        _route_body,
        grid=(B, T // TR),
        in_specs=[pl.BlockSpec((None, E, TR), lambda b, i: (b, 0, i)),
                  pl.BlockSpec((E, 1), lambda b, i: (0, 0))],
        out_specs=[out8(), out8(), out8(), pl.BlockSpec((E, LANE), lambda b, i: (0, 0))],
        out_shape=[jax.ShapeDtypeStruct((B, TOP_K, T), I32),
                   jax.ShapeDtypeStruct((B, TOP_K, T), F32),
                   jax.ShapeDtypeStruct((B, TOP_K, T), F32),
                   jax.ShapeDtypeStruct((E, LANE), F32)],
        scratch_shapes=[pltpu.VMEM((E, LANE), F32)],
        compiler_params=_cparams(("arbitrary",) * 2),
        name="moe_router",
    )(lt, router_bias.reshape(E, 1))


BM = 256


def _gmm_body(be_ref, nu_ref, x_ref, wg_ref, wu_ref, wd_ref, y_ref, wgb, wub, wdb):
    b = pl.program_id(0)

    @pl.when(b < nu_ref[0])
    def _():
        prev = be_ref[jnp.maximum(b - 1, 0)]

        @pl.when((b == 0) | (be_ref[b] != prev))
        def _():
            wgb[...] = wg_ref[...].astype(BF16)
            wub[...] = wu_ref[...].astype(BF16)
            wdb[...] = wd_ref[...].astype(BF16)

        lo, hi = _unpack_halves(x_ref[...])
        xb = jnp.concatenate([lo, hi], axis=1).astype(BF16)
        gt = jnp.dot(xb, wgb[...], preferred_element_type=F32)
        up = jnp.dot(xb, wub[...], preferred_element_type=F32)
        a = (gt * jax.nn.sigmoid(gt) * up).astype(BF16)
        y = jnp.dot(a, wdb[...], preferred_element_type=F32)
        y_ref[...] = _pack_halves(y.astype(BF16).astype(F32))

    @pl.when(b >= nu_ref[0])
    def _():
        y_ref[...] = jnp.zeros(y_ref.shape, I32)


def _grouped_mlp(xs, blk_e, n_used, w_g, w_u, w_d):
    P = xs.shape[0]
    D, De = w_g.shape[-2:]
    nb = P // BM
    rows = lambda b, be, nu: (b, 0)
    wsel = lambda b, be, nu: (be[b], 0, 0)
    grid_spec = pltpu.PrefetchScalarGridSpec(
        num_scalar_prefetch=2,
        grid=(nb,),
        in_specs=[pl.BlockSpec((BM, D // 2), rows),
                  pl.BlockSpec((None, D, De), wsel),
                  pl.BlockSpec((None, D, De), wsel),
                  pl.BlockSpec((None, De, D), wsel)],
        out_specs=pl.BlockSpec((BM, D // 2), rows),
        scratch_shapes=[pltpu.VMEM((D, De), BF16), pltpu.VMEM((D, De), BF16),
                        pltpu.VMEM((De, D), BF16)])
    return pl.pallas_call(
        _gmm_body,
        grid_spec=grid_spec,
        out_shape=jax.ShapeDtypeStruct((P, D // 2), I32),
        compiler_params=_cparams(("arbitrary",)),
        name="moe_experts",
    )(blk_e, n_used, xs, w_g, w_u, w_d)


SC_WINDOW = 128
SC_WIDTH = 256


def _sc_gather(x, idx):
    R, W = x.shape
    pieces = W // SC_WIDTH
    n = idx.shape[0] * pieces
    xp = x.reshape(R * pieces, SC_WIDTH)
    ip = (idx[:, None] * pieces + jnp.arange(pieces, dtype=I32)[None, :]).reshape(1, n)
    mesh = plsc.VectorSubcoreMesh(core_axis_name="c", subcore_axis_name="s")

    @pl.kernel(out_type=jax.ShapeDtypeStruct((n, SC_WIDTH), x.dtype), mesh=mesh)
    def gather(x_hbm, i_hbm, o_hbm):
        def body(i_vmem, o_vmem):
            pltpu.sync_copy(x_hbm.at[i_vmem.at[0]], o_vmem)

        pltpu.emit_pipeline(
            body, grid=(n // SC_WINDOW,),
            in_specs=[pl.BlockSpec((1, SC_WINDOW), lambda i: (0, i))],
            out_specs=[pl.BlockSpec((SC_WINDOW, SC_WIDTH), lambda i: (i, 0))],
            core_axis_name=("c", "s"),
            dimension_semantics=(pltpu.PARALLEL,),
        )(i_hbm, o_hbm)

    return gather(xp, ip).reshape(idx.shape[0], W)


def _final_body(yg_ref, w_ref, h_ref, x1_ref, wg_ref, wu_ref, wd_ref, g_ref, gt_ref, o_ref):
    hb = h_ref[...]
    gt = jnp.dot(hb, wg_ref[...], preferred_element_type=F32)
    up = jnp.dot(hb, wu_ref[...], preferred_element_type=F32)
    a = (gt * jax.nn.sigmoid(gt) * up).astype(BF16)
    shared = jnp.dot(a, wd_ref[...], preferred_element_type=F32)
    half = D_MODEL // 2
    w = w_ref[...]
    r_lo = r_hi = None
    for k in range(TOP_K):
        lo, hi = _unpack_halves(yg_ref[:, k * half:(k + 1) * half])
        wk = w[:, k:k + 1]
        r_lo = lo * wk if r_lo is None else r_lo + lo * wk
        r_hi = hi * wk if r_hi is None else r_hi + hi * wk
    y = jnp.concatenate([r_lo, r_hi], axis=1) + shared
    o_ref[...] = x1_ref[...] + gt_ref[...] * _rms(y, g_ref[...])


def _final(yg, w, h2, x1, ws_g, ws_u, ws_d, g_post, gt2):
    B, T, D = x1.shape
    TM = 256 if T % 256 == 0 else TQ
    row = lambda wd: pl.BlockSpec((None, TM, wd), lambda b, i: (b, i, 0))
    full = lambda a: pl.BlockSpec(a.shape, lambda b, i: (0,) * a.ndim)
    ws_g, ws_u, ws_d = ws_g.astype(BF16), ws_u.astype(BF16), ws_d.astype(BF16)
    return pl.pallas_call(
        _final_body,
        grid=(B, T // TM),
        in_specs=[row(TOP_K * D // 2), row(TOP_K), row(D), row(D),
                  full(ws_g), full(ws_u), full(ws_d),
                  pl.BlockSpec((1, D), lambda b, i: (0, 0)),
                  pl.BlockSpec((None, 1, D), lambda b, i: (b, 0, 0))],
        out_specs=row(D),
        out_shape=jax.ShapeDtypeStruct((B, T, D), F32),
        compiler_params=_cparams(("arbitrary",) * 2),
        name="shared_final",
    )(yg, w, h2, x1, ws_g, ws_u, ws_d, g_post.reshape(1, D), gt2.reshape(B, 1, D))


PAD_ROWS = 256


def _moe(h2, h2p, lt, x1, router_bias, w_g, w_u, w_d, ws_g, ws_u, ws_d, g_post, gt2):
    B, T, D = x1.shape
    N = B * T
    eid, rk, wts, cnt = _route(lt, router_bias)
    counts = cnt[:, 0].astype(I32)
    padded = (counts + BM - 1) // BM * BM
    pend = jnp.cumsum(padded)
    pstart = pend - padded
    nb = (N * TOP_K + N_EXPERTS * (BM - 1) + BM - 1) // BM
    P = nb * BM
    blk_e = jnp.minimum(jnp.searchsorted(pend, jnp.arange(nb, dtype=I32) * BM, side='right'),
                        N_EXPERTS - 1).astype(I32)
    n_used = (pend[-1] // BM).astype(I32).reshape(1)
    eid_f = eid.transpose(0, 2, 1).reshape(N * TOP_K)
    slot = pstart[eid_f] + rk.transpose(0, 2, 1).reshape(N * TOP_K).astype(I32)
    w_f = wts.transpose(0, 2, 1)
    pad_tok = N + jnp.arange(P, dtype=I32) % PAD_ROWS
    tok = pad_tok.at[slot].set(jnp.arange(N * TOP_K, dtype=I32) // TOP_K)
    h_rows = jnp.concatenate([h2p.reshape(N, D // 2), jnp.zeros((PAD_ROWS, D // 2), I32)], axis=0)
    xs = _sc_gather(h_rows, tok)
    ys = _grouped_mlp(xs, blk_e, n_used, w_g, w_u, w_d)
    yg = _sc_gather(ys, slot)
    return _final(yg.reshape(B, T, TOP_K * D // 2), w_f, h2, x1, ws_g, ws_u, ws_d, g_post, gt2)


def kernel(x, c, w_ada, b_ada, g_pre_mix, g_post_mix, g_pre_ffn, g_post_ffn, w_in, rel_bias, cmp_pe, cmp_w1, cmp_b1, cmp_w2, dsa_kv_norm, dsa_w_uk, dsa_w_uv, w_branch_nsa, w_branch_dsa, w_out, router_w, router_bias, w_exp_gate, w_exp_up, w_exp_down, w_sh_gate, w_sh_up, w_sh_down):
    depth = w_ada.shape[0]
    D = D_MODEL
    for l in range(depth):
        mod = _adaln(c, w_ada[l], b_ada[l])
        sh1, sc1, gt1, sh2, sc2, gt2 = [mod[:, k * D:(k + 1) * D] for k in range(N_MOD)]
        qn, kvn, qd, ckv, qi, small, gm = _input_proj(x, g_pre_mix[l], sc1, sh1, w_in[l], dsa_kv_norm[l])
        o_nsa, o_dsa = _token_mixers(qn, kvn, qd, ckv, qi, small, rel_bias, cmp_pe[l], cmp_w1[l],
                                     cmp_b1[l], cmp_w2[l], dsa_w_uk[l], dsa_w_uv[l])
        x1, h2, h2p, lt = _merge(o_nsa.astype(BF16), o_dsa.astype(BF16), gm, x, w_branch_nsa[l],
                                 w_branch_dsa[l], w_out[l], g_post_mix[l], gt1, g_pre_ffn[l], sc2,
                                 sh2, router_w[l])
        x = _moe(h2, h2p, lt, x1, router_bias[l], w_exp_gate[l], w_exp_up[l], w_exp_down[l],
                 w_sh_gate[l], w_sh_up[l], w_sh_down[l], g_post_ffn[l], gt2)
    return x
```

```python
import functools
import math

import jax
import jax.numpy as jnp
import numpy as np
from jax import lax
from jax.experimental import pallas as pl
from jax.experimental.pallas import tpu as pltpu
from jax.experimental.pallas import tpu_sc as plsc

F32 = jnp.float32
BF16 = jnp.bfloat16
I32 = jnp.int32

D_MODEL = 1024
N_MOD = 6
NSA_HEADS = 8
NSA_KV_GROUPS = 2
NSA_HPG = NSA_HEADS // NSA_KV_GROUPS
NSA_HEAD_DIM = 64
CMP_BLOCK = 32
CMP_STRIDE = 16
CMP_HIDDEN = 128
SLC_BLOCK = 64
SLC_TOPN = 16
WINDOW = 512
FORCE_BONUS = 1e4
DSA_HEADS = 8
DSA_HEAD_DIM = 64
DSA_KV_LATENT = 128
IDX_HEADS = 8
IDX_DIM = 64
IDX_TOPK_MAX = 256
REL_BUCKETS = 32
REL_MAX_DISTANCE = 128
N_EXPERTS = 256
N_EXPERT_GROUPS = 8
TOPK_GROUPS = 4
TOP_K = 8
D_EXPERT = 256
ROUTED_SCALE = 2.5
RMS_EPS = 1e-6
NEG_INF = -1e30

LANE = 128
TQ = 128
NK = 256
M_FLOOR = -1e29
KNOCK = -3e38
CMP_PAD = 16
CMP_NEAR = 24
VMEM_LIMIT = 56 * 1024 * 1024
SC_WINDOW = 128
SC_WIDTH = 256

PROJ_WIDTHS = (512, 768, 24, 512, 128, 512, 64, 8, 2048)
PROJ_OFFS = tuple(int(v) for v in np.cumsum((0,) + PROJ_WIDTHS))


def _cparams(sem):
    return pltpu.CompilerParams(dimension_semantics=sem, vmem_limit_bytes=VMEM_LIMIT)


def _mod_body(c_ref, w_ref, b_ref, o_ref):
    c = c_ref[...]
    s = c * jax.nn.sigmoid(c)
    o_ref[...] = jnp.dot(s, w_ref[...], preferred_element_type=F32) + b_ref[...]


def _adaln(c, w_ada, b_ada):
    B, D = c.shape
    E = w_ada.shape[1]
    cp = jnp.zeros((8, D), F32).at[:B].set(c)
    out = pl.pallas_call(
        _mod_body,
        grid=(E // D,),
        in_specs=[pl.BlockSpec((8, D), lambda j: (0, 0)),
                  pl.BlockSpec((D, D), lambda j: (0, j)),
                  pl.BlockSpec((1, D), lambda j: (0, j))],
        out_specs=pl.BlockSpec((8, D), lambda j: (0, j)),
        out_shape=jax.ShapeDtypeStruct((8, E), F32),
        compiler_params=_cparams(("arbitrary",)),
        name="adaln_mod",
    )(cp, w_ada, b_ada.reshape(1, E))
    return out[:B]


PW = (512, 768, 512, 128, 512, 128, 2048)
PO = tuple(int(v) for v in np.cumsum((0,) + PW))


def _proj_body(x_ref, g_ref, sc_ref, sh_ref, w_ref, kn_ref,
               qn_ref, kv_ref, qd_ref, ckv_ref, qi_ref, sm_ref, gm_ref):
    x = x_ref[...]
    ms = jnp.mean(x * x, axis=-1, keepdims=True)
    h = x * lax.rsqrt(ms + RMS_EPS) * g_ref[...]
    h = h * (1.0 + sc_ref[...]) + sh_ref[...]
    hb = h.astype(BF16)

    def mm(k):
        return jnp.dot(hb, w_ref[:, PO[k]:PO[k + 1]], preferred_element_type=F32)

    qn_ref[...] = (mm(0) * NSA_HEAD_DIM ** -0.5).astype(BF16)
    kv_ref[...] = mm(1).astype(BF16)
    qd_ref[...] = mm(2).astype(BF16)
    c = mm(3)
    cms = jnp.mean(c * c, axis=-1, keepdims=True)
    ckv_ref[...] = (c * lax.rsqrt(cms + RMS_EPS) * kn_ref[...]).astype(BF16)
    qi_ref[...] = (mm(4) * IDX_DIM ** -0.5).astype(BF16)
    sm_ref[...] = mm(5)
    gm_ref[...] = mm(6)


def _input_proj(x, g_pre, sc, sh, w_in, kv_norm):
    B, T, D = x.shape
    TM = 512 if T % 512 == 0 else TQ
    o = PROJ_OFFS
    small = jnp.concatenate([w_in[:, o[2]:o[3]], w_in[:, o[6]:o[7]], w_in[:, o[7]:o[8]],
                             jnp.zeros((D, 32), F32)], axis=1)
    w_r = jnp.concatenate([w_in[:, o[0]:o[1]], w_in[:, o[1]:o[2]], w_in[:, o[3]:o[4]],
                           w_in[:, o[4]:o[5]], w_in[:, o[5]:o[6]], small,
                           w_in[:, o[8]:o[9]]], axis=1).astype(BF16)
    row = lambda w: pl.BlockSpec((None, TM, w), lambda b, i: (b, i, 0))
    vec = pl.BlockSpec((None, 1, D), lambda b, i: (b, 0, 0))
    outs = pl.pallas_call(
        _proj_body,
        grid=(B, T // TM),
        in_specs=[row(D),
                  pl.BlockSpec((1, D), lambda b, i: (0, 0)),
                  vec, vec,
                  pl.BlockSpec((D, PO[-1]), lambda b, i: (0, 0)),
                  pl.BlockSpec((1, DSA_KV_LATENT), lambda b, i: (0, 0))],
        out_specs=[row(w) for w in PW],
        out_shape=[jax.ShapeDtypeStruct((B, T, PW[0]), BF16),
                   jax.ShapeDtypeStruct((B, T, PW[1]), BF16),
                   jax.ShapeDtypeStruct((B, T, PW[2]), BF16),
                   jax.ShapeDtypeStruct((B, T, PW[3]), BF16),
                   jax.ShapeDtypeStruct((B, T, PW[4]), BF16),
                   jax.ShapeDtypeStruct((B, T, PW[5]), F32),
                   jax.ShapeDtypeStruct((B, T, PW[6]), F32)],
        compiler_params=_cparams(("arbitrary", "arbitrary")),
        name="prenorm_proj",
    )(x, g_pre.reshape(1, D), sc.reshape(B, 1, D), sh.reshape(B, 1, D), w_r,
      kv_norm.reshape(1, DSA_KV_LATENT))
    return outs


def _t5_bucket(dist):
    n = jnp.maximum(dist, 0)
    exact = REL_BUCKETS // 2
    nf = jnp.maximum(n, 1).astype(F32)
    large = exact + (jnp.log(nf / exact) / math.log(REL_MAX_DISTANCE / exact)
                     * (REL_BUCKETS - exact)).astype(I32)
    return jnp.where(n < exact, n, jnp.minimum(large, REL_BUCKETS - 1))


def _delta(tab, dist):
    onehot = (_t5_bucket(dist)[..., None] == jnp.arange(REL_BUCKETS)).astype(F32)
    d = jnp.einsum('...b,bh->...h', onehot, tab - tab[REL_BUCKETS - 1],
                   precision=lax.Precision.HIGHEST)
    return jnp.where((dist >= 0)[..., None], d, NEG_INF)


def _near_tables(tab):
    H = tab.shape[1]
    c = jnp.arange(TQ)[:, None]
    q = jnp.arange(TQ)[None, :]
    tabs = [jnp.full((TQ, TQ, H), NEG_INF, F32)]
    for diff in (0, 1):
        tabs.append(_delta(tab, diff * TQ + q - c))
    tabs += [jnp.zeros((TQ, TQ, H), F32)] * 2
    t = jnp.stack(tabs)
    return t.transpose(0, 1, 3, 2).reshape(5, TQ, H * TQ)


def _cmp_body(r_ref, w1_ref, c_ref, w2_ref, o_ref):
    y = jnp.dot(r_ref[...], w1_ref[...], preferred_element_type=F32)
    M = y.shape[0]
    z = y[:, :CMP_HIDDEN] + pltpu.roll(y[:, CMP_HIDDEN:], M - 1, 0) + c_ref[...]
    z = jax.nn.gelu(z)
    o_ref[...] = jnp.dot(z.astype(BF16), w2_ref[...], preferred_element_type=F32)


def _compress(kv6, cmp_pe, cmp_w1, cmp_b1, cmp_w2):
    B, T = kv6.shape[:2]
    G, DH = NSA_KV_GROUPS, NSA_HEAD_DIM
    M = T // CMP_STRIDE
    half = CMP_STRIDE * DH
    r = kv6[:, :, 0:2].transpose(0, 2, 3, 1, 4).reshape(B, 2, G, M, half)
    w1 = cmp_w1.reshape(2, 2, half, CMP_HIDDEN)
    w1c = jnp.concatenate([w1[:, 0], w1[:, 1]], axis=-1).astype(BF16)
    cvec = (jnp.einsum('wld,wldk->wk', cmp_pe, cmp_w1, precision=lax.Precision.HIGHEST)
            + cmp_b1).reshape(2, 1, CMP_HIDDEN)
    return pl.pallas_call(
        _cmp_body,
        grid=(B, 2, G),
        in_specs=[pl.BlockSpec((None, None, None, M, half), lambda b, w, g: (b, w, g, 0, 0)),
                  pl.BlockSpec((None, half, 2 * CMP_HIDDEN), lambda b, w, g: (w, 0, 0)),
                  pl.BlockSpec((None, 1, CMP_HIDDEN), lambda b, w, g: (w, 0, 0)),
                  pl.BlockSpec((None, CMP_HIDDEN, DH), lambda b, w, g: (w, 0, 0))],
        out_specs=pl.BlockSpec((None, None, None, M, DH), lambda b, w, g: (b, w, g, 0, 0)),
        out_shape=jax.ShapeDtypeStruct((B, 2, G, M, DH), F32),
        compiler_params=_cparams(("arbitrary",) * 3),
        name="kv_compress",
    )(r, w1c, cvec, cmp_w2.astype(BF16))


def _nsa_cmp_body(qt_ref, kc_ref, vct_ref, ct_ref, ovl_ref, oc_ref, sel_ref, s_ref,
                  *, n_blk, topn):
    i = pl.program_id(2)
    qt = qt_ref[...]
    s_ref[...] = jnp.dot(kc_ref[...], qt, preferred_element_type=F32)
    near = pl.ds(pl.multiple_of(i * 8, 8), CMP_NEAR)
    s_ref[near, :] = s_ref[near, :] + ct_ref[...]
    s = s_ref[...]
    MP = s.shape[0]
    row = lax.broadcasted_iota(I32, (MP, 1), 0)
    s = jnp.where(row < i * 8 + CMP_NEAR, s, NEG_INF)
    m = jnp.maximum(jnp.max(s, axis=0, keepdims=True), M_FLOOR)
    p = jnp.exp(s - m)
    l = jnp.sum(p, axis=0, keepdims=True)
    p = p / jnp.maximum(l, 1e-30)
    pb = p.astype(BF16)
    oc_ref[...] = jnp.dot(vct_ref[...], pb, preferred_element_type=F32)[:NSA_HEAD_DIM]
    psum = p[:, 0:TQ]
    for hh in range(1, NSA_HPG):
        psum = psum + p[:, hh * TQ:(hh + 1) * TQ]
    imp = jnp.dot(ovl_ref[...], psum.astype(BF16), preferred_element_type=F32)

    blk = lax.broadcasted_iota(I32, (LANE, 1), 0)
    t = i * TQ + lax.broadcasted_iota(I32, (1, TQ), 1)
    cur = t >> int(math.log2(SLC_BLOCK))
    admiss = blk <= cur
    bonus = jnp.where(blk == 0, FORCE_BONUS, 0.0)
    bonus = jnp.where(blk == cur, FORCE_BONUS, bonus)
    bonus = jnp.where(blk == cur - 1, FORCE_BONUS, bonus)
    score = jnp.where(admiss, imp + bonus, NEG_INF)
    score = jnp.where(blk < n_blk, score, KNOCK)
    blk_b = jnp.broadcast_to(blk, (LANE, TQ))
    selneg = jnp.full((LANE, TQ), NEG_INF, F32)
    for _ in range(topn):
        mx = jnp.max(score, axis=0, keepdims=True)
        first = jnp.min(jnp.where(score == mx, blk_b, LANE), axis=0, keepdims=True)
        pick = blk_b == first
        selneg = jnp.where(pick, 0.0, selneg)
        score = jnp.where(pick, KNOCK, score)
    sel_ref[...] = jnp.where(admiss, selneg, NEG_INF).astype(BF16)


def _nsa_compressed(qt, kc_aug, vct, ct, ovl, n_blk):
    B, G, nQ = qt.shape[:3]
    MP = kc_aug.shape[2]
    body = functools.partial(_nsa_cmp_body, n_blk=n_blk, topn=min(SLC_TOPN, n_blk))
    return pl.pallas_call(
        body,
        grid=(B, G, nQ),
        in_specs=[pl.BlockSpec((None, None, None, LANE, NSA_HPG * TQ), lambda b, g, i: (b, g, i, 0, 0)),
                  pl.BlockSpec((None, None, MP, LANE), lambda b, g, i: (b, g, 0, 0)),
                  pl.BlockSpec((None, None, LANE, MP), lambda b, g, i: (b, g, 0, 0)),
                  pl.BlockSpec((None, CMP_NEAR, NSA_HPG * TQ), lambda b, g, i: (g, 0, 0)),
                  pl.BlockSpec((LANE, MP), lambda b, g, i: (0, 0))],
        out_specs=[pl.BlockSpec((None, None, None, NSA_HEAD_DIM, NSA_HPG * TQ), lambda b, g, i: (b, g, i, 0, 0)),
                   pl.BlockSpec((None, None, None, LANE, TQ), lambda b, g, i: (b, g, i, 0, 0))],
        out_shape=[jax.ShapeDtypeStruct((B, G, nQ, NSA_HEAD_DIM, NSA_HPG * TQ), F32),
                   jax.ShapeDtypeStruct((B, G, nQ, LANE, TQ), BF16)],
        scratch_shapes=[pltpu.VMEM((MP, NSA_HPG * TQ), F32)],
        compiler_params=_cparams(("arbitrary",) * 3),
        name="nsa_compressed",
    )(qt, kc_aug, vct, ct, ovl)


def _softmax_step(s, vt, m_old, acc):
    m_new = jnp.maximum(m_old, jnp.max(s, axis=0, keepdims=True))
    alpha = jnp.exp(m_old - m_new)
    p = jnp.exp(s - m_new).astype(BF16)
    return m_new, acc * alpha + jnp.dot(vt, p, preferred_element_type=F32)


AK = 256


def _pipeline(lo, hi, produce, consume, buf_a, buf_b):
    n = hi - lo

    @pl.when(n > 0)
    def _():
        produce(lo, buf_a)

        def body(p, c):
            j = lo + 2 * p
            produce(j + 1, buf_b)
            consume(j, buf_a)
            produce(j + 2, buf_a)
            consume(j + 1, buf_b)
            return c

        lax.fori_loop(0, (n - 1) // 2, body, 0)

        @pl.when(n % 2 == 1)
        def _():
            consume(hi - 1, buf_a)

        @pl.when(n % 2 == 0)
        def _():
            produce(hi - 1, buf_b)
            consume(hi - 2, buf_a)
            consume(hi - 1, buf_b)


def _attend(lo, hi, lhs_fn, rhs_ref, value_fn, m_ref, acc_ref, sa_ref, sb_ref, near_fn=None):
    def scores(j, buf):
        s = jnp.dot(lhs_fn(j), rhs_ref[...], preferred_element_type=F32)
        buf[...] = s if near_fn is None else near_fn(s, j)

    def consume(j, buf):
        m, acc = _softmax_step(buf[...], value_fn(j), m_ref[...], acc_ref[...])
        m_ref[...] = m
        acc_ref[...] = acc

    _pipeline(lo, hi, scores, consume, sa_ref, sb_ref)


def _near_add(dt_ref, i, s, j):
    parts = []
    for sub in range(AK // TQ):
        idx = jnp.clip(i - (j * (AK // TQ) + sub) + 1, 0, 4)
        parts.append(s[sub * TQ:(sub + 1) * TQ] + dt_ref[idx])
    return jnp.concatenate(parts, axis=0)


def _nsa_sel_body(qt_ref, sel_ref, ks_ref, vst_ref, dt_ref, o_ref, qa_ref, m_ref, acc_ref,
                  sa_ref, sb_ref):
    i = pl.program_id(2)
    L = NSA_HPG * TQ
    qa_ref[0:LANE, :] = qt_ref[...]
    selneg = sel_ref[...]
    qa_ref[LANE:2 * LANE, :] = jnp.concatenate([selneg] * NSA_HPG, axis=1)
    m_ref[...] = jnp.full((1, L), M_FLOOR, F32)
    acc_ref[...] = jnp.zeros(acc_ref.shape, F32)
    last = i // (AK // TQ)
    n_far = jnp.maximum(last - 1, 0)

    def keys(j):
        return ks_ref[pl.ds(pl.multiple_of(j * AK, AK), AK), :]

    def value(j):
        return vst_ref[:, pl.ds(pl.multiple_of(j * AK, AK), AK)]

    _attend(0, n_far, keys, qa_ref, value, m_ref, acc_ref, sa_ref, sb_ref)
    _attend(n_far, last + 1, keys, qa_ref, value, m_ref, acc_ref, sa_ref, sb_ref,
            near_fn=functools.partial(_near_add, dt_ref, i))
    acc = acc_ref[...]
    o_ref[...] = acc[:NSA_HEAD_DIM] / acc[NSA_HEAD_DIM:NSA_HEAD_DIM + 1]


def _nsa_selected(qt, selneg, ks_aug, vst, dt):
    B, G, nQ = qt.shape[:3]
    T = ks_aug.shape[2]
    L = NSA_HPG * TQ
    return pl.pallas_call(
        _nsa_sel_body,
        grid=(B, G, nQ),
        in_specs=[pl.BlockSpec((None, None, None, LANE, L), lambda b, g, i: (b, g, i, 0, 0)),
                  pl.BlockSpec((None, None, None, LANE, TQ), lambda b, g, i: (b, g, i, 0, 0)),
                  pl.BlockSpec((None, None, T, 2 * LANE), lambda b, g, i: (b, g, 0, 0)),
                  pl.BlockSpec((None, None, LANE, T), lambda b, g, i: (b, g, 0, 0)),
                  pl.BlockSpec((None, 5, TQ, L), lambda b, g, i: (g, 0, 0, 0))],
        out_specs=pl.BlockSpec((None, None, None, NSA_HEAD_DIM, L), lambda b, g, i: (b, g, i, 0, 0)),
        out_shape=jax.ShapeDtypeStruct((B, G, nQ, NSA_HEAD_DIM, L), F32),
        scratch_shapes=[pltpu.VMEM((2 * LANE, L), BF16),
                        pltpu.VMEM((1, L), F32),
                        pltpu.VMEM((LANE, L), F32),
                        pltpu.VMEM((AK, L), F32),
                        pltpu.VMEM((AK, L), F32)],
        compiler_params=_cparams(("arbitrary",) * 3),
        name="nsa_selected",
    )(qt, selneg, ks_aug, vst, dt)


def _nsa_win_body(qt_ref, kw_ref, vwt_ref, w0_ref, dt_ref, oc_ref, os_ref, gl_ref, o_ref):
    i = pl.program_id(2)
    span = WINDOW + TQ
    k0 = pl.multiple_of(i * TQ, TQ)
    s = jnp.dot(kw_ref[pl.ds(k0, span), :], qt_ref[...], preferred_element_type=F32)
    s = jnp.concatenate([s[0:TQ] + w0_ref[...], s[TQ:WINDOW - TQ],
                         s[WINDOW - TQ:WINDOW] + dt_ref[2], s[WINDOW:span] + dt_ref[1]], axis=0)
    m = jnp.maximum(jnp.max(s, axis=0, keepdims=True), M_FLOOR)
    p = jnp.exp(s - m).astype(BF16)
    acc = jnp.dot(vwt_ref[:, pl.ds(k0, span)], p, preferred_element_type=F32)
    ow = acc[:NSA_HEAD_DIM] / acc[NSA_HEAD_DIM:NSA_HEAD_DIM + 1]
    g = jax.nn.sigmoid(gl_ref[...])
    o_ref[...] = g[0:1] * oc_ref[...] + g[1:2] * os_ref[...] + g[2:3] * ow


def _nsa_window(qt, kw_aug, vwt, w0, dt, oc, osel, gl):
    B, G, nQ = qt.shape[:3]
    TP = kw_aug.shape[2]
    L = NSA_HPG * TQ
    tile = lambda r: pl.BlockSpec((None, None, None, r, L), lambda b, g, i: (b, g, i, 0, 0))
    return pl.pallas_call(
        _nsa_win_body,
        grid=(B, G, nQ),
        in_specs=[tile(LANE),
                  pl.BlockSpec((None, None, TP, LANE), lambda b, g, i: (b, g, 0, 0)),
                  pl.BlockSpec((None, None, LANE, TP), lambda b, g, i: (b, g, 0, 0)),
                  pl.BlockSpec((TQ, L), lambda b, g, i: (0, 0)),
                  pl.BlockSpec((None, 5, TQ, L), lambda b, g, i: (g, 0, 0, 0)),
                  tile(NSA_HEAD_DIM), tile(NSA_HEAD_DIM), tile(8)],
        out_specs=tile(NSA_HEAD_DIM),
        out_shape=jax.ShapeDtypeStruct((B, G, nQ, NSA_HEAD_DIM, L), F32),
        compiler_params=_cparams(("arbitrary",) * 3),
        name="nsa_window",
    )(qt, kw_aug, vwt, w0, dt, oc, osel, gl)


INT_MIN = -2 ** 31
CK = 2 * NK
KEY_BITS = 16


def _dsa_body(qdt_ref, qit_ref, wt_ref, ki_ref, ckv_ref, ckvt_ref, wuk_ref, wuvt_ref, dt_ref,
              o_ref, sc_ref, qb_ref, m_ref, acc_ref, sa_ref, sb_ref, *, k_top):
    i = pl.program_id(1)
    H = DSA_HEADS
    L = H * TQ
    last = i // (NK // TQ)
    n_steps = last + 1
    t_row = i * TQ + lax.broadcasted_iota(I32, (1, TQ), 1)

    qit = qit_ref[...]
    wt = wt_ref[...]

    def idx_scores(j, buf):
        k0 = pl.multiple_of(j * NK, NK)
        buf[...] = jnp.dot(ki_ref[pl.ds(k0, NK), :], qit, preferred_element_type=F32)

    def idx_reduce(j, buf):
        k0 = pl.multiple_of(j * NK, NK)
        r = jnp.maximum(buf[...], 0.0) * wt
        sc = r[:, 0:TQ]
        for h in range(1, H):
            sc = sc + r[:, h * TQ:(h + 1) * TQ]
        bits = pltpu.bitcast(sc, I32)
        ik = jnp.where(sc == 0.0, 0, bits ^ ((bits >> 31) & 0x7FFFFFFF))
        key = k0 + lax.broadcasted_iota(I32, (NK, 1), 0)
        sc_ref[pl.ds(k0, NK), :] = jnp.where(key <= t_row, jnp.maximum(ik, INT_MIN + 1), INT_MIN)

    _pipeline(0, n_steps, idx_scores, idx_reduce, sa_ref, sb_ref)

    @pl.when(n_steps % 2 == 1)
    def _():
        sc_ref[pl.ds(pl.multiple_of(n_steps * NK, NK), NK), :] = jnp.full((NK, TQ), INT_MIN, I32)

    n_chunks = (n_steps + 1) // 2

    def count(pred):
        def cbody(j, cnt):
            k0 = pl.multiple_of(j * CK, CK)
            ind = jnp.where(pred(sc_ref[pl.ds(k0, CK), :], k0), 1.0, 0.0)
            parts = [ind[r * 8:(r + 1) * 8] for r in range(CK // 8)]
            while len(parts) > 1:
                parts = [a + b for a, b in zip(parts[0::2], parts[1::2])]
            return cnt + parts[0]
        cnt8 = lax.fori_loop(0, n_chunks, cbody, jnp.zeros((8, TQ), F32))
        return jnp.sum(cnt8, axis=0, keepdims=True)

    def any_lane(cond):
        return jnp.max(jnp.where(cond, 1.0, 0.0)) > 0.0

    kf = float(k_top)
    need = t_row + 1 > k_top

    def bit_cond(c):
        b, _tu, ct = c
        return (b < 32) & any_lane(need & (ct != kf))

    def bit_body(c):
        b, tu, ct = c
        cand = tu | jnp.left_shift(jnp.int32(1), 31 - b)
        thr_c = cand ^ INT_MIN
        cnt = count(lambda blk, k0: blk >= thr_c)
        take = cnt >= kf
        return b + 1, jnp.where(take, cand, tu), jnp.where(take, cnt, ct)

    _, tu, ct = lax.while_loop(bit_cond, bit_body,
                               (jnp.int32(0), jnp.zeros((1, TQ), I32), (t_row + 1).astype(F32)))
    thr = jnp.maximum(tu ^ INT_MIN, INT_MIN + 1)

    @pl.when(any_lane(need & (ct > kf)))
    def _():
        keep = kf - count(lambda blk, k0: blk > thr)

        def tie_count(bound):
            def pred(blk, k0):
                key = k0 + lax.broadcasted_iota(I32, (CK, 1), 0)
                return jnp.where(key < bound, blk, INT_MIN) == thr
            return count(pred)

        def pos_body(b, pos):
            cand = pos | jnp.left_shift(jnp.int32(1), KEY_BITS - 1 - b)
            return jnp.where(tie_count(cand) < keep, cand, pos)

        pos = lax.fori_loop(0, KEY_BITS, pos_body, jnp.zeros((1, TQ), I32))

        def fix(j, c):
            k0 = pl.multiple_of(j * CK, CK)
            blk = sc_ref[pl.ds(k0, CK), :]
            key = k0 + lax.broadcasted_iota(I32, (CK, 1), 0)
            drop = jnp.where(key > pos, blk, INT_MIN) == thr
            sc_ref[pl.ds(k0, CK), :] = jnp.where(drop, INT_MIN, blk)
            return c

        lax.fori_loop(0, n_chunks, fix, 0)

    qdt = qdt_ref[...]
    for h in range(H):
        ql = jnp.dot(wuk_ref[h], qdt[:, h * TQ:(h + 1) * TQ], preferred_element_type=F32)
        qb_ref[0:LANE, h * TQ:(h + 1) * TQ] = (ql * DSA_HEAD_DIM ** -0.5).astype(BF16)
    eye = (lax.broadcasted_iota(I32, (LANE, TQ), 0) == lax.broadcasted_iota(I32, (LANE, TQ), 1))
    eye = jnp.where(eye, 1.0, 0.0).astype(BF16)
    qb_ref[LANE:2 * LANE, :] = jnp.concatenate([eye] * H, axis=1)
    m_ref[...] = jnp.full((1, L), M_FLOOR, F32)
    acc_ref[...] = jnp.zeros(acc_ref.shape, F32)
    alast = i // (AK // TQ)
    n_far = jnp.maximum(alast - 1, 0)

    def keys(j):
        k0 = pl.multiple_of(j * AK, AK)
        nb = jnp.where(sc_ref[pl.ds(k0, AK), :] >= thr, 0.0, NEG_INF).astype(BF16)
        return jnp.concatenate([ckv_ref[pl.ds(k0, AK), :], nb], axis=1)

    def value(j):
        return ckvt_ref[:, pl.ds(pl.multiple_of(j * AK, AK), AK)]

    _attend(0, n_far, keys, qb_ref, value, m_ref, acc_ref, sa_ref, sb_ref)
    _attend(n_far, alast + 1, keys, qb_ref, value, m_ref, acc_ref, sa_ref, sb_ref,
            near_fn=functools.partial(_near_add, dt_ref, i))
    acc = acc_ref[...]
    olat = (acc[:DSA_KV_LATENT] / acc[DSA_KV_LATENT:DSA_KV_LATENT + 1]).astype(BF16)
    for h in range(H):
        o_ref[:, h * TQ:(h + 1) * TQ] = jnp.dot(wuvt_ref[h], olat[:, h * TQ:(h + 1) * TQ],
                                                preferred_element_type=F32)


def _dsa(qdt, qit, wt, ki, ckv, ckvt, wuk, wuvt, dt, k_top):
    B, nQ = qdt.shape[:2]
    T = ki.shape[1]
    H = DSA_HEADS
    L = H * TQ
    R = ckvt.shape[1]
    tile = lambda r: pl.BlockSpec((None, None, r, L), lambda b, i: (b, i, 0, 0))
    return pl.pallas_call(
        functools.partial(_dsa_body, k_top=k_top),
        grid=(B, nQ),
        in_specs=[tile(DSA_HEAD_DIM), tile(LANE), tile(1),
                  pl.BlockSpec((None, T, LANE), lambda b, i: (b, 0, 0)),
                  pl.BlockSpec((None, T, DSA_KV_LATENT), lambda b, i: (b, 0, 0)),
                  pl.BlockSpec((None, R, T), lambda b, i: (b, 0, 0)),
                  pl.BlockSpec((H, DSA_KV_LATENT, DSA_HEAD_DIM), lambda b, i: (0, 0, 0)),
                  pl.BlockSpec((H, DSA_HEAD_DIM, DSA_KV_LATENT), lambda b, i: (0, 0, 0)),
                  pl.BlockSpec((5, TQ, L), lambda b, i: (0, 0, 0))],
        out_specs=tile(DSA_HEAD_DIM),
        out_shape=jax.ShapeDtypeStruct((B, nQ, DSA_HEAD_DIM, L), F32),
        scratch_shapes=[pltpu.VMEM((T, TQ), I32),
                        pltpu.VMEM((2 * LANE, L), BF16),
                        pltpu.VMEM((1, L), F32),
                        pltpu.VMEM((R, L), F32),
                        pltpu.VMEM((AK, L), F32),
                        pltpu.VMEM((AK, L), F32)],
        compiler_params=_cparams(("arbitrary",) * 2),
        name="dsa_attention",
    )(qdt, qit, wt, ki, ckv, ckvt, wuk, wuvt, dt)


def _heads_on_lanes(a, B, nQ, nh, dh):
    return a.reshape(B, nQ, TQ, nh, dh).transpose(0, 1, 4, 3, 2).reshape(B, nQ, dh, nh * TQ)


def _heads_off_lanes(a, B, nQ, nh, dh):
    return a.reshape(B, nQ, dh, nh, TQ).transpose(0, 1, 4, 3, 2).reshape(B, nQ * TQ, nh * dh)


def _token_mixers(qn, kvn, qd, ckv, qi, small, rel_bias, cmp_pe, cmp_w1, cmp_b1, cmp_w2,
                  dsa_w_uk, dsa_w_uv):
    B, T = qn.shape[:2]
    G, HPG, DH = NSA_KV_GROUPS, NSA_HPG, NSA_HEAD_DIM
    nQ = T // TQ
    n_blk = T // SLC_BLOCK
    L = HPG * TQ
    assert T % NK == 0 and n_blk <= LANE
    kv6 = kvn.reshape(B, T, 6, G, DH)

    qt = qn.reshape(B, nQ, TQ, G, HPG, DH).transpose(0, 3, 1, 5, 4, 2).reshape(B, G, nQ, DH, L)
    qt = jnp.concatenate([qt, jnp.ones((B, G, nQ, 1, L), BF16),
                          jnp.zeros((B, G, nQ, LANE - DH - 1, L), BF16)], axis=3)

    tab_nsa = rel_bias[:, :NSA_HEADS]
    dt_nsa = _near_tables(tab_nsa).reshape(5, TQ, G, L).transpose(2, 0, 1, 3)
    dt_dsa = _near_tables(rel_bias[:, NSA_HEADS:])

    kvc = _compress(kv6, cmp_pe, cmp_w1, cmp_b1, cmp_w2)
    M = T // CMP_STRIDE
    MP = M + CMP_PAD
    kc = kvc[:, 0].astype(BF16)
    flag = jnp.concatenate([jnp.full((CMP_PAD, 1), NEG_INF, F32), jnp.zeros((M, 1), F32)]).astype(BF16)
    kc_aug = jnp.concatenate(
        [jnp.pad(kc, ((0, 0), (0, 0), (CMP_PAD, 0), (0, 0))),
         jnp.broadcast_to(flag, (B, G, MP, 1)),
         jnp.zeros((B, G, MP, LANE - DH - 1), BF16)], axis=-1)
    vct = jnp.pad(kvc[:, 1].astype(BF16).transpose(0, 1, 3, 2),
                  ((0, 0), (0, 0), (0, LANE - DH), (CMP_PAD, 0)))
    mrow = jnp.arange(CMP_NEAR)[:, None]
    qcol = jnp.arange(TQ)[None, :]
    ct = _delta(tab_nsa, qcol - CMP_STRIDE * mrow + (2 * TQ - CMP_BLOCK + 1))
    ct = ct.reshape(CMP_NEAR, TQ, G, HPG).transpose(2, 0, 3, 1).reshape(G, CMP_NEAR, L)
    n_cmp = (T - CMP_BLOCK) // CMP_STRIDE + 1
    cs = jnp.arange(M) * CMP_STRIDE
    ss = jnp.arange(LANE) * SLC_BLOCK
    ovl = (jnp.clip(jnp.minimum(cs[None, :] + CMP_BLOCK, ss[:, None] + SLC_BLOCK)
                    - jnp.maximum(cs[None, :], ss[:, None]), 0, None).astype(F32) / CMP_BLOCK)
    ovl = jnp.where((jnp.arange(M)[None, :] < n_cmp) & (jnp.arange(LANE)[:, None] < n_blk), ovl, 0.0)
    ovl = jnp.pad(ovl, ((0, 0), (CMP_PAD, 0))).astype(BF16)
    oc, selneg = _nsa_compressed(qt, kc_aug, vct, ct, ovl, n_blk)

    ks = kv6[:, :, 2].transpose(0, 2, 1, 3)
    et = (jnp.arange(T)[:, None] // SLC_BLOCK == jnp.arange(LANE)[None, :]).astype(BF16)
    ks_aug = jnp.concatenate([ks, jnp.zeros((B, G, T, LANE - DH), BF16),
                              jnp.broadcast_to(et, (B, G, T, LANE))], axis=-1)
    vs = kv6[:, :, 3].transpose(0, 2, 3, 1)
    vst = jnp.concatenate([vs, jnp.ones((B, G, 1, T), BF16),
                           jnp.zeros((B, G, LANE - DH - 1, T), BF16)], axis=2)
    osel = _nsa_selected(qt, selneg, ks_aug, vst, dt_nsa)

    TP = T + WINDOW
    kw = jnp.pad(kv6[:, :, 4].transpose(0, 2, 1, 3), ((0, 0), (0, 0), (WINDOW, 0), (0, 0)))
    wflag = jnp.concatenate([jnp.full((WINDOW, 1), NEG_INF, F32), jnp.zeros((T, 1), F32)]).astype(BF16)
    kw_aug = jnp.concatenate([kw, jnp.broadcast_to(wflag, (B, G, TP, 1)),
                              jnp.zeros((B, G, TP, LANE - DH - 1), BF16)], axis=-1)
    vw = jnp.pad(kv6[:, :, 5].transpose(0, 2, 3, 1), ((0, 0), (0, 0), (0, 0), (WINDOW, 0)))
    vwt = jnp.concatenate([vw, jnp.ones((B, G, 1, TP), BF16),
                           jnp.zeros((B, G, LANE - DH - 1, TP), BF16)], axis=2)
    u = jnp.arange(TQ)[:, None]
    w0 = jnp.tile(jnp.where(u > jnp.arange(TQ)[None, :], 0.0, NEG_INF).astype(F32), (1, HPG))
    gl = small[..., 0:3 * NSA_HEADS].reshape(B, nQ, TQ, G, HPG, 3).transpose(0, 3, 1, 5, 4, 2)
    gl = jnp.pad(gl.reshape(B, G, nQ, 3, L), ((0, 0), (0, 0), (0, 0), (0, 5), (0, 0)))
    o_nsa_t = _nsa_window(qt, kw_aug, vwt, w0, dt_nsa, oc, osel, gl)
    o_nsa = o_nsa_t.reshape(B, G, nQ, DH, HPG, TQ).transpose(0, 2, 5, 1, 4, 3).reshape(B, T, NSA_HEADS * DH)

    H = DSA_HEADS
    qdt = _heads_on_lanes(qd, B, nQ, H, DSA_HEAD_DIM)
    qit = _heads_on_lanes(qi, B, nQ, IDX_HEADS, IDX_DIM)
    qit = jnp.pad(qit, ((0, 0), (0, 0), (0, LANE - IDX_DIM), (0, 0)))
    k_idx = small[..., 24:24 + IDX_DIM]
    w_idx = small[..., 24 + IDX_DIM:24 + IDX_DIM + IDX_HEADS] * IDX_HEADS ** -0.5
    ki = jnp.pad(k_idx.astype(BF16), ((0, 0), (0, 0), (0, LANE - IDX_DIM)))
    wt = w_idx.reshape(B, nQ, TQ, IDX_HEADS).transpose(0, 1, 3, 2).reshape(B, nQ, 1, IDX_HEADS * TQ)
    ckvt = jnp.concatenate([ckv.transpose(0, 2, 1), jnp.ones((B, 1, T), BF16),
                            jnp.zeros((B, 7, T), BF16)], axis=1)
    wuk = dsa_w_uk.transpose(1, 0, 2).astype(BF16)
    wuvt = dsa_w_uv.transpose(1, 2, 0).astype(BF16)
    o_dsa_t = _dsa(qdt, qit, wt, ki, ckv, ckvt, wuk, wuvt, dt_dsa, min(IDX_TOPK_MAX, T // 4))
    o_dsa = _heads_off_lanes(o_dsa_t, B, nQ, H, DSA_HEAD_DIM)
    return o_nsa, o_dsa


def _rms(v, g):
    return v * lax.rsqrt(jnp.mean(v * v, axis=-1, keepdims=True) + RMS_EPS) * g


HI_MASK = -65536


def _pack_halves(v):
    w = v.shape[1] // 2
    lo = lax.shift_right_logical(pltpu.bitcast(v[:, :w], I32), 16)
    hi = pltpu.bitcast(v[:, w:], I32) & HI_MASK
    return lo | hi


def _unpack_halves(p):
    return pltpu.bitcast(p << 16, F32), pltpu.bitcast(p & HI_MASK, F32)


def _merge_body(on_ref, od_ref, gm_ref, x_ref, wn_ref, wd_ref, wo_ref, gpost_ref, gt_ref,
                gpre_ref, sc_ref, sh_ref, rwt_ref, x1_ref, h2_ref, hpa_ref, hpb_ref, lt_ref):
    D = D_MODEL
    ya = jnp.dot(on_ref[...], wn_ref[...], preferred_element_type=F32)
    yb = jnp.dot(od_ref[...], wd_ref[...], preferred_element_type=F32)
    y = jax.nn.sigmoid(gm_ref[:, 0:D]) * ya + jax.nn.sigmoid(gm_ref[:, D:2 * D]) * yb
    y2 = jnp.dot(y.astype(BF16), wo_ref[...], preferred_element_type=F32)
    x1 = x_ref[...] + gt_ref[...] * _rms(y2, gpost_ref[...])
    x1_ref[...] = x1
    h2 = (_rms(x1, gpre_ref[...]) * (1.0 + sc_ref[...]) + sh_ref[...]).astype(BF16)
    h2_ref[...] = h2
    packed = _pack_halves(h2.astype(F32))
    hpa_ref[...] = packed[:, :SC_WIDTH]
    hpb_ref[...] = packed[:, SC_WIDTH:]
    lt_ref[...] = lax.dot_general(rwt_ref[...], h2, (((1,), (1,)), ((), ())),
                                  preferred_element_type=F32)


def _merge(o_nsa, o_dsa, gm, x, wn, wd, wo, g_post, gt1, g_pre, sc2, sh2, router_w):
    B, T, D = x.shape
    TM = 256
    E = router_w.shape[1]
    row = lambda w: pl.BlockSpec((None, TM, w), lambda b, i: (b, i, 0))
    full = lambda a: pl.BlockSpec(a.shape, lambda b, i: (0,) * a.ndim)
    vecb = pl.BlockSpec((None, 1, D), lambda b, i: (b, 0, 0))
    vec = pl.BlockSpec((1, D), lambda b, i: (0, 0))
    wn, wd, wo = wn.astype(BF16), wd.astype(BF16), wo.astype(BF16)
    rwt = router_w.T.astype(BF16)
    return pl.pallas_call(
        _merge_body,
        grid=(B, T // TM),
        in_specs=[row(o_nsa.shape[-1]), row(o_dsa.shape[-1]), row(2 * D), row(D),
                  full(wn), full(wd), full(wo), vec, vecb, vec, vecb, vecb, full(rwt)],
        out_specs=[row(D), row(D), row(SC_WIDTH), row(SC_WIDTH),
                   pl.BlockSpec((None, E, TM), lambda b, i: (b, 0, i))],
        out_shape=[jax.ShapeDtypeStruct((B, T, D), F32),
                   jax.ShapeDtypeStruct((B, T, D), BF16),
                   jax.ShapeDtypeStruct((B, T, SC_WIDTH), I32),
                   jax.ShapeDtypeStruct((B, T, SC_WIDTH), I32),
                   jax.ShapeDtypeStruct((B, E, T), F32)],
        compiler_params=_cparams(("arbitrary",) * 2),
        name="merge_postnorm",
    )(o_nsa, o_dsa, gm, x, wn, wd, wo, g_post.reshape(1, D), gt1.reshape(B, 1, D),
      g_pre.reshape(1, D), sc2.reshape(B, 1, D), sh2.reshape(B, 1, D), rwt)


TR = 256
GSZ = N_EXPERTS // N_EXPERT_GROUPS


def _first_max(v, idx, big):
    mx = jnp.max(v, axis=0, keepdims=True)
    return jnp.min(jnp.where(v == mx, idx, big), axis=0, keepdims=True), mx


def _route_body(lt_ref, bias_ref, eid_ref, rk_ref, wt_ref, cnt_ref, carry_ref):
    first_step = (pl.program_id(0) == 0) & (pl.program_id(1) == 0)

    @pl.when(first_step)
    def _():
        carry_ref[...] = jnp.zeros(carry_ref.shape, F32)

    s = jax.nn.sigmoid(lt_ref[...])
    sel = s + bias_ref[:, 0:1]
    grow = lax.broadcasted_iota(I32, (GSZ, TR), 0)
    gs = []
    for g in range(N_EXPERT_GROUPS):
        blk = sel[g * GSZ:(g + 1) * GSZ]
        f1, m1 = _first_max(blk, grow, GSZ)
        m2 = jnp.max(jnp.where(grow == f1, KNOCK, blk), axis=0, keepdims=True)
        gs.append(m1 + m2)
    gsc = jnp.concatenate(gs, axis=0)
    gidx = lax.broadcasted_iota(I32, (N_EXPERT_GROUPS, TR), 0)
    gsel = jnp.zeros((N_EXPERT_GROUPS, TR), F32)
    for _ in range(TOPK_GROUPS):
        f, _mx = _first_max(gsc, gidx, N_EXPERT_GROUPS)
        pick = gidx == f
        gsel = jnp.where(pick, 1.0, gsel)
        gsc = jnp.where(pick, KNOCK, gsc)
    cur = jnp.concatenate(
        [jnp.where(gsel[g:g + 1] > 0.0, sel[g * GSZ:(g + 1) * GSZ], NEG_INF)
         for g in range(N_EXPERT_GROUPS)], axis=0)
    erow = lax.broadcasted_iota(I32, (N_EXPERTS, TR), 0)
    member = jnp.zeros((N_EXPERTS, TR), F32)
    picks, eids, ws = [], [], []
    for _ in range(TOP_K):
        f, _mx = _first_max(cur, erow, N_EXPERTS)
        pick = erow == f
        picks.append(pick)
        eids.append(f)
        ws.append(jnp.sum(jnp.where(pick, s, 0.0), axis=0, keepdims=True))
        member = jnp.where(pick, 1.0, member)
        cur = jnp.where(pick, KNOCK, cur)
    wsum = ws[0]
    for k in range(1, TOP_K):
        wsum = wsum + ws[k]
    mb = member.astype(BF16)
    upper = (lax.broadcasted_iota(I32, (TR, TR), 0) < lax.broadcasted_iota(I32, (TR, TR), 1))
    upper = jnp.where(upper, 1.0, 0.0).astype(BF16)
    rank = jnp.dot(mb, upper, preferred_element_type=F32) + carry_ref[:, 0:1]
    eid_ref[...] = jnp.concatenate(eids, axis=0)
    rk_ref[...] = jnp.concatenate(
        [jnp.sum(jnp.where(p, rank, 0.0), axis=0, keepdims=True) for p in picks], axis=0)
    wt_ref[...] = jnp.concatenate([w / wsum * ROUTED_SCALE for w in ws], axis=0)
    carry_ref[...] = carry_ref[...] + jnp.dot(mb, jnp.ones((TR, LANE), BF16),
                                              preferred_element_type=F32)
    cnt_ref[...] = carry_ref[...]


def _route(lt, router_bias):
    B, E, T = lt.shape
    out8 = lambda: pl.BlockSpec((None, TOP_K, TR), lambda b, i: (b, 0, i))
    return pl.pallas_call(
        _route_body,
        grid=(B, T // TR),
        in_specs=[pl.BlockSpec((None, E, TR), lambda b, i: (b, 0, i)),
                  pl.BlockSpec((E, 1), lambda b, i: (0, 0))],
        out_specs=[out8(), out8(), out8(), pl.BlockSpec((E, LANE), lambda b, i: (0, 0))],
        out_shape=[jax.ShapeDtypeStruct((B, TOP_K, T), I32),
                   jax.ShapeDtypeStruct((B, TOP_K, T), F32),
                   jax.ShapeDtypeStruct((B, TOP_K, T), F32),
                   jax.ShapeDtypeStruct((E, LANE), F32)],
        scratch_shapes=[pltpu.VMEM((E, LANE), F32)],
        compiler_params=_cparams(("arbitrary",) * 2),
        name="moe_router",
    )(lt, router_bias.reshape(E, 1))


BM = 256


def _gmm_body(be_ref, nv_ref, xa_ref, xb_ref, wg_ref, wu_ref, wd_ref, ya_ref, yb_ref,
              wgb, wub, wdb):
    b = pl.program_id(0)
    valid = nv_ref[b]

    @pl.when(valid > 0)
    def _():
        prev = be_ref[jnp.maximum(b - 1, 0)]

        @pl.when((b == 0) | (be_ref[b] != prev))
        def _():
            wgb[...] = wg_ref[...].astype(BF16)
            wub[...] = wu_ref[...].astype(BF16)
            wdb[...] = wd_ref[...].astype(BF16)

        live = lax.broadcasted_iota(I32, (BM, 1), 0) < valid
        lo_a, hi_a = _unpack_halves(jnp.where(live, xa_ref[...], 0))
        lo_b, hi_b = _unpack_halves(jnp.where(live, xb_ref[...], 0))
        xb = jnp.concatenate([lo_a, lo_b, hi_a, hi_b], axis=1).astype(BF16)
        gt = jnp.dot(xb, wgb[...], preferred_element_type=F32)
        up = jnp.dot(xb, wub[...], preferred_element_type=F32)
        a = (gt * jax.nn.sigmoid(gt) * up).astype(BF16)
        y = jnp.dot(a, wdb[...], preferred_element_type=F32)
        packed = _pack_halves(y.astype(BF16).astype(F32))
        ya_ref[...] = packed[:, :SC_WIDTH]
        yb_ref[...] = packed[:, SC_WIDTH:]

    @pl.when(valid == 0)
    def _():
        ya_ref[...] = jnp.zeros(ya_ref.shape, I32)
        yb_ref[...] = jnp.zeros(yb_ref.shape, I32)


def _grouped_mlp(xs_a, xs_b, blk_e, n_valid, w_g, w_u, w_d):
    P = xs_a.shape[0]
    D, De = w_g.shape[-2:]
    nb = P // BM
    rows = pl.BlockSpec((BM, SC_WIDTH), lambda b, be, nv: (b, 0))
    wsel = lambda b, be, nv: (be[b], 0, 0)
    grid_spec = pltpu.PrefetchScalarGridSpec(
        num_scalar_prefetch=2,
        grid=(nb,),
        in_specs=[rows, rows,
                  pl.BlockSpec((None, D, De), wsel),
                  pl.BlockSpec((None, D, De), wsel),
                  pl.BlockSpec((None, De, D), wsel)],
        out_specs=[rows, rows],
        scratch_shapes=[pltpu.VMEM((D, De), BF16), pltpu.VMEM((D, De), BF16),
                        pltpu.VMEM((De, D), BF16)])
    out = jax.ShapeDtypeStruct((P, SC_WIDTH), I32)
    return pl.pallas_call(
        _gmm_body,
        grid_spec=grid_spec,
        out_shape=[out, out],
        compiler_params=_cparams(("arbitrary",)),
        name="moe_experts",
    )(blk_e, n_valid, xs_a, xs_b, w_g, w_u, w_d)


def _sc_mesh():
    return plsc.VectorSubcoreMesh(core_axis_name="c", subcore_axis_name="s")


def _sc_gather(x, idx):
    n = idx.shape[0]

    @pl.kernel(out_type=jax.ShapeDtypeStruct((n, SC_WIDTH), x.dtype), mesh=_sc_mesh())
    def gather(x_hbm, i_hbm, o_hbm):
        def body(i_vmem, o_vmem):
            pltpu.sync_copy(x_hbm.at[i_vmem.at[0]], o_vmem)

        pltpu.emit_pipeline(
            body, grid=(n // SC_WINDOW,),
            in_specs=[pl.BlockSpec((1, SC_WINDOW), lambda i: (0, i))],
            out_specs=[pl.BlockSpec((SC_WINDOW, SC_WIDTH), lambda i: (i, 0))],
            core_axis_name=("c", "s"),
            dimension_semantics=(pltpu.PARALLEL,),
        )(i_hbm, o_hbm)

    return gather(x, idx.reshape(1, n))


def _sc_dispatch(x, slot, n_out):
    B, T, W = x.shape
    K = slot.shape[1]
    tb = T // SC_WINDOW
    n = B * K * T

    @pl.kernel(out_type=jax.ShapeDtypeStruct((n_out, W), x.dtype), mesh=_sc_mesh(), scratch_types=[])
    def scatter(x_hbm, i_hbm, o_hbm):
        def body(x_vmem, i_vmem):
            pltpu.sync_copy(x_vmem, o_hbm.at[i_vmem.at[0]])

        pltpu.emit_pipeline(
            body, grid=(n // SC_WINDOW,),
            in_specs=[pl.BlockSpec((SC_WINDOW, W), lambda i: ((i // (K * tb)) * tb + i % tb, 0)),
                      pl.BlockSpec((1, SC_WINDOW), lambda i: (0, i))],
            out_specs=[],
            core_axis_name=("c", "s"),
            dimension_semantics=(pltpu.PARALLEL,),
        )(x_hbm, i_hbm)

    return scatter(x.reshape(B * T, W), slot.reshape(1, n))


def _final_body(ya_ref, yb_ref, w_ref, h_ref, x1_ref, wg_ref, wu_ref, wd_ref, g_ref, gt_ref, o_ref):
    hb = h_ref[...]
    gt = jnp.dot(hb, wg_ref[...], preferred_element_type=F32)
    up = jnp.dot(hb, wu_ref[...], preferred_element_type=F32)
    a = (gt * jax.nn.sigmoid(gt) * up).astype(BF16)
    shared = jnp.dot(a, wd_ref[...], preferred_element_type=F32)
    w = w_ref[...]
    parts = None
    for k in range(TOP_K):
        wk = w[:, k:k + 1]
        lo_a, hi_a = _unpack_halves(ya_ref[k])
        lo_b, hi_b = _unpack_halves(yb_ref[k])
        terms = [lo_a * wk, lo_b * wk, hi_a * wk, hi_b * wk]
        parts = terms if parts is None else [p + t for p, t in zip(parts, terms)]
    y = jnp.concatenate(parts, axis=1) + shared
    o_ref[...] = x1_ref[...] + gt_ref[...] * _rms(y, g_ref[...])


def _final(yg_a, yg_b, w, h2, x1, ws_g, ws_u, ws_d, g_post, gt2):
    B, T, D = x1.shape
    TM = 256 if T % 256 == 0 else TQ
    row = lambda wd: pl.BlockSpec((None, TM, wd), lambda b, i: (b, i, 0))
    krows = pl.BlockSpec((None, TOP_K, TM, SC_WIDTH), lambda b, i: (b, 0, i, 0))
    full = lambda a: pl.BlockSpec(a.shape, lambda b, i: (0,) * a.ndim)
    ws_g, ws_u, ws_d = ws_g.astype(BF16), ws_u.astype(BF16), ws_d.astype(BF16)
    return pl.pallas_call(
        _final_body,
        grid=(B, T // TM),
        in_specs=[krows, krows, row(TOP_K), row(D), row(D),
                  full(ws_g), full(ws_u), full(ws_d),
                  pl.BlockSpec((1, D), lambda b, i: (0, 0)),
                  pl.BlockSpec((None, 1, D), lambda b, i: (b, 0, 0))],
        out_specs=row(D),
        out_shape=jax.ShapeDtypeStruct((B, T, D), F32),
        compiler_params=_cparams(("arbitrary",) * 2),
        name="shared_final",
    )(yg_a, yg_b, w, h2, x1, ws_g, ws_u, ws_d, g_post.reshape(1, D), gt2.reshape(B, 1, D))


def _slot_body(eid_ref, rk_ref, ps_ref, o_ref):
    erow = lax.broadcasted_iota(I32, (N_EXPERTS, TR), 0)
    ps = ps_ref[:, 0:1]
    rows = []
    for k in range(TOP_K):
        base = jnp.sum(jnp.where(erow == eid_ref[k:k + 1, :], ps, 0.0), axis=0, keepdims=True)
        rows.append(base + rk_ref[k:k + 1, :])
    o_ref[...] = jnp.concatenate(rows, axis=0).astype(I32)


def _slots(eid, rk, pstart):
    B, K, T = eid.shape
    blk = pl.BlockSpec((None, K, TR), lambda b, i: (b, 0, i))
    return pl.pallas_call(
        _slot_body,
        grid=(B, T // TR),
        in_specs=[blk, blk, pl.BlockSpec((N_EXPERTS, 1), lambda b, i: (0, 0))],
        out_specs=blk,
        out_shape=jax.ShapeDtypeStruct((B, K, T), I32),
        compiler_params=_cparams(("arbitrary",) * 2),
        name="moe_slots",
    )(eid, rk, pstart.astype(F32).reshape(N_EXPERTS, 1))


def _moe(h2, hp_a, hp_b, lt, x1, router_bias, w_g, w_u, w_d, ws_g, ws_u, ws_d, g_post, gt2):
    B, T, D = x1.shape
    N = B * T
    eid, rk, wts, cnt = _route(lt, router_bias)
    counts = cnt[:, 0].astype(I32)
    padded = (counts + BM - 1) // BM * BM
    pend = jnp.cumsum(padded)
    pstart = pend - padded
    nb = (N * TOP_K + N_EXPERTS * (BM - 1) + BM - 1) // BM
    P = nb * BM
    row0 = jnp.arange(nb, dtype=I32) * BM
    blk_e = jnp.minimum(jnp.sum(pend[None, :] <= row0[:, None], axis=1), N_EXPERTS - 1).astype(I32)
    n_valid = jnp.clip(pstart[blk_e] + counts[blk_e] - row0, 0, BM).astype(I32)
    slot = _slots(eid, rk, pstart)
    xs_a = _sc_dispatch(hp_a, slot, P)
    xs_b = _sc_dispatch(hp_b, slot, P)
    ys_a, ys_b = _grouped_mlp(xs_a, xs_b, blk_e, n_valid, w_g, w_u, w_d)
    flat = slot.reshape(N * TOP_K)
    yg_a = _sc_gather(ys_a, flat).reshape(B, TOP_K, T, SC_WIDTH)
    yg_b = _sc_gather(ys_b, flat).reshape(B, TOP_K, T, SC_WIDTH)
    return _final(yg_a, yg_b, wts.transpose(0, 2, 1), h2, x1, ws_g, ws_u, ws_d, g_post, gt2)


def kernel(x, c, w_ada, b_ada, g_pre_mix, g_post_mix, g_pre_ffn, g_post_ffn, w_in, rel_bias, cmp_pe, cmp_w1, cmp_b1, cmp_w2, dsa_kv_norm, dsa_w_uk, dsa_w_uv, w_branch_nsa, w_branch_dsa, w_out, router_w, router_bias, w_exp_gate, w_exp_up, w_exp_down, w_sh_gate, w_sh_up, w_sh_down):
    depth = w_ada.shape[0]
    D = D_MODEL
    for l in range(depth):
        mod = _adaln(c, w_ada[l], b_ada[l])
        sh1, sc1, gt1, sh2, sc2, gt2 = [mod[:, k * D:(k + 1) * D] for k in range(N_MOD)]
        qn, kvn, qd, ckv, qi, small, gm = _input_proj(x, g_pre_mix[l], sc1, sh1, w_in[l], dsa_kv_norm[l])
        o_nsa, o_dsa = _token_mixers(qn, kvn, qd, ckv, qi, small, rel_bias, cmp_pe[l], cmp_w1[l],
                                     cmp_b1[l], cmp_w2[l], dsa_w_uk[l], dsa_w_uv[l])
        x1, h2, hp_a, hp_b, lt = _merge(o_nsa.astype(BF16), o_dsa.astype(BF16), gm, x,
                                        w_branch_nsa[l], w_branch_dsa[l], w_out[l], g_post_mix[l],
                                        gt1, g_pre_ffn[l], sc2, sh2, router_w[l])
        x = _moe(h2, hp_a, hp_b, lt, x1, router_bias[l], w_exp_gate[l], w_exp_up[l],
                 w_exp_down[l], w_sh_gate[l], w_sh_up[l], w_sh_down[l], g_post_ffn[l], gt2)
    return x
```

```python
import functools
import math

import jax
import jax.numpy as jnp
import numpy as np
from jax import lax
from jax.experimental import pallas as pl
from jax.experimental.pallas import tpu as pltpu
from jax.experimental.pallas import tpu_sc as plsc

F32 = jnp.float32
BF16 = jnp.bfloat16
I32 = jnp.int32

D_MODEL = 1024
N_MOD = 6
NSA_HEADS = 8
NSA_KV_GROUPS = 2
NSA_HPG = NSA_HEADS // NSA_KV_GROUPS
NSA_HEAD_DIM = 64
CMP_BLOCK = 32
CMP_STRIDE = 16
CMP_HIDDEN = 128
SLC_BLOCK = 64
SLC_TOPN = 16
WINDOW = 512
FORCE_BONUS = 1e4
DSA_HEADS = 8
DSA_HEAD_DIM = 64
DSA_KV_LATENT = 128
IDX_HEADS = 8
IDX_DIM = 64
IDX_TOPK_MAX = 256
REL_BUCKETS = 32
REL_MAX_DISTANCE = 128
N_EXPERTS = 256
N_EXPERT_GROUPS = 8
TOPK_GROUPS = 4
TOP_K = 8
D_EXPERT = 256
ROUTED_SCALE = 2.5
RMS_EPS = 1e-6
NEG_INF = -1e30

LANE = 128
TQ = 128
NK = 256
M_FLOOR = -1e29
KNOCK = -3e38
CMP_PAD = 16
CMP_NEAR = 24
VMEM_LIMIT = 56 * 1024 * 1024
SC_WINDOW = 128
SC_WIDTH = 256

PROJ_WIDTHS = (512, 768, 24, 512, 128, 512, 64, 8, 2048)
PROJ_OFFS = tuple(int(v) for v in np.cumsum((0,) + PROJ_WIDTHS))


def _cparams(sem):
    return pltpu.CompilerParams(dimension_semantics=sem, vmem_limit_bytes=VMEM_LIMIT)


def _mod_body(c_ref, w_ref, b_ref, o_ref):
    c = c_ref[...]
    s = c * jax.nn.sigmoid(c)
    o_ref[...] = jnp.dot(s, w_ref[...], preferred_element_type=F32) + b_ref[...]


def _adaln(c, w_ada, b_ada):
    B, D = c.shape
    E = w_ada.shape[1]
    cp = jnp.zeros((8, D), F32).at[:B].set(c)
    out = pl.pallas_call(
        _mod_body,
        grid=(E // D,),
        in_specs=[pl.BlockSpec((8, D), lambda j: (0, 0)),
                  pl.BlockSpec((D, D), lambda j: (0, j)),
                  pl.BlockSpec((1, D), lambda j: (0, j))],
        out_specs=pl.BlockSpec((8, D), lambda j: (0, j)),
        out_shape=jax.ShapeDtypeStruct((8, E), F32),
        compiler_params=_cparams(("arbitrary",)),
        name="adaln_mod",
    )(cp, w_ada, b_ada.reshape(1, E))
    return out[:B]


PW = (512, 768, 512, 128, 512, 128, 2048, 128)
PO = tuple(int(v) for v in np.cumsum((0,) + PW))


def _heads_to_lanes(x, nh):
    cols = []
    for p in range(nh // 2):
        t = x[:, p * LANE:(p + 1) * LANE].T
        cols += [t[0:NSA_HEAD_DIM], t[NSA_HEAD_DIM:LANE]]
    return jnp.concatenate(cols, axis=1)


def _lanes_to_heads(y, nh):
    outs = []
    for p in range(nh // 2):
        pair = jnp.concatenate([y[:, (2 * p) * LANE:(2 * p + 1) * LANE],
                                y[:, (2 * p + 1) * LANE:(2 * p + 2) * LANE]], axis=0)
        outs.append(pair.T)
    return jnp.concatenate(outs, axis=1)


def _proj_body(x_ref, g_ref, sc_ref, sh_ref, w_ref, kn_ref,
               qn_ref, kv_ref, qd_ref, ckv_ref, qi_ref, sm_ref, gm_ref, ki_ref):
    x = x_ref[...]
    ms = jnp.mean(x * x, axis=-1, keepdims=True)
    h = x * lax.rsqrt(ms + RMS_EPS) * g_ref[...]
    h = h * (1.0 + sc_ref[...]) + sh_ref[...]
    hb = h.astype(BF16)

    def mm(k):
        return jnp.dot(hb, w_ref[:, PO[k]:PO[k + 1]], preferred_element_type=F32)

    qn_ref[...] = (mm(0) * NSA_HEAD_DIM ** -0.5).astype(BF16)
    kv_ref[...] = mm(1).astype(BF16)
    qd_ref[...] = mm(2).astype(BF16)
    c = mm(3)
    cms = jnp.mean(c * c, axis=-1, keepdims=True)
    ckv_ref[...] = (c * lax.rsqrt(cms + RMS_EPS) * kn_ref[...]).astype(BF16)
    qi_ref[...] = (mm(4) * IDX_DIM ** -0.5).astype(BF16)
    sm_ref[...] = mm(5)
    gm_ref[...] = mm(6)
    ki_ref[...] = mm(7).astype(BF16)


def _input_proj(x, g_pre, sc, sh, w_in, kv_norm):
    B, T, D = x.shape
    TM = 512 if T % 512 == 0 else TQ
    o = PROJ_OFFS
    small = jnp.concatenate([w_in[:, o[2]:o[3]], w_in[:, o[6]:o[7]], w_in[:, o[7]:o[8]],
                             jnp.zeros((D, 32), F32)], axis=1)
    w_r = jnp.concatenate([w_in[:, o[0]:o[1]], w_in[:, o[1]:o[2]], w_in[:, o[3]:o[4]],
                           w_in[:, o[4]:o[5]], w_in[:, o[5]:o[6]], small,
                           w_in[:, o[8]:o[9]], w_in[:, o[6]:o[7]],
                           jnp.zeros((D, LANE - IDX_DIM), F32)], axis=1).astype(BF16)
    row = lambda w: pl.BlockSpec((None, TM, w), lambda b, i: (b, i, 0))
    vec = pl.BlockSpec((None, 1, D), lambda b, i: (b, 0, 0))
    outs = pl.pallas_call(
        _proj_body,
        grid=(B, T // TM),
        in_specs=[row(D),
                  pl.BlockSpec((1, D), lambda b, i: (0, 0)),
                  vec, vec,
                  pl.BlockSpec((D, PO[-1]), lambda b, i: (0, 0)),
                  pl.BlockSpec((1, DSA_KV_LATENT), lambda b, i: (0, 0))],
        out_specs=[row(w) for w in PW],
        out_shape=[jax.ShapeDtypeStruct((B, T, PW[0]), BF16),
                   jax.ShapeDtypeStruct((B, T, PW[1]), BF16),
                   jax.ShapeDtypeStruct((B, T, PW[2]), BF16),
                   jax.ShapeDtypeStruct((B, T, PW[3]), BF16),
                   jax.ShapeDtypeStruct((B, T, PW[4]), BF16),
                   jax.ShapeDtypeStruct((B, T, PW[5]), F32),
                   jax.ShapeDtypeStruct((B, T, PW[6]), F32),
                   jax.ShapeDtypeStruct((B, T, PW[7]), BF16)],
        compiler_params=_cparams(("arbitrary", "arbitrary")),
        name="prenorm_proj",
    )(x, g_pre.reshape(1, D), sc.reshape(B, 1, D), sh.reshape(B, 1, D), w_r,
      kv_norm.reshape(1, DSA_KV_LATENT))
    return outs


def _t5_bucket(dist):
    n = jnp.maximum(dist, 0)
    exact = REL_BUCKETS // 2
    nf = jnp.maximum(n, 1).astype(F32)
    large = exact + (jnp.log(nf / exact) / math.log(REL_MAX_DISTANCE / exact)
                     * (REL_BUCKETS - exact)).astype(I32)
    return jnp.where(n < exact, n, jnp.minimum(large, REL_BUCKETS - 1))


def _delta(tab, dist):
    onehot = (_t5_bucket(dist)[..., None] == jnp.arange(REL_BUCKETS)).astype(F32)
    d = jnp.einsum('...b,bh->...h', onehot, tab - tab[REL_BUCKETS - 1],
                   precision=lax.Precision.HIGHEST)
    return jnp.where((dist >= 0)[..., None], d, NEG_INF)


def _near_tables(tab):
    H = tab.shape[1]
    c = jnp.arange(TQ)[:, None]
    q = jnp.arange(TQ)[None, :]
    tabs = [jnp.full((TQ, TQ, H), NEG_INF, F32)]
    for diff in (0, 1):
        tabs.append(_delta(tab, diff * TQ + q - c))
    tabs += [jnp.zeros((TQ, TQ, H), F32)] * 2
    t = jnp.stack(tabs)
    return t.transpose(0, 1, 3, 2).reshape(5, TQ, H * TQ)


def _cmp_body(r_ref, w1_ref, c_ref, w2_ref, o_ref):
    y = jnp.dot(r_ref[...], w1_ref[...], preferred_element_type=F32)
    M = y.shape[0]
    z = y[:, :CMP_HIDDEN] + pltpu.roll(y[:, CMP_HIDDEN:], M - 1, 0) + c_ref[...]
    z = jax.nn.gelu(z)
    o_ref[...] = jnp.dot(z.astype(BF16), w2_ref[...], preferred_element_type=F32)


def _compress(kv6, cmp_pe, cmp_w1, cmp_b1, cmp_w2):
    B, T = kv6.shape[:2]
    G, DH = NSA_KV_GROUPS, NSA_HEAD_DIM
    M = T // CMP_STRIDE
    half = CMP_STRIDE * DH
    r = kv6[:, :, 0:2].transpose(0, 2, 3, 1, 4).reshape(B, 2, G, M, half)
    w1 = cmp_w1.reshape(2, 2, half, CMP_HIDDEN)
    w1c = jnp.concatenate([w1[:, 0], w1[:, 1]], axis=-1).astype(BF16)
    cvec = (jnp.einsum('wld,wldk->wk', cmp_pe, cmp_w1, precision=lax.Precision.HIGHEST)
            + cmp_b1).reshape(2, 1, CMP_HIDDEN)
    return pl.pallas_call(
        _cmp_body,
        grid=(B, 2, G),
        in_specs=[pl.BlockSpec((None, None, None, M, half), lambda b, w, g: (b, w, g, 0, 0)),
                  pl.BlockSpec((None, half, 2 * CMP_HIDDEN), lambda b, w, g: (w, 0, 0)),
                  pl.BlockSpec((None, 1, CMP_HIDDEN), lambda b, w, g: (w, 0, 0)),
                  pl.BlockSpec((None, CMP_HIDDEN, DH), lambda b, w, g: (w, 0, 0))],
        out_specs=pl.BlockSpec((None, None, None, M, DH), lambda b, w, g: (b, w, g, 0, 0)),
        out_shape=jax.ShapeDtypeStruct((B, 2, G, M, DH), F32),
        compiler_params=_cparams(("arbitrary",) * 3),
        name="kv_compress",
    )(r, w1c, cvec, cmp_w2.astype(BF16))


def _nsa_cmp_body(q_ref, kc_ref, vct_ref, ct_ref, ovl_ref, oc_ref, sel_ref, qt_ref, s_ref,
                  *, n_blk, topn):
    i = pl.program_id(2)
    L = NSA_HPG * TQ
    qt = jnp.concatenate([_heads_to_lanes(q_ref[...].astype(F32), NSA_HPG), jnp.ones((1, L), F32),
                          jnp.zeros((LANE - NSA_HEAD_DIM - 1, L), F32)], axis=0).astype(BF16)
    qt_ref[...] = qt
    s_ref[...] = jnp.dot(kc_ref[...], qt, preferred_element_type=F32)
    near = pl.ds(pl.multiple_of(i * 8, 8), CMP_NEAR)
    s_ref[near, :] = s_ref[near, :] + ct_ref[...]
    s = s_ref[...]
    MP = s.shape[0]
    row = lax.broadcasted_iota(I32, (MP, 1), 0)
    s = jnp.where(row < i * 8 + CMP_NEAR, s, NEG_INF)
    m = jnp.maximum(jnp.max(s, axis=0, keepdims=True), M_FLOOR)
    p = jnp.exp(s - m)
    l = jnp.sum(p, axis=0, keepdims=True)
    p = p / jnp.maximum(l, 1e-30)
    pb = p.astype(BF16)
    oc_ref[...] = jnp.dot(vct_ref[...], pb, preferred_element_type=F32)[:NSA_HEAD_DIM]
    psum = p[:, 0:TQ]
    for hh in range(1, NSA_HPG):
        psum = psum + p[:, hh * TQ:(hh + 1) * TQ]
    imp = jnp.dot(ovl_ref[...], psum.astype(BF16), preferred_element_type=F32)

    blk = lax.broadcasted_iota(I32, (LANE, 1), 0)
    t = i * TQ + lax.broadcasted_iota(I32, (1, TQ), 1)
    cur = t >> int(math.log2(SLC_BLOCK))
    admiss = blk <= cur
    bonus = jnp.where(blk == 0, FORCE_BONUS, 0.0)
    bonus = jnp.where(blk == cur, FORCE_BONUS, bonus)
    bonus = jnp.where(blk == cur - 1, FORCE_BONUS, bonus)
    score = jnp.where(admiss, imp + bonus, NEG_INF)
    score = jnp.where(blk < n_blk, score, KNOCK)
    blk_b = jnp.broadcast_to(blk, (LANE, TQ))
    selneg = jnp.full((LANE, TQ), NEG_INF, F32)
    for _ in range(topn):
        mx = jnp.max(score, axis=0, keepdims=True)
        first = jnp.min(jnp.where(score == mx, blk_b, LANE), axis=0, keepdims=True)
        pick = blk_b == first
        selneg = jnp.where(pick, 0.0, selneg)
        score = jnp.where(pick, KNOCK, score)
    sel_ref[...] = jnp.where(admiss, selneg, NEG_INF).astype(BF16)


def _nsa_compressed(qn, kc_aug, vct, ct, ovl, n_blk):
    B, T = qn.shape[:2]
    G, nQ = NSA_KV_GROUPS, T // TQ
    MP = kc_aug.shape[2]
    body = functools.partial(_nsa_cmp_body, n_blk=n_blk, topn=min(SLC_TOPN, n_blk))
    qtile = pl.BlockSpec((None, None, None, LANE, NSA_HPG * TQ), lambda b, g, i: (b, g, i, 0, 0))
    return pl.pallas_call(
        body,
        grid=(B, G, nQ),
        in_specs=[pl.BlockSpec((None, TQ, NSA_HPG * NSA_HEAD_DIM), lambda b, g, i: (b, i, g)),
                  pl.BlockSpec((None, None, MP, LANE), lambda b, g, i: (b, g, 0, 0)),
                  pl.BlockSpec((None, None, LANE, MP), lambda b, g, i: (b, g, 0, 0)),
                  pl.BlockSpec((None, CMP_NEAR, NSA_HPG * TQ), lambda b, g, i: (g, 0, 0)),
                  pl.BlockSpec((LANE, MP), lambda b, g, i: (0, 0))],
        out_specs=[pl.BlockSpec((None, None, None, NSA_HEAD_DIM, NSA_HPG * TQ), lambda b, g, i: (b, g, i, 0, 0)),
                   pl.BlockSpec((None, None, None, LANE, TQ), lambda b, g, i: (b, g, i, 0, 0)),
                   qtile],
        out_shape=[jax.ShapeDtypeStruct((B, G, nQ, NSA_HEAD_DIM, NSA_HPG * TQ), F32),
                   jax.ShapeDtypeStruct((B, G, nQ, LANE, TQ), BF16),
                   jax.ShapeDtypeStruct((B, G, nQ, LANE, NSA_HPG * TQ), BF16)],
        scratch_shapes=[pltpu.VMEM((MP, NSA_HPG * TQ), F32)],
        compiler_params=_cparams(("arbitrary",) * 3),
        name="nsa_compressed",
    )(qn, kc_aug, vct, ct, ovl)


def _softmax_step(s, vt, m_old, acc):
    m_new = jnp.maximum(m_old, jnp.max(s, axis=0, keepdims=True))
    alpha = jnp.exp(m_old - m_new)
    p = jnp.exp(s - m_new).astype(BF16)
    return m_new, acc * alpha + jnp.dot(vt, p, preferred_element_type=F32)


AK = 256


def _pipeline(lo, hi, produce, consume, buf_a, buf_b):
    n = hi - lo

    @pl.when(n > 0)
    def _():
        produce(lo, buf_a)

        def body(p, c):
            j = lo + 2 * p
            produce(j + 1, buf_b)
            consume(j, buf_a)
            produce(j + 2, buf_a)
            consume(j + 1, buf_b)
            return c

        lax.fori_loop(0, (n - 1) // 2, body, 0)

        @pl.when(n % 2 == 1)
        def _():
            consume(hi - 1, buf_a)

        @pl.when(n % 2 == 0)
        def _():
            produce(hi - 1, buf_b)
            consume(hi - 2, buf_a)
            consume(hi - 1, buf_b)


def _attend(lo, hi, lhs_fn, rhs_ref, value_fn, m_ref, acc_ref, sa_ref, sb_ref, near_fn=None):
    def scores(j, buf):
        s = jnp.dot(lhs_fn(j), rhs_ref[...], preferred_element_type=F32)
        buf[...] = s if near_fn is None else near_fn(s, j)

    def consume(j, buf):
        m, acc = _softmax_step(buf[...], value_fn(j), m_ref[...], acc_ref[...])
        m_ref[...] = m
        acc_ref[...] = acc

    _pipeline(lo, hi, scores, consume, sa_ref, sb_ref)


def _near_add(dt_ref, i, s, j):
    parts = []
    for sub in range(AK // TQ):
        idx = jnp.clip(i - (j * (AK // TQ) + sub) + 1, 0, 4)
        parts.append(s[sub * TQ:(sub + 1) * TQ] + dt_ref[idx])
    return jnp.concatenate(parts, axis=0)


def _nsa_sel_body(qt_ref, sel_ref, ks_ref, vst_ref, dt_ref, o_ref, qa_ref, m_ref, acc_ref,
                  sa_ref, sb_ref):
    i = pl.program_id(2)
    L = NSA_HPG * TQ
    qa_ref[0:LANE, :] = qt_ref[...]
    selneg = sel_ref[...]
    qa_ref[LANE:2 * LANE, :] = jnp.concatenate([selneg] * NSA_HPG, axis=1)
    m_ref[...] = jnp.full((1, L), M_FLOOR, F32)
    acc_ref[...] = jnp.zeros(acc_ref.shape, F32)
    last = i // (AK // TQ)
    n_far = jnp.maximum(last - 1, 0)

    def keys(j):
        return ks_ref[pl.ds(pl.multiple_of(j * AK, AK), AK), :]

    def value(j):
        return vst_ref[:, pl.ds(pl.multiple_of(j * AK, AK), AK)]

    _attend(0, n_far, keys, qa_ref, value, m_ref, acc_ref, sa_ref, sb_ref)
    _attend(n_far, last + 1, keys, qa_ref, value, m_ref, acc_ref, sa_ref, sb_ref,
            near_fn=functools.partial(_near_add, dt_ref, i))
    acc = acc_ref[...]
    o_ref[...] = acc[:NSA_HEAD_DIM] / acc[NSA_HEAD_DIM:NSA_HEAD_DIM + 1]


def _nsa_selected(qt, selneg, ks_aug, vst, dt):
    B, G, nQ = qt.shape[:3]
    T = ks_aug.shape[2]
    L = NSA_HPG * TQ
    return pl.pallas_call(
        _nsa_sel_body,
        grid=(B, G, nQ),
        in_specs=[pl.BlockSpec((None, None, None, LANE, L), lambda b, g, i: (b, g, i, 0, 0)),
                  pl.BlockSpec((None, None, None, LANE, TQ), lambda b, g, i: (b, g, i, 0, 0)),
                  pl.BlockSpec((None, None, T, 2 * LANE), lambda b, g, i: (b, g, 0, 0)),
                  pl.BlockSpec((None, None, LANE, T), lambda b, g, i: (b, g, 0, 0)),
                  pl.BlockSpec((None, 5, TQ, L), lambda b, g, i: (g, 0, 0, 0))],
        out_specs=pl.BlockSpec((None, None, None, NSA_HEAD_DIM, L), lambda b, g, i: (b, g, i, 0, 0)),
        out_shape=jax.ShapeDtypeStruct((B, G, nQ, NSA_HEAD_DIM, L), F32),
        scratch_shapes=[pltpu.VMEM((2 * LANE, L), BF16),
                        pltpu.VMEM((1, L), F32),
                        pltpu.VMEM((LANE, L), F32),
                        pltpu.VMEM((AK, L), F32),
                        pltpu.VMEM((AK, L), F32)],
        compiler_params=_cparams(("arbitrary",) * 3),
        name="nsa_selected",
    )(qt, selneg, ks_aug, vst, dt)


def _nsa_win_body(qt_ref, kw_ref, vwt_ref, w0_ref, dt_ref, oc_ref, os_ref, gl_ref, o_ref):
    i = pl.program_id(2)
    span = WINDOW + TQ
    k0 = pl.multiple_of(i * TQ, TQ)
    s = jnp.dot(kw_ref[pl.ds(k0, span), :], qt_ref[...], preferred_element_type=F32)
    s = jnp.concatenate([s[0:TQ] + w0_ref[...], s[TQ:WINDOW - TQ],
                         s[WINDOW - TQ:WINDOW] + dt_ref[2], s[WINDOW:span] + dt_ref[1]], axis=0)
    m = jnp.maximum(jnp.max(s, axis=0, keepdims=True), M_FLOOR)
    p = jnp.exp(s - m).astype(BF16)
    acc = jnp.dot(vwt_ref[:, pl.ds(k0, span)], p, preferred_element_type=F32)
    ow = acc[:NSA_HEAD_DIM] / acc[NSA_HEAD_DIM:NSA_HEAD_DIM + 1]
    g = jax.nn.sigmoid(gl_ref[...])
    o_t = g[0:1] * oc_ref[...] + g[1:2] * os_ref[...] + g[2:3] * ow
    o_ref[...] = _lanes_to_heads(o_t, NSA_HPG).astype(BF16)


def _nsa_window(qt, kw_aug, vwt, w0, dt, oc, osel, gl):
    B, G, nQ = qt.shape[:3]
    TP = kw_aug.shape[2]
    L = NSA_HPG * TQ
    W = NSA_HPG * NSA_HEAD_DIM
    tile = lambda r: pl.BlockSpec((None, None, None, r, L), lambda b, g, i: (b, g, i, 0, 0))
    return pl.pallas_call(
        _nsa_win_body,
        grid=(B, G, nQ),
        in_specs=[tile(LANE),
                  pl.BlockSpec((None, None, TP, LANE), lambda b, g, i: (b, g, 0, 0)),
                  pl.BlockSpec((None, None, LANE, TP), lambda b, g, i: (b, g, 0, 0)),
                  pl.BlockSpec((TQ, L), lambda b, g, i: (0, 0)),
                  pl.BlockSpec((None, 5, TQ, L), lambda b, g, i: (g, 0, 0, 0)),
                  tile(NSA_HEAD_DIM), tile(NSA_HEAD_DIM), tile(8)],
        out_specs=pl.BlockSpec((None, TQ, W), lambda b, g, i: (b, i, g)),
        out_shape=jax.ShapeDtypeStruct((B, nQ * TQ, G * W), BF16),
        compiler_params=_cparams(("arbitrary",) * 3),
        name="nsa_window",
    )(qt, kw_aug, vwt, w0, dt, oc, osel, gl)


INT_MIN = -2 ** 31
CK = 2 * NK
KEY_BITS = 16


def _dsa_body(qd_ref, qi_ref, wt_ref, ki_ref, ckv_ref, ckvt_ref, wuk_ref, wuvt_ref, dt_ref,
              o_ref, sc_ref, qb_ref, m_ref, acc_ref, sa_ref, sb_ref, *, k_top):
    i = pl.program_id(1)
    H = DSA_HEADS
    L = H * TQ
    last = i // (NK // TQ)
    n_steps = last + 1
    t_row = i * TQ + lax.broadcasted_iota(I32, (1, TQ), 1)

    qit = jnp.concatenate([_heads_to_lanes(qi_ref[...].astype(F32), IDX_HEADS),
                           jnp.zeros((LANE - IDX_DIM, L), F32)], axis=0).astype(BF16)
    wt = wt_ref[...]

    def idx_scores(j, buf):
        k0 = pl.multiple_of(j * NK, NK)
        buf[...] = jnp.dot(ki_ref[pl.ds(k0, NK), :], qit, preferred_element_type=F32)

    def idx_reduce(j, buf):
        k0 = pl.multiple_of(j * NK, NK)
        r = jnp.maximum(buf[...], 0.0) * wt
        sc = r[:, 0:TQ]
        for h in range(1, H):
            sc = sc + r[:, h * TQ:(h + 1) * TQ]
        bits = pltpu.bitcast(sc, I32)
        ik = jnp.where(sc == 0.0, 0, bits ^ ((bits >> 31) & 0x7FFFFFFF))
        key = k0 + lax.broadcasted_iota(I32, (NK, 1), 0)
        sc_ref[pl.ds(k0, NK), :] = jnp.where(key <= t_row, jnp.maximum(ik, INT_MIN + 1), INT_MIN)

    _pipeline(0, n_steps, idx_scores, idx_reduce, sa_ref, sb_ref)

    @pl.when(n_steps % 2 == 1)
    def _():
        sc_ref[pl.ds(pl.multiple_of(n_steps * NK, NK), NK), :] = jnp.full((NK, TQ), INT_MIN, I32)

    n_chunks = (n_steps + 1) // 2

    def count(pred):
        def cbody(j, cnt):
            k0 = pl.multiple_of(j * CK, CK)
            ind = jnp.where(pred(sc_ref[pl.ds(k0, CK), :], k0), 1.0, 0.0)
            parts = [ind[r * 8:(r + 1) * 8] for r in range(CK // 8)]
            while len(parts) > 1:
                parts = [a + b for a, b in zip(parts[0::2], parts[1::2])]
            return cnt + parts[0]
        cnt8 = lax.fori_loop(0, n_chunks, cbody, jnp.zeros((8, TQ), F32))
        return jnp.sum(cnt8, axis=0, keepdims=True)

    def any_lane(cond):
        return jnp.max(jnp.where(cond, 1.0, 0.0)) > 0.0

    kf = float(k_top)
    need = t_row + 1 > k_top

    def bit_cond(c):
        b, _tu, ct = c
        return (b < 32) & any_lane(need & (ct != kf))

    def bit_body(c):
        b, tu, ct = c
        cand = tu | jnp.left_shift(jnp.int32(1), 31 - b)
        thr_c = cand ^ INT_MIN
        cnt = count(lambda blk, k0: blk >= thr_c)
        take = cnt >= kf
        return b + 1, jnp.where(take, cand, tu), jnp.where(take, cnt, ct)

    _, tu, ct = lax.while_loop(bit_cond, bit_body,
                               (jnp.int32(0), jnp.zeros((1, TQ), I32), (t_row + 1).astype(F32)))
    thr = jnp.maximum(tu ^ INT_MIN, INT_MIN + 1)

    @pl.when(any_lane(need & (ct > kf)))
    def _():
        keep = kf - count(lambda blk, k0: blk > thr)

        def tie_count(bound):
            def pred(blk, k0):
                key = k0 + lax.broadcasted_iota(I32, (CK, 1), 0)
                return jnp.where(key < bound, blk, INT_MIN) == thr
            return count(pred)

        def pos_body(b, pos):
            cand = pos | jnp.left_shift(jnp.int32(1), KEY_BITS - 1 - b)
            return jnp.where(tie_count(cand) < keep, cand, pos)

        pos = lax.fori_loop(0, KEY_BITS, pos_body, jnp.zeros((1, TQ), I32))

        def fix(j, c):
            k0 = pl.multiple_of(j * CK, CK)
            blk = sc_ref[pl.ds(k0, CK), :]
            key = k0 + lax.broadcasted_iota(I32, (CK, 1), 0)
            drop = jnp.where(key > pos, blk, INT_MIN) == thr
            sc_ref[pl.ds(k0, CK), :] = jnp.where(drop, INT_MIN, blk)
            return c

        lax.fori_loop(0, n_chunks, fix, 0)

    qdt = _heads_to_lanes(qd_ref[...].astype(F32), H).astype(BF16)
    for h in range(H):
        ql = jnp.dot(wuk_ref[h], qdt[:, h * TQ:(h + 1) * TQ], preferred_element_type=F32)
        qb_ref[0:LANE, h * TQ:(h + 1) * TQ] = (ql * DSA_HEAD_DIM ** -0.5).astype(BF16)
    eye = (lax.broadcasted_iota(I32, (LANE, TQ), 0) == lax.broadcasted_iota(I32, (LANE, TQ), 1))
    eye = jnp.where(eye, 1.0, 0.0).astype(BF16)
    qb_ref[LANE:2 * LANE, :] = jnp.concatenate([eye] * H, axis=1)
    m_ref[...] = jnp.full((1, L), M_FLOOR, F32)
    acc_ref[...] = jnp.zeros(acc_ref.shape, F32)
    alast = i // (AK // TQ)
    n_far = jnp.maximum(alast - 1, 0)

    def keys(j):
        k0 = pl.multiple_of(j * AK, AK)
        nb = jnp.where(sc_ref[pl.ds(k0, AK), :] >= thr, 0.0, NEG_INF).astype(BF16)
        return jnp.concatenate([ckv_ref[pl.ds(k0, AK), :], nb], axis=1)

    def value(j):
        return ckvt_ref[:, pl.ds(pl.multiple_of(j * AK, AK), AK)]

    _attend(0, n_far, keys, qb_ref, value, m_ref, acc_ref, sa_ref, sb_ref)
    _attend(n_far, alast + 1, keys, qb_ref, value, m_ref, acc_ref, sa_ref, sb_ref,
            near_fn=functools.partial(_near_add, dt_ref, i))
    acc = acc_ref[...]
    olat = (acc[:DSA_KV_LATENT] / acc[DSA_KV_LATENT:DSA_KV_LATENT + 1]).astype(BF16)
    o_t = jnp.concatenate([jnp.dot(wuvt_ref[h], olat[:, h * TQ:(h + 1) * TQ],
                                   preferred_element_type=F32) for h in range(H)], axis=1)
    o_ref[...] = _lanes_to_heads(o_t, H).astype(BF16)


def _dsa(qd, qi, wt, ki, ckv, ckvt, wuk, wuvt, dt, k_top):
    B, T = ki.shape[:2]
    nQ = T // TQ
    H = DSA_HEADS
    L = H * TQ
    W = H * DSA_HEAD_DIM
    R = ckvt.shape[1]
    rows = pl.BlockSpec((None, TQ, W), lambda b, i: (b, i, 0))
    return pl.pallas_call(
        functools.partial(_dsa_body, k_top=k_top),
        grid=(B, nQ),
        in_specs=[rows, rows, pl.BlockSpec((None, None, 1, L), lambda b, i: (b, i, 0, 0)),
                  pl.BlockSpec((None, T, LANE), lambda b, i: (b, 0, 0)),
                  pl.BlockSpec((None, T, DSA_KV_LATENT), lambda b, i: (b, 0, 0)),
                  pl.BlockSpec((None, R, T), lambda b, i: (b, 0, 0)),
                  pl.BlockSpec((H, DSA_KV_LATENT, DSA_HEAD_DIM), lambda b, i: (0, 0, 0)),
                  pl.BlockSpec((H, DSA_HEAD_DIM, DSA_KV_LATENT), lambda b, i: (0, 0, 0)),
                  pl.BlockSpec((5, TQ, L), lambda b, i: (0, 0, 0))],
        out_specs=rows,
        out_shape=jax.ShapeDtypeStruct((B, T, W), BF16),
        scratch_shapes=[pltpu.VMEM((T, TQ), I32),
                        pltpu.VMEM((2 * LANE, L), BF16),
                        pltpu.VMEM((1, L), F32),
                        pltpu.VMEM((R, L), F32),
                        pltpu.VMEM((AK, L), F32),
                        pltpu.VMEM((AK, L), F32)],
        compiler_params=_cparams(("arbitrary",) * 2),
        name="dsa_attention",
    )(qd, qi, wt, ki, ckv, ckvt, wuk, wuvt, dt)


def _token_mixers(qn, kvn, qd, ckv, qi, small, ki, rel_bias, cmp_pe, cmp_w1, cmp_b1, cmp_w2,
                  dsa_w_uk, dsa_w_uv):
    B, T = qn.shape[:2]
    G, HPG, DH = NSA_KV_GROUPS, NSA_HPG, NSA_HEAD_DIM
    nQ = T // TQ
    n_blk = T // SLC_BLOCK
    L = HPG * TQ
    assert T % NK == 0 and n_blk <= LANE
    kv6 = kvn.reshape(B, T, 6, G, DH)

    tab_nsa = rel_bias[:, :NSA_HEADS]
    dt_nsa = _near_tables(tab_nsa).reshape(5, TQ, G, L).transpose(2, 0, 1, 3)
    dt_dsa = _near_tables(rel_bias[:, NSA_HEADS:])

    kvc = _compress(kv6, cmp_pe, cmp_w1, cmp_b1, cmp_w2)
    M = T // CMP_STRIDE
    MP = M + CMP_PAD
    kc = kvc[:, 0].astype(BF16)
    flag = jnp.concatenate([jnp.full((CMP_PAD, 1), NEG_INF, F32), jnp.zeros((M, 1), F32)]).astype(BF16)
    kc_aug = jnp.concatenate(
        [jnp.pad(kc, ((0, 0), (0, 0), (CMP_PAD, 0), (0, 0))),
         jnp.broadcast_to(flag, (B, G, MP, 1)),
         jnp.zeros((B, G, MP, LANE - DH - 1), BF16)], axis=-1)
    vct = jnp.pad(kvc[:, 1].astype(BF16).transpose(0, 1, 3, 2),
                  ((0, 0), (0, 0), (0, LANE - DH), (CMP_PAD, 0)))
    mrow = jnp.arange(CMP_NEAR)[:, None]
    qcol = jnp.arange(TQ)[None, :]
    ct = _delta(tab_nsa, qcol - CMP_STRIDE * mrow + (2 * TQ - CMP_BLOCK + 1))
    ct = ct.reshape(CMP_NEAR, TQ, G, HPG).transpose(2, 0, 3, 1).reshape(G, CMP_NEAR, L)
    n_cmp = (T - CMP_BLOCK) // CMP_STRIDE + 1
    cs = jnp.arange(M) * CMP_STRIDE
    ss = jnp.arange(LANE) * SLC_BLOCK
    ovl = (jnp.clip(jnp.minimum(cs[None, :] + CMP_BLOCK, ss[:, None] + SLC_BLOCK)
                    - jnp.maximum(cs[None, :], ss[:, None]), 0, None).astype(F32) / CMP_BLOCK)
    ovl = jnp.where((jnp.arange(M)[None, :] < n_cmp) & (jnp.arange(LANE)[:, None] < n_blk), ovl, 0.0)
    ovl = jnp.pad(ovl, ((0, 0), (CMP_PAD, 0))).astype(BF16)
    oc, selneg, qt = _nsa_compressed(qn, kc_aug, vct, ct, ovl, n_blk)

    ks = kv6[:, :, 2].transpose(0, 2, 1, 3)
    et = (jnp.arange(T)[:, None] // SLC_BLOCK == jnp.arange(LANE)[None, :]).astype(BF16)
    ks_aug = jnp.concatenate([ks, jnp.zeros((B, G, T, LANE - DH), BF16),
                              jnp.broadcast_to(et, (B, G, T, LANE))], axis=-1)
    vs = kv6[:, :, 3].transpose(0, 2, 3, 1)
    vst = jnp.concatenate([vs, jnp.ones((B, G, 1, T), BF16),
                           jnp.zeros((B, G, LANE - DH - 1, T), BF16)], axis=2)
    osel = _nsa_selected(qt, selneg, ks_aug, vst, dt_nsa)

    TP = T + WINDOW
    kw = jnp.pad(kv6[:, :, 4].transpose(0, 2, 1, 3), ((0, 0), (0, 0), (WINDOW, 0), (0, 0)))
    wflag = jnp.concatenate([jnp.full((WINDOW, 1), NEG_INF, F32), jnp.zeros((T, 1), F32)]).astype(BF16)
    kw_aug = jnp.concatenate([kw, jnp.broadcast_to(wflag, (B, G, TP, 1)),
                              jnp.zeros((B, G, TP, LANE - DH - 1), BF16)], axis=-1)
    vw = jnp.pad(kv6[:, :, 5].transpose(0, 2, 3, 1), ((0, 0), (0, 0), (0, 0), (WINDOW, 0)))
    vwt = jnp.concatenate([vw, jnp.ones((B, G, 1, TP), BF16),
                           jnp.zeros((B, G, LANE - DH - 1, TP), BF16)], axis=2)
    u = jnp.arange(TQ)[:, None]
    w0 = jnp.tile(jnp.where(u > jnp.arange(TQ)[None, :], 0.0, NEG_INF).astype(F32), (1, HPG))
    gl = small[..., 0:3 * NSA_HEADS].reshape(B, nQ, TQ, G, HPG, 3).transpose(0, 3, 1, 5, 4, 2)
    gl = jnp.pad(gl.reshape(B, G, nQ, 3, L), ((0, 0), (0, 0), (0, 0), (0, 5), (0, 0)))
    o_nsa = _nsa_window(qt, kw_aug, vwt, w0, dt_nsa, oc, osel, gl)

    H = DSA_HEADS
    w_idx = small[..., 24 + IDX_DIM:24 + IDX_DIM + IDX_HEADS] * IDX_HEADS ** -0.5
    wt = w_idx.reshape(B, nQ, TQ, IDX_HEADS).transpose(0, 1, 3, 2).reshape(B, nQ, 1, IDX_HEADS * TQ)
    ckvt = jnp.concatenate([ckv.transpose(0, 2, 1), jnp.ones((B, 1, T), BF16),
                            jnp.zeros((B, 7, T), BF16)], axis=1)
    wuk = dsa_w_uk.transpose(1, 0, 2).astype(BF16)
    wuvt = dsa_w_uv.transpose(1, 2, 0).astype(BF16)
    o_dsa = _dsa(qd, qi, wt, ki, ckv, ckvt, wuk, wuvt, dt_dsa, min(IDX_TOPK_MAX, T // 4))
    return o_nsa, o_dsa


def _rms(v, g):
    return v * lax.rsqrt(jnp.mean(v * v, axis=-1, keepdims=True) + RMS_EPS) * g


HI_MASK = -65536


def _pack_halves(v):
    w = v.shape[1] // 2
    lo = lax.shift_right_logical(pltpu.bitcast(v[:, :w], I32), 16)
    hi = pltpu.bitcast(v[:, w:], I32) & HI_MASK
    return lo | hi


def _unpack_halves(p):
    return pltpu.bitcast(p << 16, F32), pltpu.bitcast(p & HI_MASK, F32)


def _merge_body(on_ref, od_ref, gm_ref, x_ref, wn_ref, wd_ref, wo_ref, gpost_ref, gt_ref,
                gpre_ref, sc_ref, sh_ref, rwt_ref, x1_ref, h2_ref, hpa_ref, hpb_ref, lt_ref):
    D = D_MODEL
    ya = jnp.dot(on_ref[...], wn_ref[...], preferred_element_type=F32)
    yb = jnp.dot(od_ref[...], wd_ref[...], preferred_element_type=F32)
    y = jax.nn.sigmoid(gm_ref[:, 0:D]) * ya + jax.nn.sigmoid(gm_ref[:, D:2 * D]) * yb
    y2 = jnp.dot(y.astype(BF16), wo_ref[...], preferred_element_type=F32)
    x1 = x_ref[...] + gt_ref[...] * _rms(y2, gpost_ref[...])
    x1_ref[...] = x1
    h2 = (_rms(x1, gpre_ref[...]) * (1.0 + sc_ref[...]) + sh_ref[...]).astype(BF16)
    h2_ref[...] = h2
    packed = _pack_halves(h2.astype(F32))
    hpa_ref[...] = packed[:, :SC_WIDTH]
    hpb_ref[...] = packed[:, SC_WIDTH:]
    lt_ref[...] = lax.dot_general(rwt_ref[...], h2, (((1,), (1,)), ((), ())),
                                  preferred_element_type=F32)


def _merge(o_nsa, o_dsa, gm, x, wn, wd, wo, g_post, gt1, g_pre, sc2, sh2, router_w):
    B, T, D = x.shape
    TM = 256
    E = router_w.shape[1]
    row = lambda w: pl.BlockSpec((None, TM, w), lambda b, i: (b, i, 0))
    full = lambda a: pl.BlockSpec(a.shape, lambda b, i: (0,) * a.ndim)
    vecb = pl.BlockSpec((None, 1, D), lambda b, i: (b, 0, 0))
    vec = pl.BlockSpec((1, D), lambda b, i: (0, 0))
    wn, wd, wo = wn.astype(BF16), wd.astype(BF16), wo.astype(BF16)
    rwt = router_w.T.astype(BF16)
    return pl.pallas_call(
        _merge_body,
        grid=(B, T // TM),
        in_specs=[row(o_nsa.shape[-1]), row(o_dsa.shape[-1]), row(2 * D), row(D),
                  full(wn), full(wd), full(wo), vec, vecb, vec, vecb, vecb, full(rwt)],
        out_specs=[row(D), row(D), row(SC_WIDTH), row(SC_WIDTH),
                   pl.BlockSpec((None, E, TM), lambda b, i: (b, 0, i))],
        out_shape=[jax.ShapeDtypeStruct((B, T, D), F32),
                   jax.ShapeDtypeStruct((B, T, D), BF16),
                   jax.ShapeDtypeStruct((B, T, SC_WIDTH), I32),
                   jax.ShapeDtypeStruct((B, T, SC_WIDTH), I32),
                   jax.ShapeDtypeStruct((B, E, T), F32)],
        compiler_params=_cparams(("arbitrary",) * 2),
        name="merge_postnorm",
    )(o_nsa, o_dsa, gm, x, wn, wd, wo, g_post.reshape(1, D), gt1.reshape(B, 1, D),
      g_pre.reshape(1, D), sc2.reshape(B, 1, D), sh2.reshape(B, 1, D), rwt)


TR = 256
GSZ = N_EXPERTS // N_EXPERT_GROUPS


def _first_max(v, idx, big):
    mx = jnp.max(v, axis=0, keepdims=True)
    return jnp.min(jnp.where(v == mx, idx, big), axis=0, keepdims=True), mx


def _route_body(lt_ref, bias_ref, eid_ref, rk_ref, wt_ref, cnt_ref, carry_ref):
    first_step = (pl.program_id(0) == 0) & (pl.program_id(1) == 0)

    @pl.when(first_step)
    def _():
        carry_ref[...] = jnp.zeros(carry_ref.shape, F32)

    s = jax.nn.sigmoid(lt_ref[...])
    sel = s + bias_ref[:, 0:1]
    grow = lax.broadcasted_iota(I32, (GSZ, TR), 0)
    gs = []
    for g in range(N_EXPERT_GROUPS):
        blk = sel[g * GSZ:(g + 1) * GSZ]
        f1, m1 = _first_max(blk, grow, GSZ)
        m2 = jnp.max(jnp.where(grow == f1, KNOCK, blk), axis=0, keepdims=True)
        gs.append(m1 + m2)
    gsc = jnp.concatenate(gs, axis=0)
    gidx = lax.broadcasted_iota(I32, (N_EXPERT_GROUPS, TR), 0)
    gsel = jnp.zeros((N_EXPERT_GROUPS, TR), F32)
    for _ in range(TOPK_GROUPS):
        f, _mx = _first_max(gsc, gidx, N_EXPERT_GROUPS)
        pick = gidx == f
        gsel = jnp.where(pick, 1.0, gsel)
        gsc = jnp.where(pick, KNOCK, gsc)
    cur = jnp.concatenate(
        [jnp.where(gsel[g:g + 1] > 0.0, sel[g * GSZ:(g + 1) * GSZ], NEG_INF)
         for g in range(N_EXPERT_GROUPS)], axis=0)
    erow = lax.broadcasted_iota(I32, (N_EXPERTS, TR), 0)
    member = jnp.zeros((N_EXPERTS, TR), F32)
    picks, eids, ws = [], [], []
    for _ in range(TOP_K):
        f, _mx = _first_max(cur, erow, N_EXPERTS)
        pick = erow == f
        picks.append(pick)
        eids.append(f)
        ws.append(jnp.sum(jnp.where(pick, s, 0.0), axis=0, keepdims=True))
        member = jnp.where(pick, 1.0, member)
        cur = jnp.where(pick, KNOCK, cur)
    wsum = ws[0]
    for k in range(1, TOP_K):
        wsum = wsum + ws[k]
    mb = member.astype(BF16)
    upper = (lax.broadcasted_iota(I32, (TR, TR), 0) < lax.broadcasted_iota(I32, (TR, TR), 1))
    upper = jnp.where(upper, 1.0, 0.0).astype(BF16)
    rank = jnp.dot(mb, upper, preferred_element_type=F32) + carry_ref[:, 0:1]
    eid_ref[...] = jnp.concatenate(eids, axis=0)
    rk_ref[...] = jnp.concatenate(
        [jnp.sum(jnp.where(p, rank, 0.0), axis=0, keepdims=True) for p in picks], axis=0)
    wt_ref[...] = jnp.concatenate([w / wsum * ROUTED_SCALE for w in ws], axis=0)
    carry_ref[...] = carry_ref[...] + jnp.dot(mb, jnp.ones((TR, LANE), BF16),
                                              preferred_element_type=F32)
    cnt_ref[...] = carry_ref[...]


def _route(lt, router_bias):
    B, E, T = lt.shape
    out8 = lambda: pl.BlockSpec((None, TOP_K, TR), lambda b, i: (b, 0, i))
    return pl.pallas_call(
        _route_body,
        grid=(B, T // TR),
        in_specs=[pl.BlockSpec((None, E, TR), lambda b, i: (b, 0, i)),
                  pl.BlockSpec((E, 1), lambda b, i: (0, 0))],
        out_specs=[out8(), out8(), out8(), pl.BlockSpec((E, LANE), lambda b, i: (0, 0))],
        out_shape=[jax.ShapeDtypeStruct((B, TOP_K, T), I32),
                   jax.ShapeDtypeStruct((B, TOP_K, T), F32),
                   jax.ShapeDtypeStruct((B, TOP_K, T), F32),
                   jax.ShapeDtypeStruct((E, LANE), F32)],
        scratch_shapes=[pltpu.VMEM((E, LANE), F32)],
        compiler_params=_cparams(("arbitrary",) * 2),
        name="moe_router",
    )(lt, router_bias.reshape(E, 1))


BM = 256


def _gmm_body(be_ref, nv_ref, xa_ref, xb_ref, wg_ref, wu_ref, wd_ref, ya_ref, yb_ref,
              wgb, wub, wdb):
    b = pl.program_id(0)
    valid = nv_ref[b]

    @pl.when(valid > 0)
    def _():
        prev = be_ref[jnp.maximum(b - 1, 0)]

        @pl.when((b == 0) | (be_ref[b] != prev))
        def _():
            wgb[...] = wg_ref[...].astype(BF16)
            wub[...] = wu_ref[...].astype(BF16)
            wdb[...] = wd_ref[...].astype(BF16)

        for r0 in range(0, BM, BM // 2):
            rs = slice(r0, r0 + BM // 2)
            live = r0 + lax.broadcasted_iota(I32, (BM // 2, 1), 0) < valid
            lo_a, hi_a = _unpack_halves(jnp.where(live, xa_ref[rs, :], 0))
            lo_b, hi_b = _unpack_halves(jnp.where(live, xb_ref[rs, :], 0))
            xb = jnp.concatenate([lo_a, lo_b, hi_a, hi_b], axis=1).astype(BF16)
            gt = jnp.dot(xb, wgb[...], preferred_element_type=F32)
            up = jnp.dot(xb, wub[...], preferred_element_type=F32)
            a = (gt * jax.nn.sigmoid(gt) * up).astype(BF16)
            y = jnp.dot(a, wdb[...], preferred_element_type=F32)
            packed = _pack_halves(y.astype(BF16).astype(F32))
            ya_ref[rs, :] = packed[:, :SC_WIDTH]
            yb_ref[rs, :] = packed[:, SC_WIDTH:]

    @pl.when(valid == 0)
    def _():
        ya_ref[...] = jnp.zeros(ya_ref.shape, I32)
        yb_ref[...] = jnp.zeros(yb_ref.shape, I32)


def _grouped_mlp(xs_a, xs_b, blk_e, n_valid, w_g, w_u, w_d):
    P = xs_a.shape[0]
    D, De = w_g.shape[-2:]
    nb = P // BM
    rows = pl.BlockSpec((BM, SC_WIDTH), lambda b, be, nv: (b, 0))
    wsel = lambda b, be, nv: (be[b], 0, 0)
    grid_spec = pltpu.PrefetchScalarGridSpec(
        num_scalar_prefetch=2,
        grid=(nb,),
        in_specs=[rows, rows,
                  pl.BlockSpec((None, D, De), wsel),
                  pl.BlockSpec((None, D, De), wsel),
                  pl.BlockSpec((None, De, D), wsel)],
        out_specs=[rows, rows],
        scratch_shapes=[pltpu.VMEM((D, De), BF16), pltpu.VMEM((D, De), BF16),
                        pltpu.VMEM((De, D), BF16)])
    out = jax.ShapeDtypeStruct((P, SC_WIDTH), I32)
    return pl.pallas_call(
        _gmm_body,
        grid_spec=grid_spec,
        out_shape=[out, out],
        compiler_params=_cparams(("arbitrary",)),
        name="moe_experts",
    )(blk_e, n_valid, xs_a, xs_b, w_g, w_u, w_d)


def _sc_mesh():
    return plsc.VectorSubcoreMesh(core_axis_name="c", subcore_axis_name="s")


def _sc_gather(x, idx):
    n = idx.shape[0]

    @pl.kernel(out_type=jax.ShapeDtypeStruct((n, SC_WIDTH), x.dtype), mesh=_sc_mesh())
    def gather(x_hbm, i_hbm, o_hbm):
        def body(i_vmem, o_vmem):
            pltpu.sync_copy(x_hbm.at[i_vmem.at[0]], o_vmem)

        pltpu.emit_pipeline(
            body, grid=(n // SC_WINDOW,),
            in_specs=[pl.BlockSpec((1, SC_WINDOW), lambda i: (0, i))],
            out_specs=[pl.BlockSpec((SC_WINDOW, SC_WIDTH), lambda i: (i, 0))],
            core_axis_name=("c", "s"),
            dimension_semantics=(pltpu.PARALLEL,),
        )(i_hbm, o_hbm)

    return gather(x, idx.reshape(1, n))


def _sc_dispatch(x, slot, n_out):
    B, T, W = x.shape
    K = slot.shape[1]
    tb = T // SC_WINDOW
    n = B * K * T

    @pl.kernel(out_type=jax.ShapeDtypeStruct((n_out, W), x.dtype), mesh=_sc_mesh(), scratch_types=[])
    def scatter(x_hbm, i_hbm, o_hbm):
        def body(x_vmem, i_vmem):
            pltpu.sync_copy(x_vmem, o_hbm.at[i_vmem.at[0]])

        pltpu.emit_pipeline(
            body, grid=(n // SC_WINDOW,),
            in_specs=[pl.BlockSpec((SC_WINDOW, W), lambda i: ((i // (K * tb)) * tb + i % tb, 0)),
                      pl.BlockSpec((1, SC_WINDOW), lambda i: (0, i))],
            out_specs=[],
            core_axis_name=("c", "s"),
            dimension_semantics=(pltpu.PARALLEL,),
        )(x_hbm, i_hbm)

    return scatter(x.reshape(B * T, W), slot.reshape(1, n))


def _final_body(ya_ref, yb_ref, w_ref, h_ref, x1_ref, wg_ref, wu_ref, wd_ref, g_ref, gt_ref, o_ref):
    hb = h_ref[...]
    gt = jnp.dot(hb, wg_ref[...], preferred_element_type=F32)
    up = jnp.dot(hb, wu_ref[...], preferred_element_type=F32)
    a = (gt * jax.nn.sigmoid(gt) * up).astype(BF16)
    shared = jnp.dot(a, wd_ref[...], preferred_element_type=F32)
    w = w_ref[...]
    parts = None
    for k in range(TOP_K):
        wk = w[:, k:k + 1]
        lo_a, hi_a = _unpack_halves(ya_ref[k])
        lo_b, hi_b = _unpack_halves(yb_ref[k])
        terms = [lo_a * wk, lo_b * wk, hi_a * wk, hi_b * wk]
        parts = terms if parts is None else [p + t for p, t in zip(parts, terms)]
    y = jnp.concatenate(parts, axis=1) + shared
    o_ref[...] = x1_ref[...] + gt_ref[...] * _rms(y, g_ref[...])


def _final(yg_a, yg_b, w, h2, x1, ws_g, ws_u, ws_d, g_post, gt2):
    B, T, D = x1.shape
    TM = 256 if T % 256 == 0 else TQ
    row = lambda wd: pl.BlockSpec((None, TM, wd), lambda b, i: (b, i, 0))
    krows = pl.BlockSpec((None, TOP_K, TM, SC_WIDTH), lambda b, i: (b, 0, i, 0))
    full = lambda a: pl.BlockSpec(a.shape, lambda b, i: (0,) * a.ndim)
    ws_g, ws_u, ws_d = ws_g.astype(BF16), ws_u.astype(BF16), ws_d.astype(BF16)
    return pl.pallas_call(
        _final_body,
        grid=(B, T // TM),
        in_specs=[krows, krows, row(TOP_K), row(D), row(D),
                  full(ws_g), full(ws_u), full(ws_d),
                  pl.BlockSpec((1, D), lambda b, i: (0, 0)),
                  pl.BlockSpec((None, 1, D), lambda b, i: (b, 0, 0))],
        out_specs=row(D),
        out_shape=jax.ShapeDtypeStruct((B, T, D), F32),
        compiler_params=_cparams(("arbitrary",) * 2),
        name="shared_final",
    )(yg_a, yg_b, w, h2, x1, ws_g, ws_u, ws_d, g_post.reshape(1, D), gt2.reshape(B, 1, D))


def _slot_body(eid_ref, rk_ref, ps_ref, o_ref):
    erow = lax.broadcasted_iota(I32, (N_EXPERTS, TR), 0)
    ps = ps_ref[:, 0:1]
    rows = []
    for k in range(TOP_K):
        base = jnp.sum(jnp.where(erow == eid_ref[k:k + 1, :], ps, 0.0), axis=0, keepdims=True)
        rows.append(base + rk_ref[k:k + 1, :])
    o_ref[...] = jnp.concatenate(rows, axis=0).astype(I32)


def _slots(eid, rk, pstart):
    B, K, T = eid.shape
    blk = pl.BlockSpec((None, K, TR), lambda b, i: (b, 0, i))
    return pl.pallas_call(
        _slot_body,
        grid=(B, T // TR),
        in_specs=[blk, blk, pl.BlockSpec((N_EXPERTS, 1), lambda b, i: (0, 0))],
        out_specs=blk,
        out_shape=jax.ShapeDtypeStruct((B, K, T), I32),
        compiler_params=_cparams(("arbitrary",) * 2),
        name="moe_slots",
    )(eid, rk, pstart.astype(F32).reshape(N_EXPERTS, 1))


def _moe(h2, hp_a, hp_b, lt, x1, router_bias, w_g, w_u, w_d, ws_g, ws_u, ws_d, g_post, gt2):
    B, T, D = x1.shape
    N = B * T
    eid, rk, wts, cnt = _route(lt, router_bias)
    counts = cnt[:, 0].astype(I32)
    padded = (counts + BM - 1) // BM * BM
    pend = jnp.cumsum(padded)
    pstart = pend - padded
    nb = (N * TOP_K + N_EXPERTS * (BM - 1) + BM - 1) // BM
    P = nb * BM
    row0 = jnp.arange(nb, dtype=I32) * BM
    blk_e = jnp.minimum(jnp.sum(pend[None, :] <= row0[:, None], axis=1), N_EXPERTS - 1).astype(I32)
    n_valid = jnp.clip(pstart[blk_e] + counts[blk_e] - row0, 0, BM).astype(I32)
    slot = _slots(eid, rk, pstart)
    xs_a = _sc_dispatch(hp_a, slot, P)
    xs_b = _sc_dispatch(hp_b, slot, P)
    ys_a, ys_b = _grouped_mlp(xs_a, xs_b, blk_e, n_valid, w_g, w_u, w_d)
    flat = slot.reshape(N * TOP_K)
    yg_a = _sc_gather(ys_a, flat).reshape(B, TOP_K, T, SC_WIDTH)
    yg_b = _sc_gather(ys_b, flat).reshape(B, TOP_K, T, SC_WIDTH)
    return _final(yg_a, yg_b, wts.transpose(0, 2, 1), h2, x1, ws_g, ws_u, ws_d, g_post, gt2)


def kernel(x, c, w_ada, b_ada, g_pre_mix, g_post_mix, g_pre_ffn, g_post_ffn, w_in, rel_bias, cmp_pe, cmp_w1, cmp_b1, cmp_w2, dsa_kv_norm, dsa_w_uk, dsa_w_uv, w_branch_nsa, w_branch_dsa, w_out, router_w, router_bias, w_exp_gate, w_exp_up, w_exp_down, w_sh_gate, w_sh_up, w_sh_down):
    depth = w_ada.shape[0]
    D = D_MODEL
    for l in range(depth):
        mod = _adaln(c, w_ada[l], b_ada[l])
        sh1, sc1, gt1, sh2, sc2, gt2 = [mod[:, k * D:(k + 1) * D] for k in range(N_MOD)]
        qn, kvn, qd, ckv, qi, small, gm, ki = _input_proj(x, g_pre_mix[l], sc1, sh1, w_in[l],
                                                          dsa_kv_norm[l])
        o_nsa, o_dsa = _token_mixers(qn, kvn, qd, ckv, qi, small, ki, rel_bias, cmp_pe[l], cmp_w1[l],
                                     cmp_b1[l], cmp_w2[l], dsa_w_uk[l], dsa_w_uv[l])
        x1, h2, hp_a, hp_b, lt = _merge(o_nsa, o_dsa, gm, x,
                                        w_branch_nsa[l], w_branch_dsa[l], w_out[l], g_post_mix[l],
                                        gt1, g_pre_ffn[l], sc2, sh2, router_w[l])
        x = _moe(h2, hp_a, hp_b, lt, x1, router_bias[l], w_exp_gate[l], w_exp_up[l],
                 w_exp_down[l], w_sh_gate[l], w_sh_up[l], w_sh_down[l], g_post_ffn[l], gt2)
    return x
```

```python
import functools
import math

import jax
import jax.numpy as jnp
import numpy as np
from jax import lax
from jax.experimental import pallas as pl
from jax.experimental.pallas import tpu as pltpu
from jax.experimental.pallas import tpu_sc as plsc

F32 = jnp.float32
BF16 = jnp.bfloat16
I32 = jnp.int32

D_MODEL = 1024
N_MOD = 6
NSA_HEADS = 8
NSA_KV_GROUPS = 2
NSA_HPG = NSA_HEADS // NSA_KV_GROUPS
NSA_HEAD_DIM = 64
CMP_BLOCK = 32
CMP_STRIDE = 16
CMP_HIDDEN = 128
SLC_BLOCK = 64
SLC_TOPN = 16
WINDOW = 512
FORCE_BONUS = 1e4
DSA_HEADS = 8
DSA_HEAD_DIM = 64
DSA_KV_LATENT = 128
IDX_HEADS = 8
IDX_DIM = 64
IDX_TOPK_MAX = 256
REL_BUCKETS = 32
REL_MAX_DISTANCE = 128
N_EXPERTS = 256
N_EXPERT_GROUPS = 8
TOPK_GROUPS = 4
TOP_K = 8
D_EXPERT = 256
ROUTED_SCALE = 2.5
RMS_EPS = 1e-6
NEG_INF = -1e30

LANE = 128
TQ = 128
NK = 256
M_FLOOR = -1e29
KNOCK = -3e38
CMP_PAD = 16
CMP_NEAR = 24
VMEM_LIMIT = 56 * 1024 * 1024
SC_WINDOW = 128
SC_WIDTH = 256

PROJ_WIDTHS = (512, 768, 24, 512, 128, 512, 64, 8, 2048)
PROJ_OFFS = tuple(int(v) for v in np.cumsum((0,) + PROJ_WIDTHS))


def _cparams(sem):
    return pltpu.CompilerParams(dimension_semantics=sem, vmem_limit_bytes=VMEM_LIMIT)


def _mod_body(c_ref, w_ref, b_ref, o_ref):
    c = c_ref[...]
    s = c * jax.nn.sigmoid(c)
    o_ref[...] = jnp.dot(s, w_ref[...], preferred_element_type=F32) + b_ref[...]


def _adaln(c, w_ada, b_ada):
    B, D = c.shape
    E = w_ada.shape[1]
    cp = jnp.zeros((8, D), F32).at[:B].set(c)
    out = pl.pallas_call(
        _mod_body,
        grid=(E // D,),
        in_specs=[pl.BlockSpec((8, D), lambda j: (0, 0)),
                  pl.BlockSpec((D, D), lambda j: (0, j)),
                  pl.BlockSpec((1, D), lambda j: (0, j))],
        out_specs=pl.BlockSpec((8, D), lambda j: (0, j)),
        out_shape=jax.ShapeDtypeStruct((8, E), F32),
        compiler_params=_cparams(("arbitrary",)),
        name="adaln_mod",
    )(cp, w_ada, b_ada.reshape(1, E))
    return out[:B]


PW = (512, 768, 512, 128, 512, 128, 2048, 128)
PO = tuple(int(v) for v in np.cumsum((0,) + PW))


def _heads_to_lanes(x, nh):
    cols = []
    for p in range(nh // 2):
        t = x[:, p * LANE:(p + 1) * LANE].T
        cols += [t[0:NSA_HEAD_DIM], t[NSA_HEAD_DIM:LANE]]
    return jnp.concatenate(cols, axis=1)


def _lanes_to_heads(y, nh):
    outs = []
    for p in range(nh // 2):
        pair = jnp.concatenate([y[:, (2 * p) * LANE:(2 * p + 1) * LANE],
                                y[:, (2 * p + 1) * LANE:(2 * p + 2) * LANE]], axis=0)
        outs.append(pair.T)
    return jnp.concatenate(outs, axis=1)


def _proj_body(x_ref, g_ref, sc_ref, sh_ref, w_ref, kn_ref,
               qn_ref, kv_ref, qd_ref, ckv_ref, qi_ref, sm_ref, gm_ref, ki_ref):
    x = x_ref[...]
    ms = jnp.mean(x * x, axis=-1, keepdims=True)
    h = x * lax.rsqrt(ms + RMS_EPS) * g_ref[...]
    h = h * (1.0 + sc_ref[...]) + sh_ref[...]
    hb = h.astype(BF16)

    def mm(k):
        return jnp.dot(hb, w_ref[:, PO[k]:PO[k + 1]], preferred_element_type=F32)

    qn_ref[...] = (mm(0) * NSA_HEAD_DIM ** -0.5).astype(BF16)
    kv_ref[...] = mm(1).astype(BF16)
    qd_ref[...] = mm(2).astype(BF16)
    c = mm(3)
    cms = jnp.mean(c * c, axis=-1, keepdims=True)
    ckv_ref[...] = (c * lax.rsqrt(cms + RMS_EPS) * kn_ref[...]).astype(BF16)
    qi_ref[...] = (mm(4) * IDX_DIM ** -0.5).astype(BF16)
    sm_ref[...] = mm(5)
    gm_ref[...] = mm(6)
    ki_ref[...] = mm(7).astype(BF16)


def _input_proj(x, g_pre, sc, sh, w_in, kv_norm):
    B, T, D = x.shape
    TM = 512 if T % 512 == 0 else TQ
    o = PROJ_OFFS
    small = jnp.concatenate([w_in[:, o[2]:o[3]], w_in[:, o[6]:o[7]], w_in[:, o[7]:o[8]],
                             jnp.zeros((D, 32), F32)], axis=1)
    w_r = jnp.concatenate([w_in[:, o[0]:o[1]], w_in[:, o[1]:o[2]], w_in[:, o[3]:o[4]],
                           w_in[:, o[4]:o[5]], w_in[:, o[5]:o[6]], small,
                           w_in[:, o[8]:o[9]], w_in[:, o[6]:o[7]],
                           jnp.zeros((D, LANE - IDX_DIM), F32)], axis=1).astype(BF16)
    row = lambda w: pl.BlockSpec((None, TM, w), lambda b, i: (b, i, 0))
    vec = pl.BlockSpec((None, 1, D), lambda b, i: (b, 0, 0))
    outs = pl.pallas_call(
        _proj_body,
        grid=(B, T // TM),
        in_specs=[row(D),
                  pl.BlockSpec((1, D), lambda b, i: (0, 0)),
                  vec, vec,
                  pl.BlockSpec((D, PO[-1]), lambda b, i: (0, 0)),
                  pl.BlockSpec((1, DSA_KV_LATENT), lambda b, i: (0, 0))],
        out_specs=[row(w) for w in PW],
        out_shape=[jax.ShapeDtypeStruct((B, T, PW[0]), BF16),
                   jax.ShapeDtypeStruct((B, T, PW[1]), BF16),
                   jax.ShapeDtypeStruct((B, T, PW[2]), BF16),
                   jax.ShapeDtypeStruct((B, T, PW[3]), BF16),
                   jax.ShapeDtypeStruct((B, T, PW[4]), BF16),
                   jax.ShapeDtypeStruct((B, T, PW[5]), F32),
                   jax.ShapeDtypeStruct((B, T, PW[6]), F32),
                   jax.ShapeDtypeStruct((B, T, PW[7]), BF16)],
        compiler_params=_cparams(("arbitrary", "arbitrary")),
        name="prenorm_proj",
    )(x, g_pre.reshape(1, D), sc.reshape(B, 1, D), sh.reshape(B, 1, D), w_r,
      kv_norm.reshape(1, DSA_KV_LATENT))
    return outs


def _t5_bucket(dist):
    n = jnp.maximum(dist, 0)
    exact = REL_BUCKETS // 2
    nf = jnp.maximum(n, 1).astype(F32)
    large = exact + (jnp.log(nf / exact) / math.log(REL_MAX_DISTANCE / exact)
                     * (REL_BUCKETS - exact)).astype(I32)
    return jnp.where(n < exact, n, jnp.minimum(large, REL_BUCKETS - 1))


def _delta(tab, dist):
    onehot = (_t5_bucket(dist)[..., None] == jnp.arange(REL_BUCKETS)).astype(F32)
    d = jnp.einsum('...b,bh->...h', onehot, tab - tab[REL_BUCKETS - 1],
                   precision=lax.Precision.HIGHEST)
    return jnp.where((dist >= 0)[..., None], d, NEG_INF)


def _near_tables(tab):
    H = tab.shape[1]
    c = jnp.arange(TQ)[:, None]
    q = jnp.arange(TQ)[None, :]
    tabs = [jnp.full((TQ, TQ, H), NEG_INF, F32)]
    for diff in (0, 1):
        tabs.append(_delta(tab, diff * TQ + q - c))
    tabs += [jnp.zeros((TQ, TQ, H), F32)] * 2
    t = jnp.stack(tabs)
    return t.transpose(0, 1, 3, 2).reshape(5, TQ, H * TQ)


def _cmp_body(r_ref, w1_ref, c_ref, w2_ref, o_ref):
    y = jnp.dot(r_ref[...], w1_ref[...], preferred_element_type=F32)
    M = y.shape[0]
    z = y[:, :CMP_HIDDEN] + pltpu.roll(y[:, CMP_HIDDEN:], M - 1, 0) + c_ref[...]
    z = jax.nn.gelu(z)
    o_ref[...] = jnp.dot(z.astype(BF16), w2_ref[...], preferred_element_type=F32)


def _compress(kv6, cmp_pe, cmp_w1, cmp_b1, cmp_w2):
    B, T = kv6.shape[:2]
    G, DH = NSA_KV_GROUPS, NSA_HEAD_DIM
    M = T // CMP_STRIDE
    half = CMP_STRIDE * DH
    r = kv6[:, :, 0:2].transpose(0, 2, 3, 1, 4).reshape(B, 2, G, M, half)
    w1 = cmp_w1.reshape(2, 2, half, CMP_HIDDEN)
    w1c = jnp.concatenate([w1[:, 0], w1[:, 1]], axis=-1).astype(BF16)
    cvec = (jnp.einsum('wld,wldk->wk', cmp_pe, cmp_w1, precision=lax.Precision.HIGHEST)
            + cmp_b1).reshape(2, 1, CMP_HIDDEN)
    return pl.pallas_call(
        _cmp_body,
        grid=(B, 2, G),
        in_specs=[pl.BlockSpec((None, None, None, M, half), lambda b, w, g: (b, w, g, 0, 0)),
                  pl.BlockSpec((None, half, 2 * CMP_HIDDEN), lambda b, w, g: (w, 0, 0)),
                  pl.BlockSpec((None, 1, CMP_HIDDEN), lambda b, w, g: (w, 0, 0)),
                  pl.BlockSpec((None, CMP_HIDDEN, DH), lambda b, w, g: (w, 0, 0))],
        out_specs=pl.BlockSpec((None, None, None, M, DH), lambda b, w, g: (b, w, g, 0, 0)),
        out_shape=jax.ShapeDtypeStruct((B, 2, G, M, DH), F32),
        compiler_params=_cparams(("arbitrary",) * 3),
        name="kv_compress",
    )(r, w1c, cvec, cmp_w2.astype(BF16))


def _nsa_cmp_body(q_ref, kc_ref, vct_ref, ct_ref, ovl_ref, oc_ref, sel_ref, qt_ref, s_ref,
                  *, n_blk, topn):
    i = pl.program_id(2)
    L = NSA_HPG * TQ
    qt = jnp.concatenate([_heads_to_lanes(q_ref[...].astype(F32), NSA_HPG), jnp.ones((1, L), F32),
                          jnp.zeros((LANE - NSA_HEAD_DIM - 1, L), F32)], axis=0).astype(BF16)
    qt_ref[...] = qt
    s_ref[...] = jnp.dot(kc_ref[...], qt, preferred_element_type=F32)
    near = pl.ds(pl.multiple_of(i * 8, 8), CMP_NEAR)
    s_ref[near, :] = s_ref[near, :] + ct_ref[...]
    s = s_ref[...]
    MP = s.shape[0]
    row = lax.broadcasted_iota(I32, (MP, 1), 0)
    s = jnp.where(row < i * 8 + CMP_NEAR, s, NEG_INF)
    m = jnp.maximum(jnp.max(s, axis=0, keepdims=True), M_FLOOR)
    p = jnp.exp(s - m)
    l = jnp.sum(p, axis=0, keepdims=True)
    p = p / jnp.maximum(l, 1e-30)
    pb = p.astype(BF16)
    oc_ref[...] = jnp.dot(vct_ref[...], pb, preferred_element_type=F32)[:NSA_HEAD_DIM]
    psum = p[:, 0:TQ]
    for hh in range(1, NSA_HPG):
        psum = psum + p[:, hh * TQ:(hh + 1) * TQ]
    imp = jnp.dot(ovl_ref[...], psum.astype(BF16), preferred_element_type=F32)

    blk = lax.broadcasted_iota(I32, (LANE, 1), 0)
    t = i * TQ + lax.broadcasted_iota(I32, (1, TQ), 1)
    cur = t >> int(math.log2(SLC_BLOCK))
    admiss = blk <= cur
    bonus = jnp.where(blk == 0, FORCE_BONUS, 0.0)
    bonus = jnp.where(blk == cur, FORCE_BONUS, bonus)
    bonus = jnp.where(blk == cur - 1, FORCE_BONUS, bonus)
    score = jnp.where(admiss, imp + bonus, NEG_INF)
    score = jnp.where(blk < n_blk, score, KNOCK)
    blk_b = jnp.broadcast_to(blk, (LANE, TQ))
    selneg = jnp.full((LANE, TQ), NEG_INF, F32)
    for _ in range(topn):
        mx = jnp.max(score, axis=0, keepdims=True)
        first = jnp.min(jnp.where(score == mx, blk_b, LANE), axis=0, keepdims=True)
        pick = blk_b == first
        selneg = jnp.where(pick, 0.0, selneg)
        score = jnp.where(pick, KNOCK, score)
    sel_ref[...] = jnp.where(admiss, selneg, NEG_INF).astype(BF16)


def _nsa_compressed(qn, kc_aug, vct, ct, ovl, n_blk):
    B, T = qn.shape[:2]
    G, nQ = NSA_KV_GROUPS, T // TQ
    MP = kc_aug.shape[2]
    body = functools.partial(_nsa_cmp_body, n_blk=n_blk, topn=min(SLC_TOPN, n_blk))
    qtile = pl.BlockSpec((None, None, None, LANE, NSA_HPG * TQ), lambda b, g, i: (b, g, i, 0, 0))
    return pl.pallas_call(
        body,
        grid=(B, G, nQ),
        in_specs=[pl.BlockSpec((None, TQ, NSA_HPG * NSA_HEAD_DIM), lambda b, g, i: (b, i, g)),
                  pl.BlockSpec((None, None, MP, LANE), lambda b, g, i: (b, g, 0, 0)),
                  pl.BlockSpec((None, None, LANE, MP), lambda b, g, i: (b, g, 0, 0)),
                  pl.BlockSpec((None, CMP_NEAR, NSA_HPG * TQ), lambda b, g, i: (g, 0, 0)),
                  pl.BlockSpec((LANE, MP), lambda b, g, i: (0, 0))],
        out_specs=[pl.BlockSpec((None, None, None, NSA_HEAD_DIM, NSA_HPG * TQ), lambda b, g, i: (b, g, i, 0, 0)),
                   pl.BlockSpec((None, None, None, LANE, TQ), lambda b, g, i: (b, g, i, 0, 0)),
                   qtile],
        out_shape=[jax.ShapeDtypeStruct((B, G, nQ, NSA_HEAD_DIM, NSA_HPG * TQ), F32),
                   jax.ShapeDtypeStruct((B, G, nQ, LANE, TQ), BF16),
                   jax.ShapeDtypeStruct((B, G, nQ, LANE, NSA_HPG * TQ), BF16)],
        scratch_shapes=[pltpu.VMEM((MP, NSA_HPG * TQ), F32)],
        compiler_params=_cparams(("arbitrary",) * 3),
        name="nsa_compressed",
    )(qn, kc_aug, vct, ct, ovl)


def _softmax_step(s, vt, m_old, acc):
    m_new = jnp.maximum(m_old, jnp.max(s, axis=0, keepdims=True))
    alpha = jnp.exp(m_old - m_new)
    p = jnp.exp(s - m_new).astype(BF16)
    return m_new, acc * alpha + jnp.dot(vt, p, preferred_element_type=F32)


AK = 256


def _pipeline(lo, hi, produce, consume, buf_a, buf_b):
    n = hi - lo

    @pl.when(n > 0)
    def _():
        produce(lo, buf_a)

        def body(p, c):
            j = lo + 2 * p
            produce(j + 1, buf_b)
            consume(j, buf_a)
            produce(j + 2, buf_a)
            consume(j + 1, buf_b)
            return c

        lax.fori_loop(0, (n - 1) // 2, body, 0)

        @pl.when(n % 2 == 1)
        def _():
            consume(hi - 1, buf_a)

        @pl.when(n % 2 == 0)
        def _():
            produce(hi - 1, buf_b)
            consume(hi - 2, buf_a)
            consume(hi - 1, buf_b)


def _attend(lo, hi, lhs_fn, rhs_ref, value_fn, m_ref, acc_ref, sa_ref, sb_ref, near_fn=None):
    def scores(j, buf):
        s = jnp.dot(lhs_fn(j), rhs_ref[...], preferred_element_type=F32)
        buf[...] = s if near_fn is None else near_fn(s, j)

    def consume(j, buf):
        m, acc = _softmax_step(buf[...], value_fn(j), m_ref[...], acc_ref[...])
        m_ref[...] = m
        acc_ref[...] = acc

    _pipeline(lo, hi, scores, consume, sa_ref, sb_ref)


def _near_add(dt_ref, i, s, j):
    parts = []
    for sub in range(AK // TQ):
        idx = jnp.clip(i - (j * (AK // TQ) + sub) + 1, 0, 4)
        parts.append(s[sub * TQ:(sub + 1) * TQ] + dt_ref[idx])
    return jnp.concatenate(parts, axis=0)


def _nsa_sel_body(qt_ref, sel_ref, ks_ref, vst_ref, dt_ref, o_ref, qa_ref, m_ref, acc_ref,
                  sa_ref, sb_ref):
    i = pl.program_id(2)
    L = NSA_HPG * TQ
    qa_ref[0:LANE, :] = qt_ref[...]
    selneg = sel_ref[...]
    qa_ref[LANE:2 * LANE, :] = jnp.concatenate([selneg] * NSA_HPG, axis=1)
    m_ref[...] = jnp.full((1, L), M_FLOOR, F32)
    acc_ref[...] = jnp.zeros(acc_ref.shape, F32)
    last = i // (AK // TQ)
    n_far = jnp.maximum(last - 1, 0)

    def keys(j):
        return ks_ref[pl.ds(pl.multiple_of(j * AK, AK), AK), :]

    def value(j):
        return vst_ref[:, pl.ds(pl.multiple_of(j * AK, AK), AK)]

    _attend(0, n_far, keys, qa_ref, value, m_ref, acc_ref, sa_ref, sb_ref)
    _attend(n_far, last + 1, keys, qa_ref, value, m_ref, acc_ref, sa_ref, sb_ref,
            near_fn=functools.partial(_near_add, dt_ref, i))
    acc = acc_ref[...]
    o_ref[...] = acc[:NSA_HEAD_DIM] / acc[NSA_HEAD_DIM:NSA_HEAD_DIM + 1]


def _nsa_selected(qt, selneg, ks_aug, vst, dt):
    B, G, nQ = qt.shape[:3]
    T = ks_aug.shape[2]
    L = NSA_HPG * TQ
    return pl.pallas_call(
        _nsa_sel_body,
        grid=(B, G, nQ),
        in_specs=[pl.BlockSpec((None, None, None, LANE, L), lambda b, g, i: (b, g, i, 0, 0)),
                  pl.BlockSpec((None, None, None, LANE, TQ), lambda b, g, i: (b, g, i, 0, 0)),
                  pl.BlockSpec((None, None, T, 2 * LANE), lambda b, g, i: (b, g, 0, 0)),
                  pl.BlockSpec((None, None, LANE, T), lambda b, g, i: (b, g, 0, 0)),
                  pl.BlockSpec((None, 5, TQ, L), lambda b, g, i: (g, 0, 0, 0))],
        out_specs=pl.BlockSpec((None, None, None, NSA_HEAD_DIM, L), lambda b, g, i: (b, g, i, 0, 0)),
        out_shape=jax.ShapeDtypeStruct((B, G, nQ, NSA_HEAD_DIM, L), F32),
        scratch_shapes=[pltpu.VMEM((2 * LANE, L), BF16),
                        pltpu.VMEM((1, L), F32),
                        pltpu.VMEM((LANE, L), F32),
                        pltpu.VMEM((AK, L), F32),
                        pltpu.VMEM((AK, L), F32)],
        compiler_params=_cparams(("arbitrary",) * 3),
        name="nsa_selected",
    )(qt, selneg, ks_aug, vst, dt)


def _nsa_win_body(qt_ref, kw_ref, vwt_ref, w0_ref, dt_ref, oc_ref, os_ref, gl_ref, o_ref):
    i = pl.program_id(2)
    span = WINDOW + TQ
    k0 = pl.multiple_of(i * TQ, TQ)
    s = jnp.dot(kw_ref[pl.ds(k0, span), :], qt_ref[...], preferred_element_type=F32)
    s = jnp.concatenate([s[0:TQ] + w0_ref[...], s[TQ:WINDOW - TQ],
                         s[WINDOW - TQ:WINDOW] + dt_ref[2], s[WINDOW:span] + dt_ref[1]], axis=0)
    m = jnp.maximum(jnp.max(s, axis=0, keepdims=True), M_FLOOR)
    p = jnp.exp(s - m).astype(BF16)
    acc = jnp.dot(vwt_ref[:, pl.ds(k0, span)], p, preferred_element_type=F32)
    ow = acc[:NSA_HEAD_DIM] / acc[NSA_HEAD_DIM:NSA_HEAD_DIM + 1]
    g = jax.nn.sigmoid(gl_ref[...])
    o_t = g[0:1] * oc_ref[...] + g[1:2] * os_ref[...] + g[2:3] * ow
    o_ref[...] = _lanes_to_heads(o_t, NSA_HPG).astype(BF16)


def _nsa_window(qt, kw_aug, vwt, w0, dt, oc, osel, gl):
    B, G, nQ = qt.shape[:3]
    TP = kw_aug.shape[2]
    L = NSA_HPG * TQ
    W = NSA_HPG * NSA_HEAD_DIM
    tile = lambda r: pl.BlockSpec((None, None, None, r, L), lambda b, g, i: (b, g, i, 0, 0))
    return pl.pallas_call(
        _nsa_win_body,
        grid=(B, G, nQ),
        in_specs=[tile(LANE),
                  pl.BlockSpec((None, None, TP, LANE), lambda b, g, i: (b, g, 0, 0)),
                  pl.BlockSpec((None, None, LANE, TP), lambda b, g, i: (b, g, 0, 0)),
                  pl.BlockSpec((TQ, L), lambda b, g, i: (0, 0)),
                  pl.BlockSpec((None, 5, TQ, L), lambda b, g, i: (g, 0, 0, 0)),
                  tile(NSA_HEAD_DIM), tile(NSA_HEAD_DIM), tile(8)],
        out_specs=pl.BlockSpec((None, TQ, W), lambda b, g, i: (b, i, g)),
        out_shape=jax.ShapeDtypeStruct((B, nQ * TQ, G * W), BF16),
        compiler_params=_cparams(("arbitrary",) * 3),
        name="nsa_window",
    )(qt, kw_aug, vwt, w0, dt, oc, osel, gl)


INT_MIN = -2 ** 31
CK = 2 * NK
KEY_BITS = 16


def _dsa_body(qd_ref, qi_ref, wt_ref, ki_ref, ckv_ref, ckvt_ref, wuk_ref, wuvt_ref, dt_ref,
              o_ref, sc_ref, qb_ref, m_ref, acc_ref, sa_ref, sb_ref, *, k_top):
    i = pl.program_id(1)
    H = DSA_HEADS
    L = H * TQ
    last = i // (NK // TQ)
    n_steps = last + 1
    t_row = i * TQ + lax.broadcasted_iota(I32, (1, TQ), 1)

    qit = jnp.concatenate([_heads_to_lanes(qi_ref[...].astype(F32), IDX_HEADS),
                           jnp.zeros((LANE - IDX_DIM, L), F32)], axis=0).astype(BF16)
    wt = wt_ref[...]

    def idx_scores(j, buf):
        k0 = pl.multiple_of(j * NK, NK)
        buf[...] = jnp.dot(ki_ref[pl.ds(k0, NK), :], qit, preferred_element_type=F32)

    def idx_reduce(j, buf):
        k0 = pl.multiple_of(j * NK, NK)
        r = jnp.maximum(buf[...], 0.0) * wt
        sc = r[:, 0:TQ]
        for h in range(1, H):
            sc = sc + r[:, h * TQ:(h + 1) * TQ]
        bits = pltpu.bitcast(sc, I32)
        ik = jnp.where(sc == 0.0, 0, bits ^ ((bits >> 31) & 0x7FFFFFFF))
        key = k0 + lax.broadcasted_iota(I32, (NK, 1), 0)
        sc_ref[pl.ds(k0, NK), :] = jnp.where(key <= t_row, jnp.maximum(ik, INT_MIN + 1), INT_MIN)

    _pipeline(0, n_steps, idx_scores, idx_reduce, sa_ref, sb_ref)

    @pl.when(n_steps % 2 == 1)
    def _():
        sc_ref[pl.ds(pl.multiple_of(n_steps * NK, NK), NK), :] = jnp.full((NK, TQ), INT_MIN, I32)

    n_chunks = (n_steps + 1) // 2

    def count(pred):
        def cbody(j, cnt):
            k0 = pl.multiple_of(j * CK, CK)
            ind = jnp.where(pred(sc_ref[pl.ds(k0, CK), :], k0), 1.0, 0.0)
            parts = [ind[r * 8:(r + 1) * 8] for r in range(CK // 8)]
            while len(parts) > 1:
                parts = [a + b for a, b in zip(parts[0::2], parts[1::2])]
            return cnt + parts[0]
        cnt8 = lax.fori_loop(0, n_chunks, cbody, jnp.zeros((8, TQ), F32))
        return jnp.sum(cnt8, axis=0, keepdims=True)

    def any_lane(cond):
        return jnp.max(jnp.where(cond, 1.0, 0.0)) > 0.0

    kf = float(k_top)
    need = t_row + 1 > k_top

    def bit_body(b, c):
        tu, ct = c
        cand = tu | jnp.left_shift(jnp.int32(1), 31 - b)
        thr_c = cand ^ INT_MIN
        cnt = count(lambda blk, k0: blk >= thr_c)
        take = cnt >= kf
        return jnp.where(take, cand, tu), jnp.where(take, cnt, ct)

    tu, ct = lax.fori_loop(0, 32, bit_body, (jnp.zeros((1, TQ), I32), (t_row + 1).astype(F32)))
    thr = jnp.maximum(tu ^ INT_MIN, INT_MIN + 1)

    @pl.when(any_lane(need & (ct > kf)))
    def _():
        keep = kf - count(lambda blk, k0: blk > thr)

        def tie_count(bound):
            def pred(blk, k0):
                key = k0 + lax.broadcasted_iota(I32, (CK, 1), 0)
                return jnp.where(key < bound, blk, INT_MIN) == thr
            return count(pred)

        def pos_body(b, pos):
            cand = pos | jnp.left_shift(jnp.int32(1), KEY_BITS - 1 - b)
            return jnp.where(tie_count(cand) < keep, cand, pos)

        pos = lax.fori_loop(0, KEY_BITS, pos_body, jnp.zeros((1, TQ), I32))

        def fix(j, c):
            k0 = pl.multiple_of(j * CK, CK)
            blk = sc_ref[pl.ds(k0, CK), :]
            key = k0 + lax.broadcasted_iota(I32, (CK, 1), 0)
            drop = jnp.where(key > pos, blk, INT_MIN) == thr
            sc_ref[pl.ds(k0, CK), :] = jnp.where(drop, INT_MIN, blk)
            return c

        lax.fori_loop(0, n_chunks, fix, 0)

    qdt = _heads_to_lanes(qd_ref[...].astype(F32), H).astype(BF16)
    for h in range(H):
        ql = jnp.dot(wuk_ref[h], qdt[:, h * TQ:(h + 1) * TQ], preferred_element_type=F32)
        qb_ref[0:LANE, h * TQ:(h + 1) * TQ] = (ql * DSA_HEAD_DIM ** -0.5).astype(BF16)
    eye = (lax.broadcasted_iota(I32, (LANE, TQ), 0) == lax.broadcasted_iota(I32, (LANE, TQ), 1))
    eye = jnp.where(eye, 1.0, 0.0).astype(BF16)
    qb_ref[LANE:2 * LANE, :] = jnp.concatenate([eye] * H, axis=1)
    m_ref[...] = jnp.full((1, L), M_FLOOR, F32)
    acc_ref[...] = jnp.zeros(acc_ref.shape, F32)
    alast = i // (AK // TQ)
    n_far = jnp.maximum(alast - 1, 0)

    def keys(j):
        k0 = pl.multiple_of(j * AK, AK)
        nb = jnp.where(sc_ref[pl.ds(k0, AK), :] >= thr, 0.0, NEG_INF).astype(BF16)
        return jnp.concatenate([ckv_ref[pl.ds(k0, AK), :], nb], axis=1)

    def value(j):
        return ckvt_ref[:, pl.ds(pl.multiple_of(j * AK, AK), AK)]

    _attend(0, n_far, keys, qb_ref, value, m_ref, acc_ref, sa_ref, sb_ref)
    _attend(n_far, alast + 1, keys, qb_ref, value, m_ref, acc_ref, sa_ref, sb_ref,
            near_fn=functools.partial(_near_add, dt_ref, i))
    acc = acc_ref[...]
    olat = (acc[:DSA_KV_LATENT] / acc[DSA_KV_LATENT:DSA_KV_LATENT + 1]).astype(BF16)
    o_t = jnp.concatenate([jnp.dot(wuvt_ref[h], olat[:, h * TQ:(h + 1) * TQ],
                                   preferred_element_type=F32) for h in range(H)], axis=1)
    o_ref[...] = _lanes_to_heads(o_t, H).astype(BF16)


def _dsa(qd, qi, wt, ki, ckv, ckvt, wuk, wuvt, dt, k_top):
    B, T = ki.shape[:2]
    nQ = T // TQ
    H = DSA_HEADS
    L = H * TQ
    W = H * DSA_HEAD_DIM
    R = ckvt.shape[1]
    rows = pl.BlockSpec((None, TQ, W), lambda b, i: (b, i, 0))
    return pl.pallas_call(
        functools.partial(_dsa_body, k_top=k_top),
        grid=(B, nQ),
        in_specs=[rows, rows, pl.BlockSpec((None, None, 1, L), lambda b, i: (b, i, 0, 0)),
                  pl.BlockSpec((None, T, LANE), lambda b, i: (b, 0, 0)),
                  pl.BlockSpec((None, T, DSA_KV_LATENT), lambda b, i: (b, 0, 0)),
                  pl.BlockSpec((None, R, T), lambda b, i: (b, 0, 0)),
                  pl.BlockSpec((H, DSA_KV_LATENT, DSA_HEAD_DIM), lambda b, i: (0, 0, 0)),
                  pl.BlockSpec((H, DSA_HEAD_DIM, DSA_KV_LATENT), lambda b, i: (0, 0, 0)),
                  pl.BlockSpec((5, TQ, L), lambda b, i: (0, 0, 0))],
        out_specs=rows,
        out_shape=jax.ShapeDtypeStruct((B, T, W), BF16),
        scratch_shapes=[pltpu.VMEM((T, TQ), I32),
                        pltpu.VMEM((2 * LANE, L), BF16),
                        pltpu.VMEM((1, L), F32),
                        pltpu.VMEM((R, L), F32),
                        pltpu.VMEM((AK, L), F32),
                        pltpu.VMEM((AK, L), F32)],
        compiler_params=_cparams(("arbitrary",) * 2),
        name="dsa_attention",
    )(qd, qi, wt, ki, ckv, ckvt, wuk, wuvt, dt)


def _token_mixers(qn, kvn, qd, ckv, qi, small, ki, rel_bias, cmp_pe, cmp_w1, cmp_b1, cmp_w2,
                  dsa_w_uk, dsa_w_uv):
    B, T = qn.shape[:2]
    G, HPG, DH = NSA_KV_GROUPS, NSA_HPG, NSA_HEAD_DIM
    nQ = T // TQ
    n_blk = T // SLC_BLOCK
    L = HPG * TQ
    assert T % NK == 0 and n_blk <= LANE
    kv6 = kvn.reshape(B, T, 6, G, DH)

    tab_nsa = rel_bias[:, :NSA_HEADS]
    dt_nsa = _near_tables(tab_nsa).reshape(5, TQ, G, L).transpose(2, 0, 1, 3)
    dt_dsa = _near_tables(rel_bias[:, NSA_HEADS:])

    kvc = _compress(kv6, cmp_pe, cmp_w1, cmp_b1, cmp_w2)
    M = T // CMP_STRIDE
    MP = M + CMP_PAD
    kc = kvc[:, 0].astype(BF16)
    flag = jnp.concatenate([jnp.full((CMP_PAD, 1), NEG_INF, F32), jnp.zeros((M, 1), F32)]).astype(BF16)
    kc_aug = jnp.concatenate(
        [jnp.pad(kc, ((0, 0), (0, 0), (CMP_PAD, 0), (0, 0))),
         jnp.broadcast_to(flag, (B, G, MP, 1)),
         jnp.zeros((B, G, MP, LANE - DH - 1), BF16)], axis=-1)
    vct = jnp.pad(kvc[:, 1].astype(BF16).transpose(0, 1, 3, 2),
                  ((0, 0), (0, 0), (0, LANE - DH), (CMP_PAD, 0)))
    mrow = jnp.arange(CMP_NEAR)[:, None]
    qcol = jnp.arange(TQ)[None, :]
    ct = _delta(tab_nsa, qcol - CMP_STRIDE * mrow + (2 * TQ - CMP_BLOCK + 1))
    ct = ct.reshape(CMP_NEAR, TQ, G, HPG).transpose(2, 0, 3, 1).reshape(G, CMP_NEAR, L)
    n_cmp = (T - CMP_BLOCK) // CMP_STRIDE + 1
    cs = jnp.arange(M) * CMP_STRIDE
    ss = jnp.arange(LANE) * SLC_BLOCK
    ovl = (jnp.clip(jnp.minimum(cs[None, :] + CMP_BLOCK, ss[:, None] + SLC_BLOCK)
                    - jnp.maximum(cs[None, :], ss[:, None]), 0, None).astype(F32) / CMP_BLOCK)
    ovl = jnp.where((jnp.arange(M)[None, :] < n_cmp) & (jnp.arange(LANE)[:, None] < n_blk), ovl, 0.0)
    ovl = jnp.pad(ovl, ((0, 0), (CMP_PAD, 0))).astype(BF16)
    oc, selneg, qt = _nsa_compressed(qn, kc_aug, vct, ct, ovl, n_blk)

    ks = kv6[:, :, 2].transpose(0, 2, 1, 3)
    et = (jnp.arange(T)[:, None] // SLC_BLOCK == jnp.arange(LANE)[None, :]).astype(BF16)
    ks_aug = jnp.concatenate([ks, jnp.zeros((B, G, T, LANE - DH), BF16),
                              jnp.broadcast_to(et, (B, G, T, LANE))], axis=-1)
    vs = kv6[:, :, 3].transpose(0, 2, 3, 1)
    vst = jnp.concatenate([vs, jnp.ones((B, G, 1, T), BF16),
                           jnp.zeros((B, G, LANE - DH - 1, T), BF16)], axis=2)
    osel = _nsa_selected(qt, selneg, ks_aug, vst, dt_nsa)

    TP = T + WINDOW
    kw = jnp.pad(kv6[:, :, 4].transpose(0, 2, 1, 3), ((0, 0), (0, 0), (WINDOW, 0), (0, 0)))
    wflag = jnp.concatenate([jnp.full((WINDOW, 1), NEG_INF, F32), jnp.zeros((T, 1), F32)]).astype(BF16)
    kw_aug = jnp.concatenate([kw, jnp.broadcast_to(wflag, (B, G, TP, 1)),
                              jnp.zeros((B, G, TP, LANE - DH - 1), BF16)], axis=-1)
    vw = jnp.pad(kv6[:, :, 5].transpose(0, 2, 3, 1), ((0, 0), (0, 0), (0, 0), (WINDOW, 0)))
    vwt = jnp.concatenate([vw, jnp.ones((B, G, 1, TP), BF16),
                           jnp.zeros((B, G, LANE - DH - 1, TP), BF16)], axis=2)
    u = jnp.arange(TQ)[:, None]
    w0 = jnp.tile(jnp.where(u > jnp.arange(TQ)[None, :], 0.0, NEG_INF).astype(F32), (1, HPG))
    gl = small[..., 0:3 * NSA_HEADS].reshape(B, nQ, TQ, G, HPG, 3).transpose(0, 3, 1, 5, 4, 2)
    gl = jnp.pad(gl.reshape(B, G, nQ, 3, L), ((0, 0), (0, 0), (0, 0), (0, 5), (0, 0)))
    o_nsa = _nsa_window(qt, kw_aug, vwt, w0, dt_nsa, oc, osel, gl)

    H = DSA_HEADS
    w_idx = small[..., 24 + IDX_DIM:24 + IDX_DIM + IDX_HEADS] * IDX_HEADS ** -0.5
    wt = w_idx.reshape(B, nQ, TQ, IDX_HEADS).transpose(0, 1, 3, 2).reshape(B, nQ, 1, IDX_HEADS * TQ)
    ckvt = jnp.concatenate([ckv.transpose(0, 2, 1), jnp.ones((B, 1, T), BF16),
                            jnp.zeros((B, 7, T), BF16)], axis=1)
    wuk = dsa_w_uk.transpose(1, 0, 2).astype(BF16)
    wuvt = dsa_w_uv.transpose(1, 2, 0).astype(BF16)
    o_dsa = _dsa(qd, qi, wt, ki, ckv, ckvt, wuk, wuvt, dt_dsa, min(IDX_TOPK_MAX, T // 4))
    return o_nsa, o_dsa


def _rms(v, g):
    return v * lax.rsqrt(jnp.mean(v * v, axis=-1, keepdims=True) + RMS_EPS) * g


HI_MASK = -65536


def _pack_halves(v):
    w = v.shape[1] // 2
    lo = lax.shift_right_logical(pltpu.bitcast(v[:, :w], I32), 16)
    hi = pltpu.bitcast(v[:, w:], I32) & HI_MASK
    return lo | hi


def _unpack_halves(p):
    return pltpu.bitcast(p << 16, F32), pltpu.bitcast(p & HI_MASK, F32)


def _merge_body(on_ref, od_ref, gm_ref, x_ref, wn_ref, wd_ref, wo_ref, gpost_ref, gt_ref,
                gpre_ref, sc_ref, sh_ref, rwt_ref, x1_ref, h2_ref, hpa_ref, hpb_ref, lt_ref):
    D = D_MODEL
    ya = jnp.dot(on_ref[...], wn_ref[...], preferred_element_type=F32)
    yb = jnp.dot(od_ref[...], wd_ref[...], preferred_element_type=F32)
    y = jax.nn.sigmoid(gm_ref[:, 0:D]) * ya + jax.nn.sigmoid(gm_ref[:, D:2 * D]) * yb
    y2 = jnp.dot(y.astype(BF16), wo_ref[...], preferred_element_type=F32)
    x1 = x_ref[...] + gt_ref[...] * _rms(y2, gpost_ref[...])
    x1_ref[...] = x1
    h2 = (_rms(x1, gpre_ref[...]) * (1.0 + sc_ref[...]) + sh_ref[...]).astype(BF16)
    h2_ref[...] = h2
    packed = _pack_halves(h2.astype(F32))
    hpa_ref[...] = packed[:, :SC_WIDTH]
    hpb_ref[...] = packed[:, SC_WIDTH:]
    lt_ref[...] = lax.dot_general(rwt_ref[...], h2, (((1,), (1,)), ((), ())),
                                  preferred_element_type=F32)


def _merge(o_nsa, o_dsa, gm, x, wn, wd, wo, g_post, gt1, g_pre, sc2, sh2, router_w):
    B, T, D = x.shape
    TM = 256
    E = router_w.shape[1]
    row = lambda w: pl.BlockSpec((None, TM, w), lambda b, i: (b, i, 0))
    full = lambda a: pl.BlockSpec(a.shape, lambda b, i: (0,) * a.ndim)
    vecb = pl.BlockSpec((None, 1, D), lambda b, i: (b, 0, 0))
    vec = pl.BlockSpec((1, D), lambda b, i: (0, 0))
    wn, wd, wo = wn.astype(BF16), wd.astype(BF16), wo.astype(BF16)
    rwt = router_w.T.astype(BF16)
    return pl.pallas_call(
        _merge_body,
        grid=(B, T // TM),
        in_specs=[row(o_nsa.shape[-1]), row(o_dsa.shape[-1]), row(2 * D), row(D),
                  full(wn), full(wd), full(wo), vec, vecb, vec, vecb, vecb, full(rwt)],
        out_specs=[row(D), row(D), row(SC_WIDTH), row(SC_WIDTH),
                   pl.BlockSpec((None, E, TM), lambda b, i: (b, 0, i))],
        out_shape=[jax.ShapeDtypeStruct((B, T, D), F32),
                   jax.ShapeDtypeStruct((B, T, D), BF16),
                   jax.ShapeDtypeStruct((B, T, SC_WIDTH), I32),
                   jax.ShapeDtypeStruct((B, T, SC_WIDTH), I32),
                   jax.ShapeDtypeStruct((B, E, T), F32)],
        compiler_params=_cparams(("arbitrary",) * 2),
        name="merge_postnorm",
    )(o_nsa, o_dsa, gm, x, wn, wd, wo, g_post.reshape(1, D), gt1.reshape(B, 1, D),
      g_pre.reshape(1, D), sc2.reshape(B, 1, D), sh2.reshape(B, 1, D), rwt)


TR = 256
GSZ = N_EXPERTS // N_EXPERT_GROUPS


def _first_max(v, idx, big):
    mx = jnp.max(v, axis=0, keepdims=True)
    return jnp.min(jnp.where(v == mx, idx, big), axis=0, keepdims=True), mx


def _route_body(lt_ref, bias_ref, eid_ref, rk_ref, wt_ref, cnt_ref, carry_ref):
    first_step = (pl.program_id(0) == 0) & (pl.program_id(1) == 0)

    @pl.when(first_step)
    def _():
        carry_ref[...] = jnp.zeros(carry_ref.shape, F32)

    s = jax.nn.sigmoid(lt_ref[...])
    sel = s + bias_ref[:, 0:1]
    grow = lax.broadcasted_iota(I32, (GSZ, TR), 0)
    gs = []
    for g in range(N_EXPERT_GROUPS):
        blk = sel[g * GSZ:(g + 1) * GSZ]
        f1, m1 = _first_max(blk, grow, GSZ)
        m2 = jnp.max(jnp.where(grow == f1, KNOCK, blk), axis=0, keepdims=True)
        gs.append(m1 + m2)
    gsc = jnp.concatenate(gs, axis=0)
    gidx = lax.broadcasted_iota(I32, (N_EXPERT_GROUPS, TR), 0)
    gsel = jnp.zeros((N_EXPERT_GROUPS, TR), F32)
    for _ in range(TOPK_GROUPS):
        f, _mx = _first_max(gsc, gidx, N_EXPERT_GROUPS)
        pick = gidx == f
        gsel = jnp.where(pick, 1.0, gsel)
        gsc = jnp.where(pick, KNOCK, gsc)
    cur = jnp.concatenate(
        [jnp.where(gsel[g:g + 1] > 0.0, sel[g * GSZ:(g + 1) * GSZ], NEG_INF)
         for g in range(N_EXPERT_GROUPS)], axis=0)
    erow = lax.broadcasted_iota(I32, (N_EXPERTS, TR), 0)
    member = jnp.zeros((N_EXPERTS, TR), F32)
    picks, eids, ws = [], [], []
    for _ in range(TOP_K):
        f, _mx = _first_max(cur, erow, N_EXPERTS)
        pick = erow == f
        picks.append(pick)
        eids.append(f)
        ws.append(jnp.sum(jnp.where(pick, s, 0.0), axis=0, keepdims=True))
        member = jnp.where(pick, 1.0, member)
        cur = jnp.where(pick, KNOCK, cur)
    wsum = ws[0]
    for k in range(1, TOP_K):
        wsum = wsum + ws[k]
    mb = member.astype(BF16)
    upper = (lax.broadcasted_iota(I32, (TR, TR), 0) < lax.broadcasted_iota(I32, (TR, TR), 1))
    upper = jnp.where(upper, 1.0, 0.0).astype(BF16)
    rank = jnp.dot(mb, upper, preferred_element_type=F32) + carry_ref[:, 0:1]
    eid_ref[...] = jnp.concatenate(eids, axis=0)
    rk_ref[...] = jnp.concatenate(
        [jnp.sum(jnp.where(p, rank, 0.0), axis=0, keepdims=True) for p in picks], axis=0)
    wt_ref[...] = jnp.concatenate([w / wsum * ROUTED_SCALE for w in ws], axis=0)
    carry_ref[...] = carry_ref[...] + jnp.dot(mb, jnp.ones((TR, LANE), BF16),
                                              preferred_element_type=F32)
    cnt_ref[...] = carry_ref[...]


def _route(lt, router_bias):
    B, E, T = lt.shape
    out8 = lambda: pl.BlockSpec((None, TOP_K, TR), lambda b, i: (b, 0, i))
    return pl.pallas_call(
        _route_body,
        grid=(B, T // TR),
        in_specs=[pl.BlockSpec((None, E, TR), lambda b, i: (b, 0, i)),
                  pl.BlockSpec((E, 1), lambda b, i: (0, 0))],
        out_specs=[out8(), out8(), out8(), pl.BlockSpec((E, LANE), lambda b, i: (0, 0))],
        out_shape=[jax.ShapeDtypeStruct((B, TOP_K, T), I32),
                   jax.ShapeDtypeStruct((B, TOP_K, T), F32),
                   jax.ShapeDtypeStruct((B, TOP_K, T), F32),
                   jax.ShapeDtypeStruct((E, LANE), F32)],
        scratch_shapes=[pltpu.VMEM((E, LANE), F32)],
        compiler_params=_cparams(("arbitrary",) * 2),
        name="moe_router",
    )(lt, router_bias.reshape(E, 1))


BM = 256


def _gmm_body(be_ref, nv_ref, ea_ref, eb_ref, xa_ref, xb_ref, wga_ref, wua_ref, wda_ref,
              wgb_ref, wub_ref, wdb_ref, ya_ref, yb_ref, wgb, wub, wdb):
    b = pl.program_id(0)
    valid = nv_ref[b]

    @pl.when(valid > 0)
    def _():
        prev = be_ref[jnp.maximum(b - 1, 0)]
        changed = (b == 0) | (be_ref[b] != prev)

        @pl.when(changed & (ea_ref[b] == be_ref[b]))
        def _():
            wgb[...] = wga_ref[...].astype(BF16)
            wub[...] = wua_ref[...].astype(BF16)
            wdb[...] = wda_ref[...].astype(BF16)

        @pl.when(changed & (ea_ref[b] != be_ref[b]))
        def _():
            wgb[...] = wgb_ref[...].astype(BF16)
            wub[...] = wub_ref[...].astype(BF16)
            wdb[...] = wdb_ref[...].astype(BF16)

        live = lax.broadcasted_iota(I32, (BM, 1), 0) < valid
        lo_a, hi_a = _unpack_halves(jnp.where(live, xa_ref[...], 0))
        lo_b, hi_b = _unpack_halves(jnp.where(live, xb_ref[...], 0))
        xb = jnp.concatenate([lo_a, lo_b, hi_a, hi_b], axis=1).astype(BF16)
        gt = jnp.dot(xb, wgb[...], preferred_element_type=F32)
        up = jnp.dot(xb, wub[...], preferred_element_type=F32)
        a = (gt * jax.nn.sigmoid(gt) * up).astype(BF16)
        y = jnp.dot(a, wdb[...], preferred_element_type=F32)
        packed = _pack_halves(y.astype(BF16).astype(F32))
        ya_ref[...] = packed[:, :SC_WIDTH]
        yb_ref[...] = packed[:, SC_WIDTH:]

    @pl.when(valid == 0)
    def _():
        ya_ref[...] = jnp.zeros(ya_ref.shape, I32)
        yb_ref[...] = jnp.zeros(yb_ref.shape, I32)


def _grouped_mlp(xs_a, xs_b, blk_e, n_valid, exp_a, exp_b, w_g, w_u, w_d):
    P = xs_a.shape[0]
    D, De = w_g.shape[-2:]
    nb = P // BM
    rows = pl.BlockSpec((BM, SC_WIDTH), lambda b, be, nv, ea, eb: (b, 0))
    sel_a = lambda b, be, nv, ea, eb: (ea[b], 0, 0)
    sel_b = lambda b, be, nv, ea, eb: (eb[b], 0, 0)
    grid_spec = pltpu.PrefetchScalarGridSpec(
        num_scalar_prefetch=4,
        grid=(nb,),
        in_specs=[rows, rows,
                  pl.BlockSpec((None, D, De), sel_a),
                  pl.BlockSpec((None, D, De), sel_a),
                  pl.BlockSpec((None, De, D), sel_a),
                  pl.BlockSpec((None, D, De), sel_b),
                  pl.BlockSpec((None, D, De), sel_b),
                  pl.BlockSpec((None, De, D), sel_b)],
        out_specs=[rows, rows],
        scratch_shapes=[pltpu.VMEM((D, De), BF16), pltpu.VMEM((D, De), BF16),
                        pltpu.VMEM((De, D), BF16)])
    out = jax.ShapeDtypeStruct((P, SC_WIDTH), I32)
    return pl.pallas_call(
        _gmm_body,
        grid_spec=grid_spec,
        out_shape=[out, out],
        compiler_params=_cparams(("arbitrary",)),
        name="moe_experts",
    )(blk_e, n_valid, exp_a, exp_b, xs_a, xs_b, w_g, w_u, w_d, w_g, w_u, w_d)


def _sc_mesh():
    return plsc.VectorSubcoreMesh(core_axis_name="c", subcore_axis_name="s")


def _sc_gather(x, idx):
    n = idx.shape[0]

    @pl.kernel(out_type=jax.ShapeDtypeStruct((n, SC_WIDTH), x.dtype), mesh=_sc_mesh())
    def gather(x_hbm, i_hbm, o_hbm):
        def body(i_vmem, o_vmem):
            pltpu.sync_copy(x_hbm.at[i_vmem.at[0]], o_vmem)

        pltpu.emit_pipeline(
            body, grid=(n // SC_WINDOW,),
            in_specs=[pl.BlockSpec((1, SC_WINDOW), lambda i: (0, i))],
            out_specs=[pl.BlockSpec((SC_WINDOW, SC_WIDTH), lambda i: (i, 0))],
            core_axis_name=("c", "s"),
            dimension_semantics=(pltpu.PARALLEL,),
        )(i_hbm, o_hbm)

    return gather(x, idx.reshape(1, n))


def _sc_dispatch(x, slot, n_out):
    B, T, W = x.shape
    K = slot.shape[1]
    tb = T // SC_WINDOW
    n = B * K * T

    @pl.kernel(out_type=jax.ShapeDtypeStruct((n_out, W), x.dtype), mesh=_sc_mesh(), scratch_types=[])
    def scatter(x_hbm, i_hbm, o_hbm):
        def body(x_vmem, i_vmem):
            pltpu.sync_copy(x_vmem, o_hbm.at[i_vmem.at[0]])

        pltpu.emit_pipeline(
            body, grid=(n // SC_WINDOW,),
            in_specs=[pl.BlockSpec((SC_WINDOW, W), lambda i: ((i // (K * tb)) * tb + i % tb, 0)),
                      pl.BlockSpec((1, SC_WINDOW), lambda i: (0, i))],
            out_specs=[],
            core_axis_name=("c", "s"),
            dimension_semantics=(pltpu.PARALLEL,),
        )(x_hbm, i_hbm)

    return scatter(x.reshape(B * T, W), slot.reshape(1, n))


def _final_body(ya_ref, yb_ref, w_ref, h_ref, x1_ref, wg_ref, wu_ref, wd_ref, g_ref, gt_ref, o_ref):
    hb = h_ref[...]
    gt = jnp.dot(hb, wg_ref[...], preferred_element_type=F32)
    up = jnp.dot(hb, wu_ref[...], preferred_element_type=F32)
    a = (gt * jax.nn.sigmoid(gt) * up).astype(BF16)
    shared = jnp.dot(a, wd_ref[...], preferred_element_type=F32)
    w = w_ref[...]
    parts = None
    for k in range(TOP_K):
        wk = w[:, k:k + 1]
        lo_a, hi_a = _unpack_halves(ya_ref[k])
        lo_b, hi_b = _unpack_halves(yb_ref[k])
        terms = [lo_a * wk, lo_b * wk, hi_a * wk, hi_b * wk]
        parts = terms if parts is None else [p + t for p, t in zip(parts, terms)]
    y = jnp.concatenate(parts, axis=1) + shared
    o_ref[...] = x1_ref[...] + gt_ref[...] * _rms(y, g_ref[...])


def _final(yg_a, yg_b, w, h2, x1, ws_g, ws_u, ws_d, g_post, gt2):
    B, T, D = x1.shape
    TM = 256 if T % 256 == 0 else TQ
    row = lambda wd: pl.BlockSpec((None, TM, wd), lambda b, i: (b, i, 0))
    krows = pl.BlockSpec((None, TOP_K, TM, SC_WIDTH), lambda b, i: (b, 0, i, 0))
    full = lambda a: pl.BlockSpec(a.shape, lambda b, i: (0,) * a.ndim)
    ws_g, ws_u, ws_d = ws_g.astype(BF16), ws_u.astype(BF16), ws_d.astype(BF16)
    return pl.pallas_call(
        _final_body,
        grid=(B, T // TM),
        in_specs=[krows, krows, row(TOP_K), row(D), row(D),
                  full(ws_g), full(ws_u), full(ws_d),
                  pl.BlockSpec((1, D), lambda b, i: (0, 0)),
                  pl.BlockSpec((None, 1, D), lambda b, i: (b, 0, 0))],
        out_specs=row(D),
        out_shape=jax.ShapeDtypeStruct((B, T, D), F32),
        compiler_params=_cparams(("arbitrary",) * 2),
        name="shared_final",
    )(yg_a, yg_b, w, h2, x1, ws_g, ws_u, ws_d, g_post.reshape(1, D), gt2.reshape(B, 1, D))


def _slot_body(eid_ref, rk_ref, ps_ref, o_ref):
    erow = lax.broadcasted_iota(I32, (N_EXPERTS, TR), 0)
    ps = ps_ref[:, 0:1]
    rows = []
    for k in range(TOP_K):
        base = jnp.sum(jnp.where(erow == eid_ref[k:k + 1, :], ps, 0.0), axis=0, keepdims=True)
        rows.append(base + rk_ref[k:k + 1, :])
    o_ref[...] = jnp.concatenate(rows, axis=0).astype(I32)


def _slots(eid, rk, pstart):
    B, K, T = eid.shape
    blk = pl.BlockSpec((None, K, TR), lambda b, i: (b, 0, i))
    return pl.pallas_call(
        _slot_body,
        grid=(B, T // TR),
        in_specs=[blk, blk, pl.BlockSpec((N_EXPERTS, 1), lambda b, i: (0, 0))],
        out_specs=blk,
        out_shape=jax.ShapeDtypeStruct((B, K, T), I32),
        compiler_params=_cparams(("arbitrary",) * 2),
        name="moe_slots",
    )(eid, rk, pstart.astype(F32).reshape(N_EXPERTS, 1))


def _moe(h2, hp_a, hp_b, lt, x1, router_bias, w_g, w_u, w_d, ws_g, ws_u, ws_d, g_post, gt2):
    B, T, D = x1.shape
    N = B * T
    eid, rk, wts, cnt = _route(lt, router_bias)
    counts = cnt[:, 0].astype(I32)
    padded = (counts + BM - 1) // BM * BM
    pend = jnp.cumsum(padded)
    pstart = pend - padded
    nb = (N * TOP_K + N_EXPERTS * (BM - 1) + BM - 1) // BM
    P = nb * BM
    row0 = jnp.arange(nb, dtype=I32) * BM
    blk_e = jnp.minimum(jnp.sum(pend[None, :] <= row0[:, None], axis=1), N_EXPERTS - 1).astype(I32)
    n_valid = jnp.clip(pstart[blk_e] + counts[blk_e] - row0, 0, BM).astype(I32)
    active = counts > 0
    rank = jnp.cumsum(active.astype(I32)) - 1
    n_act = rank[-1] + 1
    eids = jnp.arange(N_EXPERTS, dtype=I32)
    act_list = jnp.sum(jnp.where((rank[None, :] == eids[:, None]) & active[None, :], eids[None, :], 0),
                       axis=1)
    pos = jnp.where(n_valid > 0, rank[blk_e], n_act - 1)
    exp_a = act_list[jnp.minimum(pos + pos % 2, n_act - 1)].astype(I32)
    exp_b = act_list[jnp.minimum(pos + 1 - pos % 2, n_act - 1)].astype(I32)
    slot = _slots(eid, rk, pstart)
    xs_a = _sc_dispatch(hp_a, slot, P)
    xs_b = _sc_dispatch(hp_b, slot, P)
    ys_a, ys_b = _grouped_mlp(xs_a, xs_b, blk_e, n_valid, exp_a, exp_b, w_g, w_u, w_d)
    flat = slot.reshape(N * TOP_K)
    yg_a = _sc_gather(ys_a, flat).reshape(B, TOP_K, T, SC_WIDTH)
    yg_b = _sc_gather(ys_b, flat).reshape(B, TOP_K, T, SC_WIDTH)
    return _final(yg_a, yg_b, wts.transpose(0, 2, 1), h2, x1, ws_g, ws_u, ws_d, g_post, gt2)


def kernel(x, c, w_ada, b_ada, g_pre_mix, g_post_mix, g_pre_ffn, g_post_ffn, w_in, rel_bias, cmp_pe, cmp_w1, cmp_b1, cmp_w2, dsa_kv_norm, dsa_w_uk, dsa_w_uv, w_branch_nsa, w_branch_dsa, w_out, router_w, router_bias, w_exp_gate, w_exp_up, w_exp_down, w_sh_gate, w_sh_up, w_sh_down):
    depth = w_ada.shape[0]
    D = D_MODEL
    for l in range(depth):
        mod = _adaln(c, w_ada[l], b_ada[l])
        sh1, sc1, gt1, sh2, sc2, gt2 = [mod[:, k * D:(k + 1) * D] for k in range(N_MOD)]
        qn, kvn, qd, ckv, qi, small, gm, ki = _input_proj(x, g_pre_mix[l], sc1, sh1, w_in[l],
                                                          dsa_kv_norm[l])
        o_nsa, o_dsa = _token_mixers(qn, kvn, qd, ckv, qi, small, ki, rel_bias, cmp_pe[l], cmp_w1[l],
                                     cmp_b1[l], cmp_w2[l], dsa_w_uk[l], dsa_w_uv[l])
        x1, h2, hp_a, hp_b, lt = _merge(o_nsa, o_dsa, gm, x,
                                        w_branch_nsa[l], w_branch_dsa[l], w_out[l], g_post_mix[l],
                                        gt1, g_pre_ffn[l], sc2, sh2, router_w[l])
        x = _moe(h2, hp_a, hp_b, lt, x1, router_bias[l], w_exp_gate[l], w_exp_up[l],
                 w_exp_down[l], w_sh_gate[l], w_sh_up[l], w_sh_down[l], g_post_ffn[l], gt2)
    return x
```

```python
import functools
import math

import jax
import jax.numpy as jnp
import numpy as np
from jax import lax
from jax.experimental import pallas as pl
from jax.experimental.pallas import tpu as pltpu
from jax.experimental.pallas import tpu_sc as plsc

F32 = jnp.float32
BF16 = jnp.bfloat16
I32 = jnp.int32

D_MODEL = 1024
N_MOD = 6
NSA_HEADS = 8
NSA_KV_GROUPS = 2
NSA_HPG = NSA_HEADS // NSA_KV_GROUPS
NSA_HEAD_DIM = 64
CMP_BLOCK = 32
CMP_STRIDE = 16
CMP_HIDDEN = 128
SLC_BLOCK = 64
SLC_TOPN = 16
WINDOW = 512
FORCE_BONUS = 1e4
DSA_HEADS = 8
DSA_HEAD_DIM = 64
DSA_KV_LATENT = 128
IDX_HEADS = 8
IDX_DIM = 64
IDX_TOPK_MAX = 256
REL_BUCKETS = 32
REL_MAX_DISTANCE = 128
N_EXPERTS = 256
N_EXPERT_GROUPS = 8
TOPK_GROUPS = 4
TOP_K = 8
D_EXPERT = 256
ROUTED_SCALE = 2.5
RMS_EPS = 1e-6
NEG_INF = -1e30

LANE = 128
TQ = 128
NK = 256
M_FLOOR = -1e29
KNOCK = -3e38
CMP_PAD = 16
CMP_NEAR = 24
VMEM_LIMIT = 56 * 1024 * 1024
SC_WINDOW = 128
SC_WIDTH = 256

PROJ_WIDTHS = (512, 768, 24, 512, 128, 512, 64, 8, 2048)
PROJ_OFFS = tuple(int(v) for v in np.cumsum((0,) + PROJ_WIDTHS))


def _cparams(sem):
    return pltpu.CompilerParams(dimension_semantics=sem, vmem_limit_bytes=VMEM_LIMIT)


def _mod_body(c_ref, w_ref, b_ref, o_ref):
    c = c_ref[...]
    s = c * jax.nn.sigmoid(c)
    o_ref[...] = jnp.dot(s, w_ref[...], preferred_element_type=F32) + b_ref[...]


def _adaln(c, w_ada, b_ada):
    B, D = c.shape
    E = w_ada.shape[1]
    cp = jnp.zeros((8, D), F32).at[:B].set(c)
    out = pl.pallas_call(
        _mod_body,
        grid=(E // D,),
        in_specs=[pl.BlockSpec((8, D), lambda j: (0, 0)),
                  pl.BlockSpec((D, D), lambda j: (0, j)),
                  pl.BlockSpec((1, D), lambda j: (0, j))],
        out_specs=pl.BlockSpec((8, D), lambda j: (0, j)),
        out_shape=jax.ShapeDtypeStruct((8, E), F32),
        compiler_params=_cparams(("arbitrary",)),
        name="adaln_mod",
    )(cp, w_ada, b_ada.reshape(1, E))
    return out[:B]


PW = (512, 768, 512, 128, 512, 128, 2048, 128)
PO = tuple(int(v) for v in np.cumsum((0,) + PW))


def _heads_to_lanes(x, nh):
    cols = []
    for p in range(nh // 2):
        t = x[:, p * LANE:(p + 1) * LANE].T
        cols += [t[0:NSA_HEAD_DIM], t[NSA_HEAD_DIM:LANE]]
    return jnp.concatenate(cols, axis=1)


def _lanes_to_heads(y, nh):
    outs = []
    for p in range(nh // 2):
        pair = jnp.concatenate([y[:, (2 * p) * LANE:(2 * p + 1) * LANE],
                                y[:, (2 * p + 1) * LANE:(2 * p + 2) * LANE]], axis=0)
        outs.append(pair.T)
    return jnp.concatenate(outs, axis=1)


def _proj_body(x_ref, g_ref, sc_ref, sh_ref, w_ref, kn_ref,
               qn_ref, kv_ref, qd_ref, ckv_ref, qi_ref, sm_ref, gm_ref, ki_ref):
    x = x_ref[...]
    ms = jnp.mean(x * x, axis=-1, keepdims=True)
    h = x * lax.rsqrt(ms + RMS_EPS) * g_ref[...]
    h = h * (1.0 + sc_ref[...]) + sh_ref[...]
    hb = h.astype(BF16)

    def mm(k):
        return jnp.dot(hb, w_ref[:, PO[k]:PO[k + 1]], preferred_element_type=F32)

    qn_ref[...] = (mm(0) * NSA_HEAD_DIM ** -0.5).astype(BF16)
    kv_ref[...] = mm(1).astype(BF16)
    qd_ref[...] = mm(2).astype(BF16)
    c = mm(3)
    cms = jnp.mean(c * c, axis=-1, keepdims=True)
    ckv_ref[...] = (c * lax.rsqrt(cms + RMS_EPS) * kn_ref[...]).astype(BF16)
    qi_ref[...] = (mm(4) * IDX_DIM ** -0.5).astype(BF16)
    sm_ref[...] = mm(5)
    gm_ref[...] = mm(6)
    ki_ref[...] = mm(7).astype(BF16)


def _input_proj(x, g_pre, sc, sh, w_in, kv_norm):
    B, T, D = x.shape
    TM = 512 if T % 512 == 0 else TQ
    o = PROJ_OFFS
    small = jnp.concatenate([w_in[:, o[2]:o[3]], w_in[:, o[6]:o[7]], w_in[:, o[7]:o[8]],
                             jnp.zeros((D, 32), F32)], axis=1)
    w_r = jnp.concatenate([w_in[:, o[0]:o[1]], w_in[:, o[1]:o[2]], w_in[:, o[3]:o[4]],
                           w_in[:, o[4]:o[5]], w_in[:, o[5]:o[6]], small,
                           w_in[:, o[8]:o[9]], w_in[:, o[6]:o[7]],
                           jnp.zeros((D, LANE - IDX_DIM), F32)], axis=1).astype(BF16)
    row = lambda w: pl.BlockSpec((None, TM, w), lambda b, i: (b, i, 0))
    vec = pl.BlockSpec((None, 1, D), lambda b, i: (b, 0, 0))
    outs = pl.pallas_call(
        _proj_body,
        grid=(B, T // TM),
        in_specs=[row(D),
                  pl.BlockSpec((1, D), lambda b, i: (0, 0)),
                  vec, vec,
                  pl.BlockSpec((D, PO[-1]), lambda b, i: (0, 0)),
                  pl.BlockSpec((1, DSA_KV_LATENT), lambda b, i: (0, 0))],
        out_specs=[row(w) for w in PW],
        out_shape=[jax.ShapeDtypeStruct((B, T, PW[0]), BF16),
                   jax.ShapeDtypeStruct((B, T, PW[1]), BF16),
                   jax.ShapeDtypeStruct((B, T, PW[2]), BF16),
                   jax.ShapeDtypeStruct((B, T, PW[3]), BF16),
                   jax.ShapeDtypeStruct((B, T, PW[4]), BF16),
                   jax.ShapeDtypeStruct((B, T, PW[5]), F32),
                   jax.ShapeDtypeStruct((B, T, PW[6]), F32),
                   jax.ShapeDtypeStruct((B, T, PW[7]), BF16)],
        compiler_params=_cparams(("arbitrary", "arbitrary")),
        name="prenorm_proj",
    )(x, g_pre.reshape(1, D), sc.reshape(B, 1, D), sh.reshape(B, 1, D), w_r,
      kv_norm.reshape(1, DSA_KV_LATENT))
    return outs


def _t5_bucket(dist):
    n = jnp.maximum(dist, 0)
    exact = REL_BUCKETS // 2
    nf = jnp.maximum(n, 1).astype(F32)
    large = exact + (jnp.log(nf / exact) / math.log(REL_MAX_DISTANCE / exact)
                     * (REL_BUCKETS - exact)).astype(I32)
    return jnp.where(n < exact, n, jnp.minimum(large, REL_BUCKETS - 1))


def _delta(tab, dist):
    onehot = (_t5_bucket(dist)[..., None] == jnp.arange(REL_BUCKETS)).astype(F32)
    d = jnp.einsum('...b,bh->...h', onehot, tab - tab[REL_BUCKETS - 1],
                   precision=lax.Precision.HIGHEST)
    return jnp.where((dist >= 0)[..., None], d, NEG_INF)


def _near_tables(tab):
    H = tab.shape[1]
    c = jnp.arange(TQ)[:, None]
    q = jnp.arange(TQ)[None, :]
    tabs = [jnp.full((TQ, TQ, H), NEG_INF, F32)]
    for diff in (0, 1):
        tabs.append(_delta(tab, diff * TQ + q - c))
    tabs += [jnp.zeros((TQ, TQ, H), F32)] * 2
    t = jnp.stack(tabs)
    return t.transpose(0, 1, 3, 2).reshape(5, TQ, H * TQ)


def _cmp_body(r_ref, w1_ref, c_ref, w2_ref, o_ref):
    y = jnp.dot(r_ref[...], w1_ref[...], preferred_element_type=F32)
    M = y.shape[0]
    z = y[:, :CMP_HIDDEN] + pltpu.roll(y[:, CMP_HIDDEN:], M - 1, 0) + c_ref[...]
    z = jax.nn.gelu(z)
    o_ref[...] = jnp.dot(z.astype(BF16), w2_ref[...], preferred_element_type=F32)


def _compress(kv6, cmp_pe, cmp_w1, cmp_b1, cmp_w2):
    B, T = kv6.shape[:2]
    G, DH = NSA_KV_GROUPS, NSA_HEAD_DIM
    M = T // CMP_STRIDE
    half = CMP_STRIDE * DH
    r = kv6[:, :, 0:2].transpose(0, 2, 3, 1, 4).reshape(B, 2, G, M, half)
    w1 = cmp_w1.reshape(2, 2, half, CMP_HIDDEN)
    w1c = jnp.concatenate([w1[:, 0], w1[:, 1]], axis=-1).astype(BF16)
    cvec = (jnp.einsum('wld,wldk->wk', cmp_pe, cmp_w1, precision=lax.Precision.HIGHEST)
            + cmp_b1).reshape(2, 1, CMP_HIDDEN)
    return pl.pallas_call(
        _cmp_body,
        grid=(B, 2, G),
        in_specs=[pl.BlockSpec((None, None, None, M, half), lambda b, w, g: (b, w, g, 0, 0)),
                  pl.BlockSpec((None, half, 2 * CMP_HIDDEN), lambda b, w, g: (w, 0, 0)),
                  pl.BlockSpec((None, 1, CMP_HIDDEN), lambda b, w, g: (w, 0, 0)),
                  pl.BlockSpec((None, CMP_HIDDEN, DH), lambda b, w, g: (w, 0, 0))],
        out_specs=pl.BlockSpec((None, None, None, M, DH), lambda b, w, g: (b, w, g, 0, 0)),
        out_shape=jax.ShapeDtypeStruct((B, 2, G, M, DH), F32),
        compiler_params=_cparams(("arbitrary",) * 3),
        name="kv_compress",
    )(r, w1c, cvec, cmp_w2.astype(BF16))


def _nsa_cmp_body(q_ref, kc_ref, vct_ref, ct_ref, ovl_ref, oc_ref, sel_ref, qt_ref, s_ref,
                  *, n_blk, topn):
    i = pl.program_id(2)
    L = NSA_HPG * TQ
    qt = jnp.concatenate([_heads_to_lanes(q_ref[...].astype(F32), NSA_HPG), jnp.ones((1, L), F32),
                          jnp.zeros((LANE - NSA_HEAD_DIM - 1, L), F32)], axis=0).astype(BF16)
    qt_ref[...] = qt
    s_ref[...] = jnp.dot(kc_ref[...], qt, preferred_element_type=F32)
    near = pl.ds(pl.multiple_of(i * 8, 8), CMP_NEAR)
    s_ref[near, :] = s_ref[near, :] + ct_ref[...]
    s = s_ref[...]
    MP = s.shape[0]
    row = lax.broadcasted_iota(I32, (MP, 1), 0)
    s = jnp.where(row < i * 8 + CMP_NEAR, s, NEG_INF)
    m = jnp.maximum(jnp.max(s, axis=0, keepdims=True), M_FLOOR)
    p = jnp.exp(s - m)
    l = jnp.sum(p, axis=0, keepdims=True)
    p = p / jnp.maximum(l, 1e-30)
    pb = p.astype(BF16)
    oc_ref[...] = jnp.dot(vct_ref[...], pb, preferred_element_type=F32)[:NSA_HEAD_DIM]
    psum = p[:, 0:TQ]
    for hh in range(1, NSA_HPG):
        psum = psum + p[:, hh * TQ:(hh + 1) * TQ]
    imp = jnp.dot(ovl_ref[...], psum.astype(BF16), preferred_element_type=F32)

    blk = lax.broadcasted_iota(I32, (LANE, 1), 0)
    t = i * TQ + lax.broadcasted_iota(I32, (1, TQ), 1)
    cur = t >> int(math.log2(SLC_BLOCK))
    admiss = blk <= cur
    bonus = jnp.where(blk == 0, FORCE_BONUS, 0.0)
    bonus = jnp.where(blk == cur, FORCE_BONUS, bonus)
    bonus = jnp.where(blk == cur - 1, FORCE_BONUS, bonus)
    score = jnp.where(admiss, imp + bonus, NEG_INF)
    score = jnp.where(blk < n_blk, score, KNOCK)
    blk_b = jnp.broadcast_to(blk, (LANE, TQ))
    selneg = jnp.full((LANE, TQ), NEG_INF, F32)
    for _ in range(topn):
        mx = jnp.max(score, axis=0, keepdims=True)
        first = jnp.min(jnp.where(score == mx, blk_b, LANE), axis=0, keepdims=True)
        pick = blk_b == first
        selneg = jnp.where(pick, 0.0, selneg)
        score = jnp.where(pick, KNOCK, score)
    sel_ref[...] = jnp.where(admiss, selneg, NEG_INF).astype(BF16)


def _nsa_compressed(qn, kc_aug, vct, ct, ovl, n_blk):
    B, T = qn.shape[:2]
    G, nQ = NSA_KV_GROUPS, T // TQ
    MP = kc_aug.shape[2]
    body = functools.partial(_nsa_cmp_body, n_blk=n_blk, topn=min(SLC_TOPN, n_blk))
    qtile = pl.BlockSpec((None, None, None, LANE, NSA_HPG * TQ), lambda b, g, i: (b, g, i, 0, 0))
    return pl.pallas_call(
        body,
        grid=(B, G, nQ),
        in_specs=[pl.BlockSpec((None, TQ, NSA_HPG * NSA_HEAD_DIM), lambda b, g, i: (b, i, g)),
                  pl.BlockSpec((None, None, MP, LANE), lambda b, g, i: (b, g, 0, 0)),
                  pl.BlockSpec((None, None, LANE, MP), lambda b, g, i: (b, g, 0, 0)),
                  pl.BlockSpec((None, CMP_NEAR, NSA_HPG * TQ), lambda b, g, i: (g, 0, 0)),
                  pl.BlockSpec((LANE, MP), lambda b, g, i: (0, 0))],
        out_specs=[pl.BlockSpec((None, None, None, NSA_HEAD_DIM, NSA_HPG * TQ), lambda b, g, i: (b, g, i, 0, 0)),
                   pl.BlockSpec((None, None, None, LANE, TQ), lambda b, g, i: (b, g, i, 0, 0)),
                   qtile],
        out_shape=[jax.ShapeDtypeStruct((B, G, nQ, NSA_HEAD_DIM, NSA_HPG * TQ), F32),
                   jax.ShapeDtypeStruct((B, G, nQ, LANE, TQ), BF16),
                   jax.ShapeDtypeStruct((B, G, nQ, LANE, NSA_HPG * TQ), BF16)],
        scratch_shapes=[pltpu.VMEM((MP, NSA_HPG * TQ), F32)],
        compiler_params=_cparams(("arbitrary",) * 3),
        name="nsa_compressed",
    )(qn, kc_aug, vct, ct, ovl)


def _softmax_step(s, vt, m_old, acc):
    m_new = jnp.maximum(m_old, jnp.max(s, axis=0, keepdims=True))
    alpha = jnp.exp(m_old - m_new)
    p = jnp.exp(s - m_new).astype(BF16)
    return m_new, acc * alpha + jnp.dot(vt, p, preferred_element_type=F32)


AK = 256


def _pipeline(lo, hi, produce, consume, buf_a, buf_b):
    n = hi - lo

    @pl.when(n > 0)
    def _():
        produce(lo, buf_a)

        def body(p, c):
            j = lo + 2 * p
            produce(j + 1, buf_b)
            consume(j, buf_a)
            produce(j + 2, buf_a)
            consume(j + 1, buf_b)
            return c

        lax.fori_loop(0, (n - 1) // 2, body, 0)

        @pl.when(n % 2 == 1)
        def _():
            consume(hi - 1, buf_a)

        @pl.when(n % 2 == 0)
        def _():
            produce(hi - 1, buf_b)
            consume(hi - 2, buf_a)
            consume(hi - 1, buf_b)


def _attend(lo, hi, lhs_fn, rhs_ref, value_fn, m_ref, acc_ref, sa_ref, sb_ref, near_fn=None):
    def scores(j, buf):
        s = jnp.dot(lhs_fn(j), rhs_ref[...], preferred_element_type=F32)
        buf[...] = s if near_fn is None else near_fn(s, j)

    def consume(j, buf):
        m, acc = _softmax_step(buf[...], value_fn(j), m_ref[...], acc_ref[...])
        m_ref[...] = m
        acc_ref[...] = acc

    _pipeline(lo, hi, scores, consume, sa_ref, sb_ref)


def _near_add(dt_ref, i, s, j):
    parts = []
    for sub in range(AK // TQ):
        idx = jnp.clip(i - (j * (AK // TQ) + sub) + 1, 0, 4)
        parts.append(s[sub * TQ:(sub + 1) * TQ] + dt_ref[idx])
    return jnp.concatenate(parts, axis=0)


def _nsa_sel_body(qt_ref, sel_ref, ks_ref, vst_ref, dt_ref, o_ref, qa_ref, m_ref, acc_ref,
                  sa_ref, sb_ref):
    i = pl.program_id(2)
    L = NSA_HPG * TQ
    qa_ref[0:LANE, :] = qt_ref[...]
    selneg = sel_ref[...]
    qa_ref[LANE:2 * LANE, :] = jnp.concatenate([selneg] * NSA_HPG, axis=1)
    m_ref[...] = jnp.full((1, L), M_FLOOR, F32)
    acc_ref[...] = jnp.zeros(acc_ref.shape, F32)
    last = i // (AK // TQ)
    n_far = jnp.maximum(last - 1, 0)

    def keys(j):
        return ks_ref[pl.ds(pl.multiple_of(j * AK, AK), AK), :]

    def value(j):
        return vst_ref[:, pl.ds(pl.multiple_of(j * AK, AK), AK)]

    _attend(0, n_far, keys, qa_ref, value, m_ref, acc_ref, sa_ref, sb_ref)
    _attend(n_far, last + 1, keys, qa_ref, value, m_ref, acc_ref, sa_ref, sb_ref,
            near_fn=functools.partial(_near_add, dt_ref, i))
    acc = acc_ref[...]
    o_ref[...] = acc[:NSA_HEAD_DIM] / acc[NSA_HEAD_DIM:NSA_HEAD_DIM + 1]


def _nsa_selected(qt, selneg, ks_aug, vst, dt):
    B, G, nQ = qt.shape[:3]
    T = ks_aug.shape[2]
    L = NSA_HPG * TQ
    return pl.pallas_call(
        _nsa_sel_body,
        grid=(B, G, nQ),
        in_specs=[pl.BlockSpec((None, None, None, LANE, L), lambda b, g, i: (b, g, i, 0, 0)),
                  pl.BlockSpec((None, None, None, LANE, TQ), lambda b, g, i: (b, g, i, 0, 0)),
                  pl.BlockSpec((None, None, T, 2 * LANE), lambda b, g, i: (b, g, 0, 0)),
                  pl.BlockSpec((None, None, LANE, T), lambda b, g, i: (b, g, 0, 0)),
                  pl.BlockSpec((None, 5, TQ, L), lambda b, g, i: (g, 0, 0, 0))],
        out_specs=pl.BlockSpec((None, None, None, NSA_HEAD_DIM, L), lambda b, g, i: (b, g, i, 0, 0)),
        out_shape=jax.ShapeDtypeStruct((B, G, nQ, NSA_HEAD_DIM, L), F32),
        scratch_shapes=[pltpu.VMEM((2 * LANE, L), BF16),
                        pltpu.VMEM((1, L), F32),
                        pltpu.VMEM((LANE, L), F32),
                        pltpu.VMEM((AK, L), F32),
                        pltpu.VMEM((AK, L), F32)],
        compiler_params=_cparams(("arbitrary",) * 3),
        name="nsa_selected",
    )(qt, selneg, ks_aug, vst, dt)


def _nsa_win_body(qt_ref, kw_ref, vwt_ref, w0_ref, dt_ref, oc_ref, os_ref, gl_ref, o_ref):
    i = pl.program_id(2)
    span = WINDOW + TQ
    k0 = pl.multiple_of(i * TQ, TQ)
    s = jnp.dot(kw_ref[pl.ds(k0, span), :], qt_ref[...], preferred_element_type=F32)
    s = jnp.concatenate([s[0:TQ] + w0_ref[...], s[TQ:WINDOW - TQ],
                         s[WINDOW - TQ:WINDOW] + dt_ref[2], s[WINDOW:span] + dt_ref[1]], axis=0)
    m = jnp.maximum(jnp.max(s, axis=0, keepdims=True), M_FLOOR)
    p = jnp.exp(s - m).astype(BF16)
    acc = jnp.dot(vwt_ref[:, pl.ds(k0, span)], p, preferred_element_type=F32)
    ow = acc[:NSA_HEAD_DIM] / acc[NSA_HEAD_DIM:NSA_HEAD_DIM + 1]
    g = jax.nn.sigmoid(gl_ref[...])
    o_t = g[0:1] * oc_ref[...] + g[1:2] * os_ref[...] + g[2:3] * ow
    o_ref[...] = _lanes_to_heads(o_t, NSA_HPG).astype(BF16)


def _nsa_window(qt, kw_aug, vwt, w0, dt, oc, osel, gl):
    B, G, nQ = qt.shape[:3]
    TP = kw_aug.shape[2]
    L = NSA_HPG * TQ
    W = NSA_HPG * NSA_HEAD_DIM
    tile = lambda r: pl.BlockSpec((None, None, None, r, L), lambda b, g, i: (b, g, i, 0, 0))
    return pl.pallas_call(
        _nsa_win_body,
        grid=(B, G, nQ),
        in_specs=[tile(LANE),
                  pl.BlockSpec((None, None, TP, LANE), lambda b, g, i: (b, g, 0, 0)),
                  pl.BlockSpec((None, None, LANE, TP), lambda b, g, i: (b, g, 0, 0)),
                  pl.BlockSpec((TQ, L), lambda b, g, i: (0, 0)),
                  pl.BlockSpec((None, 5, TQ, L), lambda b, g, i: (g, 0, 0, 0)),
                  tile(NSA_HEAD_DIM), tile(NSA_HEAD_DIM), tile(8)],
        out_specs=pl.BlockSpec((None, TQ, W), lambda b, g, i: (b, i, g)),
        out_shape=jax.ShapeDtypeStruct((B, nQ * TQ, G * W), BF16),
        compiler_params=_cparams(("arbitrary",) * 3),
        name="nsa_window",
    )(qt, kw_aug, vwt, w0, dt, oc, osel, gl)


INT_MIN = -2 ** 31
I16 = jnp.int16
I16_MIN = -2 ** 15
CK = 2 * NK
KEY_BITS = 16


def _dsa_body(qd_ref, qi_ref, wt_ref, ki_ref, ckv_ref, ckvt_ref, wuk_ref, wuvt_ref, dt_ref,
              o_ref, sc_ref, qb_ref, m_ref, acc_ref, sa_ref, sb_ref, pl_ref, *, k_top):
    i = pl.program_id(1)
    H = DSA_HEADS
    L = H * TQ
    last = i // (NK // TQ)
    n_steps = last + 1
    t_row = i * TQ + lax.broadcasted_iota(I32, (1, TQ), 1)

    qit = jnp.concatenate([_heads_to_lanes(qi_ref[...].astype(F32), IDX_HEADS),
                           jnp.zeros((LANE - IDX_DIM, L), F32)], axis=0).astype(BF16)
    wt = wt_ref[...]

    def idx_scores(j, buf):
        k0 = pl.multiple_of(j * NK, NK)
        buf[...] = jnp.dot(ki_ref[pl.ds(k0, NK), :], qit, preferred_element_type=F32)

    def idx_reduce(j, buf):
        k0 = pl.multiple_of(j * NK, NK)
        r = jnp.maximum(buf[...], 0.0) * wt
        sc = r[:, 0:TQ]
        for h in range(1, H):
            sc = sc + r[:, h * TQ:(h + 1) * TQ]
        bits = pltpu.bitcast(sc, I32)
        ik = jnp.where(sc == 0.0, 0, bits ^ ((bits >> 31) & 0x7FFFFFFF))
        key = k0 + lax.broadcasted_iota(I32, (NK, 1), 0)
        ikm = jnp.where(key <= t_row, jnp.maximum(ik, INT_MIN + 1), INT_MIN)
        sc_ref[pl.ds(k0, NK), :] = ikm
        pl_ref[pl.ds(k0, NK), :] = (ikm >> 16).astype(I16)

    _pipeline(0, n_steps, idx_scores, idx_reduce, sa_ref, sb_ref)

    @pl.when(n_steps % 2 == 1)
    def _():
        tail = pl.ds(pl.multiple_of(n_steps * NK, NK), NK)
        sc_ref[tail, :] = jnp.full((NK, TQ), INT_MIN, I32)
        pl_ref[tail, :] = jnp.full((NK, TQ), I16_MIN, I16)

    n_chunks = (n_steps + 1) // 2

    def count(pred):
        def cbody(j, cnt):
            k0 = pl.multiple_of(j * CK, CK)
            ind = jnp.where(pred(sc_ref[pl.ds(k0, CK), :], k0), 1.0, 0.0)
            parts = [ind[r * 8:(r + 1) * 8] for r in range(CK // 8)]
            while len(parts) > 1:
                parts = [a + b for a, b in zip(parts[0::2], parts[1::2])]
            return cnt + parts[0]
        cnt8 = lax.fori_loop(0, n_chunks, cbody, jnp.zeros((8, TQ), F32))
        return jnp.sum(cnt8, axis=0, keepdims=True)

    def any_lane(cond):
        return jnp.max(jnp.where(cond, 1.0, 0.0)) > 0.0

    kf = float(k_top)
    need = t_row + 1 > k_top

    def count16(thr16):
        def cbody(j, cnt):
            blk = pl_ref[pl.ds(pl.multiple_of(j * CK, CK), CK), :]
            ind = jnp.where(blk >= thr16, jnp.int16(1), jnp.int16(0))
            parts = [ind[r * 16:(r + 1) * 16] for r in range(CK // 16)]
            while len(parts) > 1:
                parts = [a + b for a, b in zip(parts[0::2], parts[1::2])]
            return cnt + parts[0]
        cnt16 = lax.fori_loop(0, n_chunks, cbody, jnp.zeros((16, TQ), I16))
        return jnp.sum(cnt16.astype(F32), axis=0, keepdims=True)

    def search16(target, ct0):
        def bit_body(b, c):
            tu, ct = c
            cand = tu | jnp.left_shift(jnp.int32(1), 15 - b)
            cnt = count16((cand + I16_MIN).astype(I16))
            take = cnt >= target
            return jnp.where(take, cand, tu), jnp.where(take, cnt, ct)
        return lax.fori_loop(0, 16, bit_body, (jnp.zeros((1, TQ), I32), ct0))

    tu_hi, c_ge = search16(kf, (t_row + 1).astype(F32))
    hi_s = tu_hi + I16_MIN
    c_gt = jnp.where(hi_s == -I16_MIN - 1, 0.0,
                     count16((jnp.minimum(hi_s + 1, -I16_MIN - 1)).astype(I16)))

    def low_plane(j, c):
        rows = pl.ds(pl.multiple_of(j * CK, CK), CK)
        ik = sc_ref[rows, :]
        lo_s = (ik & 0xFFFF) + I16_MIN
        pl_ref[rows, :] = jnp.where((ik >> 16) == hi_s, lo_s, I16_MIN).astype(I16)
        return c

    lax.fori_loop(0, n_chunks, low_plane, 0)
    tu_lo, c_lo = search16(kf - c_gt, c_ge - c_gt)
    ct = c_gt + c_lo
    thr = jnp.maximum(((tu_hi << 16) | tu_lo) ^ INT_MIN, INT_MIN + 1)
    thr = jnp.where(need, thr, INT_MIN + 1)

    @pl.when(any_lane(need & (ct > kf)))
    def _():
        keep = kf - count(lambda blk, k0: blk > thr)

        def tie_count(bound):
            def pred(blk, k0):
                key = k0 + lax.broadcasted_iota(I32, (CK, 1), 0)
                return jnp.where(key < bound, blk, INT_MIN) == thr
            return count(pred)

        def pos_body(b, pos):
            cand = pos | jnp.left_shift(jnp.int32(1), KEY_BITS - 1 - b)
            return jnp.where(tie_count(cand) < keep, cand, pos)

        pos = lax.fori_loop(0, KEY_BITS, pos_body, jnp.zeros((1, TQ), I32))

        def fix(j, c):
            k0 = pl.multiple_of(j * CK, CK)
            blk = sc_ref[pl.ds(k0, CK), :]
            key = k0 + lax.broadcasted_iota(I32, (CK, 1), 0)
            drop = jnp.where(key > pos, blk, INT_MIN) == thr
            sc_ref[pl.ds(k0, CK), :] = jnp.where(drop, INT_MIN, blk)
            return c

        lax.fori_loop(0, n_chunks, fix, 0)

    qdt = _heads_to_lanes(qd_ref[...].astype(F32), H).astype(BF16)
    for h in range(H):
        ql = jnp.dot(wuk_ref[h], qdt[:, h * TQ:(h + 1) * TQ], preferred_element_type=F32)
        qb_ref[0:LANE, h * TQ:(h + 1) * TQ] = (ql * DSA_HEAD_DIM ** -0.5).astype(BF16)
    eye = (lax.broadcasted_iota(I32, (LANE, TQ), 0) == lax.broadcasted_iota(I32, (LANE, TQ), 1))
    eye = jnp.where(eye, 1.0, 0.0).astype(BF16)
    qb_ref[LANE:2 * LANE, :] = jnp.concatenate([eye] * H, axis=1)
    m_ref[...] = jnp.full((1, L), M_FLOOR, F32)
    acc_ref[...] = jnp.zeros(acc_ref.shape, F32)
    alast = i // (AK // TQ)
    n_far = jnp.maximum(alast - 1, 0)

    def keys(j):
        k0 = pl.multiple_of(j * AK, AK)
        nb = jnp.where(sc_ref[pl.ds(k0, AK), :] >= thr, 0.0, NEG_INF).astype(BF16)
        return jnp.concatenate([ckv_ref[pl.ds(k0, AK), :], nb], axis=1)

    def value(j):
        return ckvt_ref[:, pl.ds(pl.multiple_of(j * AK, AK), AK)]

    _attend(0, n_far, keys, qb_ref, value, m_ref, acc_ref, sa_ref, sb_ref)
    _attend(n_far, alast + 1, keys, qb_ref, value, m_ref, acc_ref, sa_ref, sb_ref,
            near_fn=functools.partial(_near_add, dt_ref, i))
    acc = acc_ref[...]
    olat = (acc[:DSA_KV_LATENT] / acc[DSA_KV_LATENT:DSA_KV_LATENT + 1]).astype(BF16)
    o_t = jnp.concatenate([jnp.dot(wuvt_ref[h], olat[:, h * TQ:(h + 1) * TQ],
                                   preferred_element_type=F32) for h in range(H)], axis=1)
    o_ref[...] = _lanes_to_heads(o_t, H).astype(BF16)


def _dsa(qd, qi, wt, ki, ckv, ckvt, wuk, wuvt, dt, k_top):
    B, T = ki.shape[:2]
    nQ = T // TQ
    H = DSA_HEADS
    L = H * TQ
    W = H * DSA_HEAD_DIM
    R = ckvt.shape[1]
    rows = pl.BlockSpec((None, TQ, W), lambda b, i: (b, i, 0))
    return pl.pallas_call(
        functools.partial(_dsa_body, k_top=k_top),
        grid=(B, nQ),
        in_specs=[rows, rows, pl.BlockSpec((None, None, 1, L), lambda b, i: (b, i, 0, 0)),
                  pl.BlockSpec((None, T, LANE), lambda b, i: (b, 0, 0)),
                  pl.BlockSpec((None, T, DSA_KV_LATENT), lambda b, i: (b, 0, 0)),
                  pl.BlockSpec((None, R, T), lambda b, i: (b, 0, 0)),
                  pl.BlockSpec((H, DSA_KV_LATENT, DSA_HEAD_DIM), lambda b, i: (0, 0, 0)),
                  pl.BlockSpec((H, DSA_HEAD_DIM, DSA_KV_LATENT), lambda b, i: (0, 0, 0)),
                  pl.BlockSpec((5, TQ, L), lambda b, i: (0, 0, 0))],
        out_specs=rows,
        out_shape=jax.ShapeDtypeStruct((B, T, W), BF16),
        scratch_shapes=[pltpu.VMEM((T, TQ), I32),
                        pltpu.VMEM((2 * LANE, L), BF16),
                        pltpu.VMEM((1, L), F32),
                        pltpu.VMEM((R, L), F32),
                        pltpu.VMEM((AK, L), F32),
                        pltpu.VMEM((AK, L), F32),
                        pltpu.VMEM((T, TQ), I16)],
        compiler_params=_cparams(("arbitrary",) * 2),
        name="dsa_attention",
    )(qd, qi, wt, ki, ckv, ckvt, wuk, wuvt, dt)


def _token_mixers(qn, kvn, qd, ckv, qi, small, ki, rel_bias, cmp_pe, cmp_w1, cmp_b1, cmp_w2,
                  dsa_w_uk, dsa_w_uv):
    B, T = qn.shape[:2]
    G, HPG, DH = NSA_KV_GROUPS, NSA_HPG, NSA_HEAD_DIM
    nQ = T // TQ
    n_blk = T // SLC_BLOCK
    L = HPG * TQ
    assert T % NK == 0 and n_blk <= LANE
    kv6 = kvn.reshape(B, T, 6, G, DH)

    tab_nsa = rel_bias[:, :NSA_HEADS]
    dt_nsa = _near_tables(tab_nsa).reshape(5, TQ, G, L).transpose(2, 0, 1, 3)
    dt_dsa = _near_tables(rel_bias[:, NSA_HEADS:])

    kvc = _compress(kv6, cmp_pe, cmp_w1, cmp_b1, cmp_w2)
    M = T // CMP_STRIDE
    MP = M + CMP_PAD
    kc = kvc[:, 0].astype(BF16)
    flag = jnp.concatenate([jnp.full((CMP_PAD, 1), NEG_INF, F32), jnp.zeros((M, 1), F32)]).astype(BF16)
    kc_aug = jnp.concatenate(
        [jnp.pad(kc, ((0, 0), (0, 0), (CMP_PAD, 0), (0, 0))),
         jnp.broadcast_to(flag, (B, G, MP, 1)),
         jnp.zeros((B, G, MP, LANE - DH - 1), BF16)], axis=-1)
    vct = jnp.pad(kvc[:, 1].astype(BF16).transpose(0, 1, 3, 2),
                  ((0, 0), (0, 0), (0, LANE - DH), (CMP_PAD, 0)))
    mrow = jnp.arange(CMP_NEAR)[:, None]
    qcol = jnp.arange(TQ)[None, :]
    ct = _delta(tab_nsa, qcol - CMP_STRIDE * mrow + (2 * TQ - CMP_BLOCK + 1))
    ct = ct.reshape(CMP_NEAR, TQ, G, HPG).transpose(2, 0, 3, 1).reshape(G, CMP_NEAR, L)
    n_cmp = (T - CMP_BLOCK) // CMP_STRIDE + 1
    cs = jnp.arange(M) * CMP_STRIDE
    ss = jnp.arange(LANE) * SLC_BLOCK
    ovl = (jnp.clip(jnp.minimum(cs[None, :] + CMP_BLOCK, ss[:, None] + SLC_BLOCK)
                    - jnp.maximum(cs[None, :], ss[:, None]), 0, None).astype(F32) / CMP_BLOCK)
    ovl = jnp.where((jnp.arange(M)[None, :] < n_cmp) & (jnp.arange(LANE)[:, None] < n_blk), ovl, 0.0)
    ovl = jnp.pad(ovl, ((0, 0), (CMP_PAD, 0))).astype(BF16)
    oc, selneg, qt = _nsa_compressed(qn, kc_aug, vct, ct, ovl, n_blk)

    ks = kv6[:, :, 2].transpose(0, 2, 1, 3)
    et = (jnp.arange(T)[:, None] // SLC_BLOCK == jnp.arange(LANE)[None, :]).astype(BF16)
    ks_aug = jnp.concatenate([ks, jnp.zeros((B, G, T, LANE - DH), BF16),
                              jnp.broadcast_to(et, (B, G, T, LANE))], axis=-1)
    vs = kv6[:, :, 3].transpose(0, 2, 3, 1)
    vst = jnp.concatenate([vs, jnp.ones((B, G, 1, T), BF16),
                           jnp.zeros((B, G, LANE - DH - 1, T), BF16)], axis=2)
    osel = _nsa_selected(qt, selneg, ks_aug, vst, dt_nsa)

    TP = T + WINDOW
    kw = jnp.pad(kv6[:, :, 4].transpose(0, 2, 1, 3), ((0, 0), (0, 0), (WINDOW, 0), (0, 0)))
    wflag = jnp.concatenate([jnp.full((WINDOW, 1), NEG_INF, F32), jnp.zeros((T, 1), F32)]).astype(BF16)
    kw_aug = jnp.concatenate([kw, jnp.broadcast_to(wflag, (B, G, TP, 1)),
                              jnp.zeros((B, G, TP, LANE - DH - 1), BF16)], axis=-1)
    vw = jnp.pad(kv6[:, :, 5].transpose(0, 2, 3, 1), ((0, 0), (0, 0), (0, 0), (WINDOW, 0)))
    vwt = jnp.concatenate([vw, jnp.ones((B, G, 1, TP), BF16),
                           jnp.zeros((B, G, LANE - DH - 1, TP), BF16)], axis=2)
    u = jnp.arange(TQ)[:, None]
    w0 = jnp.tile(jnp.where(u > jnp.arange(TQ)[None, :], 0.0, NEG_INF).astype(F32), (1, HPG))
    gl = small[..., 0:3 * NSA_HEADS].reshape(B, nQ, TQ, G, HPG, 3).transpose(0, 3, 1, 5, 4, 2)
    gl = jnp.pad(gl.reshape(B, G, nQ, 3, L), ((0, 0), (0, 0), (0, 0), (0, 5), (0, 0)))
    o_nsa = _nsa_window(qt, kw_aug, vwt, w0, dt_nsa, oc, osel, gl)

    H = DSA_HEADS
    w_idx = small[..., 24 + IDX_DIM:24 + IDX_DIM + IDX_HEADS] * IDX_HEADS ** -0.5
    wt = w_idx.reshape(B, nQ, TQ, IDX_HEADS).transpose(0, 1, 3, 2).reshape(B, nQ, 1, IDX_HEADS * TQ)
    ckvt = jnp.concatenate([ckv.transpose(0, 2, 1), jnp.ones((B, 1, T), BF16),
                            jnp.zeros((B, 7, T), BF16)], axis=1)
    wuk = dsa_w_uk.transpose(1, 0, 2).astype(BF16)
    wuvt = dsa_w_uv.transpose(1, 2, 0).astype(BF16)
    o_dsa = _dsa(qd, qi, wt, ki, ckv, ckvt, wuk, wuvt, dt_dsa, min(IDX_TOPK_MAX, T // 4))
    return o_nsa, o_dsa


def _rms(v, g):
    return v * lax.rsqrt(jnp.mean(v * v, axis=-1, keepdims=True) + RMS_EPS) * g


HI_MASK = -65536


def _pack_halves(v):
    w = v.shape[1] // 2
    lo = lax.shift_right_logical(pltpu.bitcast(v[:, :w], I32), 16)
    hi = pltpu.bitcast(v[:, w:], I32) & HI_MASK
    return lo | hi


def _unpack_halves(p):
    return pltpu.bitcast(p << 16, F32), pltpu.bitcast(p & HI_MASK, F32)


def _merge_body(on_ref, od_ref, gm_ref, x_ref, wn_ref, wd_ref, wo_ref, gpost_ref, gt_ref,
                gpre_ref, sc_ref, sh_ref, rwt_ref, x1_ref, h2_ref, hpa_ref, hpb_ref, lt_ref):
    D = D_MODEL
    ya = jnp.dot(on_ref[...], wn_ref[...], preferred_element_type=F32)
    yb = jnp.dot(od_ref[...], wd_ref[...], preferred_element_type=F32)
    y = jax.nn.sigmoid(gm_ref[:, 0:D]) * ya + jax.nn.sigmoid(gm_ref[:, D:2 * D]) * yb
    y2 = jnp.dot(y.astype(BF16), wo_ref[...], preferred_element_type=F32)
    x1 = x_ref[...] + gt_ref[...] * _rms(y2, gpost_ref[...])
    x1_ref[...] = x1
    h2 = (_rms(x1, gpre_ref[...]) * (1.0 + sc_ref[...]) + sh_ref[...]).astype(BF16)
    h2_ref[...] = h2
    packed = _pack_halves(h2.astype(F32))
    hpa_ref[...] = packed[:, :SC_WIDTH]
    hpb_ref[...] = packed[:, SC_WIDTH:]
    lt_ref[...] = lax.dot_general(rwt_ref[...], h2, (((1,), (1,)), ((), ())),
                                  preferred_element_type=F32)


def _merge(o_nsa, o_dsa, gm, x, wn, wd, wo, g_post, gt1, g_pre, sc2, sh2, router_w):
    B, T, D = x.shape
    TM = 256
    E = router_w.shape[1]
    row = lambda w: pl.BlockSpec((None, TM, w), lambda b, i: (b, i, 0))
    full = lambda a: pl.BlockSpec(a.shape, lambda b, i: (0,) * a.ndim)
    vecb = pl.BlockSpec((None, 1, D), lambda b, i: (b, 0, 0))
    vec = pl.BlockSpec((1, D), lambda b, i: (0, 0))
    wn, wd, wo = wn.astype(BF16), wd.astype(BF16), wo.astype(BF16)
    rwt = router_w.T.astype(BF16)
    return pl.pallas_call(
        _merge_body,
        grid=(B, T // TM),
        in_specs=[row(o_nsa.shape[-1]), row(o_dsa.shape[-1]), row(2 * D), row(D),
                  full(wn), full(wd), full(wo), vec, vecb, vec, vecb, vecb, full(rwt)],
        out_specs=[row(D), row(D), row(SC_WIDTH), row(SC_WIDTH),
                   pl.BlockSpec((None, E, TM), lambda b, i: (b, 0, i))],
        out_shape=[jax.ShapeDtypeStruct((B, T, D), F32),
                   jax.ShapeDtypeStruct((B, T, D), BF16),
                   jax.ShapeDtypeStruct((B, T, SC_WIDTH), I32),
                   jax.ShapeDtypeStruct((B, T, SC_WIDTH), I32),
                   jax.ShapeDtypeStruct((B, E, T), F32)],
        compiler_params=_cparams(("arbitrary",) * 2),
        name="merge_postnorm",
    )(o_nsa, o_dsa, gm, x, wn, wd, wo, g_post.reshape(1, D), gt1.reshape(B, 1, D),
      g_pre.reshape(1, D), sc2.reshape(B, 1, D), sh2.reshape(B, 1, D), rwt)


TR = 256
GSZ = N_EXPERTS // N_EXPERT_GROUPS


def _first_max(v, idx, big):
    mx = jnp.max(v, axis=0, keepdims=True)
    return jnp.min(jnp.where(v == mx, idx, big), axis=0, keepdims=True), mx


def _route_body(lt_ref, bias_ref, eid_ref, rk_ref, wt_ref, cnt_ref, carry_ref):
    first_step = (pl.program_id(0) == 0) & (pl.program_id(1) == 0)

    @pl.when(first_step)
    def _():
        carry_ref[...] = jnp.zeros(carry_ref.shape, F32)

    s = jax.nn.sigmoid(lt_ref[...])
    sel = s + bias_ref[:, 0:1]
    grow = lax.broadcasted_iota(I32, (GSZ, TR), 0)
    gs = []
    for g in range(N_EXPERT_GROUPS):
        blk = sel[g * GSZ:(g + 1) * GSZ]
        f1, m1 = _first_max(blk, grow, GSZ)
        m2 = jnp.max(jnp.where(grow == f1, KNOCK, blk), axis=0, keepdims=True)
        gs.append(m1 + m2)
    gsc = jnp.concatenate(gs, axis=0)
    gidx = lax.broadcasted_iota(I32, (N_EXPERT_GROUPS, TR), 0)
    gsel = jnp.zeros((N_EXPERT_GROUPS, TR), F32)
    for _ in range(TOPK_GROUPS):
        f, _mx = _first_max(gsc, gidx, N_EXPERT_GROUPS)
        pick = gidx == f
        gsel = jnp.where(pick, 1.0, gsel)
        gsc = jnp.where(pick, KNOCK, gsc)
    cur = jnp.concatenate(
        [jnp.where(gsel[g:g + 1] > 0.0, sel[g * GSZ:(g + 1) * GSZ], NEG_INF)
         for g in range(N_EXPERT_GROUPS)], axis=0)
    erow = lax.broadcasted_iota(I32, (N_EXPERTS, TR), 0)
    member = jnp.zeros((N_EXPERTS, TR), F32)
    picks, eids, ws = [], [], []
    for _ in range(TOP_K):
        f, _mx = _first_max(cur, erow, N_EXPERTS)
        pick = erow == f
        picks.append(pick)
        eids.append(f)
        ws.append(jnp.sum(jnp.where(pick, s, 0.0), axis=0, keepdims=True))
        member = jnp.where(pick, 1.0, member)
        cur = jnp.where(pick, KNOCK, cur)
    wsum = ws[0]
    for k in range(1, TOP_K):
        wsum = wsum + ws[k]
    mb = member.astype(BF16)
    upper = (lax.broadcasted_iota(I32, (TR, TR), 0) < lax.broadcasted_iota(I32, (TR, TR), 1))
    upper = jnp.where(upper, 1.0, 0.0).astype(BF16)
    rank = jnp.dot(mb, upper, preferred_element_type=F32) + carry_ref[:, 0:1]
    eid_ref[...] = jnp.concatenate(eids, axis=0)
    rk_ref[...] = jnp.concatenate(
        [jnp.sum(jnp.where(p, rank, 0.0), axis=0, keepdims=True) for p in picks], axis=0)
    wt_ref[...] = jnp.concatenate([w / wsum * ROUTED_SCALE for w in ws], axis=0)
    carry_ref[...] = carry_ref[...] + jnp.dot(mb, jnp.ones((TR, LANE), BF16),
                                              preferred_element_type=F32)
    cnt_ref[...] = carry_ref[...]


def _route(lt, router_bias):
    B, E, T = lt.shape
    out8 = lambda: pl.BlockSpec((None, TOP_K, TR), lambda b, i: (b, 0, i))
    return pl.pallas_call(
        _route_body,
        grid=(B, T // TR),
        in_specs=[pl.BlockSpec((None, E, TR), lambda b, i: (b, 0, i)),
                  pl.BlockSpec((E, 1), lambda b, i: (0, 0))],
        out_specs=[out8(), out8(), out8(), pl.BlockSpec((E, LANE), lambda b, i: (0, 0))],
        out_shape=[jax.ShapeDtypeStruct((B, TOP_K, T), I32),
                   jax.ShapeDtypeStruct((B, TOP_K, T), F32),
                   jax.ShapeDtypeStruct((B, TOP_K, T), F32),
                   jax.ShapeDtypeStruct((E, LANE), F32)],
        scratch_shapes=[pltpu.VMEM((E, LANE), F32)],
        compiler_params=_cparams(("arbitrary",) * 2),
        name="moe_router",
    )(lt, router_bias.reshape(E, 1))


BM = 256


def _gmm_body(be_ref, nv_ref, ea_ref, eb_ref, xa_ref, xb_ref, wga_ref, wua_ref, wda_ref,
              wgb_ref, wub_ref, wdb_ref, ya_ref, yb_ref, wgb, wub, wdb):
    b = pl.program_id(0)
    valid = nv_ref[b]

    @pl.when(valid > 0)
    def _():
        prev = be_ref[jnp.maximum(b - 1, 0)]
        changed = (b == 0) | (be_ref[b] != prev)

        @pl.when(changed & (ea_ref[b] == be_ref[b]))
        def _():
            wgb[...] = wga_ref[...].astype(BF16)
            wub[...] = wua_ref[...].astype(BF16)
            wdb[...] = wda_ref[...].astype(BF16)

        @pl.when(changed & (ea_ref[b] != be_ref[b]))
        def _():
            wgb[...] = wgb_ref[...].astype(BF16)
            wub[...] = wub_ref[...].astype(BF16)
            wdb[...] = wdb_ref[...].astype(BF16)

        live = lax.broadcasted_iota(I32, (BM, 1), 0) < valid
        lo_a, hi_a = _unpack_halves(jnp.where(live, xa_ref[...], 0))
        lo_b, hi_b = _unpack_halves(jnp.where(live, xb_ref[...], 0))
        xb = jnp.concatenate([lo_a, lo_b, hi_a, hi_b], axis=1).astype(BF16)
        gt = jnp.dot(xb, wgb[...], preferred_element_type=F32)
        up = jnp.dot(xb, wub[...], preferred_element_type=F32)
        a = (gt * jax.nn.sigmoid(gt) * up).astype(BF16)
        y = jnp.dot(a, wdb[...], preferred_element_type=F32)
        packed = _pack_halves(y.astype(BF16).astype(F32))
        ya_ref[...] = packed[:, :SC_WIDTH]
        yb_ref[...] = packed[:, SC_WIDTH:]

    @pl.when(valid == 0)
    def _():
        ya_ref[...] = jnp.zeros(ya_ref.shape, I32)
        yb_ref[...] = jnp.zeros(yb_ref.shape, I32)


def _grouped_mlp(xs_a, xs_b, blk_e, n_valid, exp_a, exp_b, w_g, w_u, w_d):
    P = xs_a.shape[0]
    D, De = w_g.shape[-2:]
    nb = P // BM
    rows = pl.BlockSpec((BM, SC_WIDTH), lambda b, be, nv, ea, eb: (b, 0))
    sel_a = lambda b, be, nv, ea, eb: (ea[b], 0, 0)
    sel_b = lambda b, be, nv, ea, eb: (eb[b], 0, 0)
    grid_spec = pltpu.PrefetchScalarGridSpec(
        num_scalar_prefetch=4,
        grid=(nb,),
        in_specs=[rows, rows,
                  pl.BlockSpec((None, D, De), sel_a),
                  pl.BlockSpec((None, D, De), sel_a),
                  pl.BlockSpec((None, De, D), sel_a),
                  pl.BlockSpec((None, D, De), sel_b),
                  pl.BlockSpec((None, D, De), sel_b),
                  pl.BlockSpec((None, De, D), sel_b)],
        out_specs=[rows, rows],
        scratch_shapes=[pltpu.VMEM((D, De), BF16), pltpu.VMEM((D, De), BF16),
                        pltpu.VMEM((De, D), BF16)])
    out = jax.ShapeDtypeStruct((P, SC_WIDTH), I32)
    return pl.pallas_call(
        _gmm_body,
        grid_spec=grid_spec,
        out_shape=[out, out],
        compiler_params=_cparams(("arbitrary",)),
        name="moe_experts",
    )(blk_e, n_valid, exp_a, exp_b, xs_a, xs_b, w_g, w_u, w_d, w_g, w_u, w_d)


def _sc_mesh():
    return plsc.VectorSubcoreMesh(core_axis_name="c", subcore_axis_name="s")


def _sc_gather(x, idx):
    n = idx.shape[0]

    @pl.kernel(out_type=jax.ShapeDtypeStruct((n, SC_WIDTH), x.dtype), mesh=_sc_mesh())
    def gather(x_hbm, i_hbm, o_hbm):
        def body(i_vmem, o_vmem):
            pltpu.sync_copy(x_hbm.at[i_vmem.at[0]], o_vmem)

        pltpu.emit_pipeline(
            body, grid=(n // SC_WINDOW,),
            in_specs=[pl.BlockSpec((1, SC_WINDOW), lambda i: (0, i))],
            out_specs=[pl.BlockSpec((SC_WINDOW, SC_WIDTH), lambda i: (i, 0))],
            core_axis_name=("c", "s"),
            dimension_semantics=(pltpu.PARALLEL,),
        )(i_hbm, o_hbm)

    return gather(x, idx.reshape(1, n))


def _sc_dispatch(x, slot, n_out):
    B, T, W = x.shape
    K = slot.shape[1]
    tb = T // SC_WINDOW
    n = B * K * T

    @pl.kernel(out_type=jax.ShapeDtypeStruct((n_out, W), x.dtype), mesh=_sc_mesh(), scratch_types=[])
    def scatter(x_hbm, i_hbm, o_hbm):
        def body(x_vmem, i_vmem):
            pltpu.sync_copy(x_vmem, o_hbm.at[i_vmem.at[0]])

        pltpu.emit_pipeline(
            body, grid=(n // SC_WINDOW,),
            in_specs=[pl.BlockSpec((SC_WINDOW, W), lambda i: ((i // (K * tb)) * tb + i % tb, 0)),
                      pl.BlockSpec((1, SC_WINDOW), lambda i: (0, i))],
            out_specs=[],
            core_axis_name=("c", "s"),
            dimension_semantics=(pltpu.PARALLEL,),
        )(x_hbm, i_hbm)

    return scatter(x.reshape(B * T, W), slot.reshape(1, n))


def _final_body(ya_ref, yb_ref, w_ref, h_ref, x1_ref, wg_ref, wu_ref, wd_ref, g_ref, gt_ref, o_ref):
    hb = h_ref[...]
    gt = jnp.dot(hb, wg_ref[...], preferred_element_type=F32)
    up = jnp.dot(hb, wu_ref[...], preferred_element_type=F32)
    a = (gt * jax.nn.sigmoid(gt) * up).astype(BF16)
    shared = jnp.dot(a, wd_ref[...], preferred_element_type=F32)
    w = w_ref[...]
    parts = None
    for k in range(TOP_K):
        wk = w[:, k:k + 1]
        lo_a, hi_a = _unpack_halves(ya_ref[k])
        lo_b, hi_b = _unpack_halves(yb_ref[k])
        terms = [lo_a * wk, lo_b * wk, hi_a * wk, hi_b * wk]
        parts = terms if parts is None else [p + t for p, t in zip(parts, terms)]
    y = jnp.concatenate(parts, axis=1) + shared
    o_ref[...] = x1_ref[...] + gt_ref[...] * _rms(y, g_ref[...])


def _final(yg_a, yg_b, w, h2, x1, ws_g, ws_u, ws_d, g_post, gt2):
    B, T, D = x1.shape
    TM = 256 if T % 256 == 0 else TQ
    row = lambda wd: pl.BlockSpec((None, TM, wd), lambda b, i: (b, i, 0))
    krows = pl.BlockSpec((None, TOP_K, TM, SC_WIDTH), lambda b, i: (b, 0, i, 0))
    full = lambda a: pl.BlockSpec(a.shape, lambda b, i: (0,) * a.ndim)
    ws_g, ws_u, ws_d = ws_g.astype(BF16), ws_u.astype(BF16), ws_d.astype(BF16)
    return pl.pallas_call(
        _final_body,
        grid=(B, T // TM),
        in_specs=[krows, krows, row(TOP_K), row(D), row(D),
                  full(ws_g), full(ws_u), full(ws_d),
                  pl.BlockSpec((1, D), lambda b, i: (0, 0)),
                  pl.BlockSpec((None, 1, D), lambda b, i: (b, 0, 0))],
        out_specs=row(D),
        out_shape=jax.ShapeDtypeStruct((B, T, D), F32),
        compiler_params=_cparams(("arbitrary",) * 2),
        name="shared_final",
    )(yg_a, yg_b, w, h2, x1, ws_g, ws_u, ws_d, g_post.reshape(1, D), gt2.reshape(B, 1, D))


def _slot_body(eid_ref, rk_ref, ps_ref, o_ref):
    erow = lax.broadcasted_iota(I32, (N_EXPERTS, TR), 0)
    ps = ps_ref[:, 0:1]
    rows = []
    for k in range(TOP_K):
        base = jnp.sum(jnp.where(erow == eid_ref[k:k + 1, :], ps, 0.0), axis=0, keepdims=True)
        rows.append(base + rk_ref[k:k + 1, :])
    o_ref[...] = jnp.concatenate(rows, axis=0).astype(I32)


def _slots(eid, rk, pstart):
    B, K, T = eid.shape
    blk = pl.BlockSpec((None, K, TR), lambda b, i: (b, 0, i))
    return pl.pallas_call(
        _slot_body,
        grid=(B, T // TR),
        in_specs=[blk, blk, pl.BlockSpec((N_EXPERTS, 1), lambda b, i: (0, 0))],
        out_specs=blk,
        out_shape=jax.ShapeDtypeStruct((B, K, T), I32),
        compiler_params=_cparams(("arbitrary",) * 2),
        name="moe_slots",
    )(eid, rk, pstart.astype(F32).reshape(N_EXPERTS, 1))


def _moe(h2, hp_a, hp_b, lt, x1, router_bias, w_g, w_u, w_d, ws_g, ws_u, ws_d, g_post, gt2):
    B, T, D = x1.shape
    N = B * T
    eid, rk, wts, cnt = _route(lt, router_bias)
    counts = cnt[:, 0].astype(I32)
    padded = (counts + BM - 1) // BM * BM
    pend = jnp.cumsum(padded)
    pstart = pend - padded
    nb = (N * TOP_K + N_EXPERTS * (BM - 1) + BM - 1) // BM
    P = nb * BM
    row0 = jnp.arange(nb, dtype=I32) * BM
    blk_e = jnp.minimum(jnp.sum(pend[None, :] <= row0[:, None], axis=1), N_EXPERTS - 1).astype(I32)
    n_valid = jnp.clip(pstart[blk_e] + counts[blk_e] - row0, 0, BM).astype(I32)
    active = counts > 0
    rank = jnp.cumsum(active.astype(I32)) - 1
    n_act = rank[-1] + 1
    eids = jnp.arange(N_EXPERTS, dtype=I32)
    act_list = jnp.sum(jnp.where((rank[None, :] == eids[:, None]) & active[None, :], eids[None, :], 0),
                       axis=1)
    pos = jnp.where(n_valid > 0, rank[blk_e], n_act - 1)
    exp_a = act_list[jnp.minimum(pos + pos % 2, n_act - 1)].astype(I32)
    exp_b = act_list[jnp.minimum(pos + 1 - pos % 2, n_act - 1)].astype(I32)
    slot = _slots(eid, rk, pstart)
    xs_a = _sc_dispatch(hp_a, slot, P)
    xs_b = _sc_dispatch(hp_b, slot, P)
    ys_a, ys_b = _grouped_mlp(xs_a, xs_b, blk_e, n_valid, exp_a, exp_b, w_g, w_u, w_d)
    flat = slot.reshape(N * TOP_K)
    yg_a = _sc_gather(ys_a, flat).reshape(B, TOP_K, T, SC_WIDTH)
    yg_b = _sc_gather(ys_b, flat).reshape(B, TOP_K, T, SC_WIDTH)
    return _final(yg_a, yg_b, wts.transpose(0, 2, 1), h2, x1, ws_g, ws_u, ws_d, g_post, gt2)


def kernel(x, c, w_ada, b_ada, g_pre_mix, g_post_mix, g_pre_ffn, g_post_ffn, w_in, rel_bias, cmp_pe, cmp_w1, cmp_b1, cmp_w2, dsa_kv_norm, dsa_w_uk, dsa_w_uv, w_branch_nsa, w_branch_dsa, w_out, router_w, router_bias, w_exp_gate, w_exp_up, w_exp_down, w_sh_gate, w_sh_up, w_sh_down):
    depth = w_ada.shape[0]
    D = D_MODEL
    for l in range(depth):
        mod = _adaln(c, w_ada[l], b_ada[l])
        sh1, sc1, gt1, sh2, sc2, gt2 = [mod[:, k * D:(k + 1) * D] for k in range(N_MOD)]
        qn, kvn, qd, ckv, qi, small, gm, ki = _input_proj(x, g_pre_mix[l], sc1, sh1, w_in[l],
                                                          dsa_kv_norm[l])
        o_nsa, o_dsa = _token_mixers(qn, kvn, qd, ckv, qi, small, ki, rel_bias, cmp_pe[l], cmp_w1[l],
                                     cmp_b1[l], cmp_w2[l], dsa_w_uk[l], dsa_w_uv[l])
        x1, h2, hp_a, hp_b, lt = _merge(o_nsa, o_dsa, gm, x,
                                        w_branch_nsa[l], w_branch_dsa[l], w_out[l], g_post_mix[l],
                                        gt1, g_pre_ffn[l], sc2, sh2, router_w[l])
        x = _moe(h2, hp_a, hp_b, lt, x1, router_bias[l], w_exp_gate[l], w_exp_up[l],
                 w_exp_down[l], w_sh_gate[l], w_sh_up[l], w_sh_down[l], g_post_ffn[l], gt2)
    return x
```

```python
import functools
import math

import jax
import jax.numpy as jnp
import numpy as np
from jax import lax
from jax.experimental import pallas as pl
from jax.experimental.pallas import tpu as pltpu
from jax.experimental.pallas import tpu_sc as plsc

F32 = jnp.float32
BF16 = jnp.bfloat16
I32 = jnp.int32

D_MODEL = 1024
N_MOD = 6
NSA_HEADS = 8
NSA_KV_GROUPS = 2
NSA_HPG = NSA_HEADS // NSA_KV_GROUPS
NSA_HEAD_DIM = 64
CMP_BLOCK = 32
CMP_STRIDE = 16
CMP_HIDDEN = 128
SLC_BLOCK = 64
SLC_TOPN = 16
WINDOW = 512
FORCE_BONUS = 1e4
DSA_HEADS = 8
DSA_HEAD_DIM = 64
DSA_KV_LATENT = 128
IDX_HEADS = 8
IDX_DIM = 64
IDX_TOPK_MAX = 256
REL_BUCKETS = 32
REL_MAX_DISTANCE = 128
N_EXPERTS = 256
N_EXPERT_GROUPS = 8
TOPK_GROUPS = 4
TOP_K = 8
D_EXPERT = 256
ROUTED_SCALE = 2.5
RMS_EPS = 1e-6
NEG_INF = -1e30
LOG2E = math.log2(math.e)

LANE = 128
TQ = 128
NK = 256
M_FLOOR = -1e29
KNOCK = -3e38
CMP_PAD = 16
CMP_NEAR = 24
V_ROWS = 80
VMEM_LIMIT = 56 * 1024 * 1024
SC_WINDOW = 128
SC_WIDTH = 256

PROJ_WIDTHS = (512, 768, 24, 512, 128, 512, 64, 8, 2048)
PROJ_OFFS = tuple(int(v) for v in np.cumsum((0,) + PROJ_WIDTHS))


def _cparams(sem):
    return pltpu.CompilerParams(dimension_semantics=sem, vmem_limit_bytes=VMEM_LIMIT)


def _mod_body(c_ref, w_ref, b_ref, o_ref):
    c = c_ref[...]
    s = c * jax.nn.sigmoid(c)
    o_ref[...] = jnp.dot(s, w_ref[...], preferred_element_type=F32) + b_ref[...]


def _adaln(c, w_ada, b_ada):
    B, D = c.shape
    E = w_ada.shape[1]
    cp = jnp.zeros((8, D), F32).at[:B].set(c)
    out = pl.pallas_call(
        _mod_body,
        grid=(E // D,),
        in_specs=[pl.BlockSpec((8, D), lambda j: (0, 0)),
                  pl.BlockSpec((D, D), lambda j: (0, j)),
                  pl.BlockSpec((1, D), lambda j: (0, j))],
        out_specs=pl.BlockSpec((8, D), lambda j: (0, j)),
        out_shape=jax.ShapeDtypeStruct((8, E), F32),
        compiler_params=_cparams(("arbitrary",)),
        name="adaln_mod",
    )(cp, w_ada, b_ada.reshape(1, E))
    return out[:B]


PW = (512, 768, 512, 128, 512, 128, 2048, 128)
PO = tuple(int(v) for v in np.cumsum((0,) + PW))


def _heads_to_lanes(x, nh):
    cols = []
    for p in range(nh // 2):
        t = x[:, p * LANE:(p + 1) * LANE].T
        cols += [t[0:NSA_HEAD_DIM], t[NSA_HEAD_DIM:LANE]]
    return jnp.concatenate(cols, axis=1)


def _lanes_to_heads(y, nh):
    outs = []
    for p in range(nh // 2):
        pair = jnp.concatenate([y[:, (2 * p) * LANE:(2 * p + 1) * LANE],
                                y[:, (2 * p + 1) * LANE:(2 * p + 2) * LANE]], axis=0)
        outs.append(pair.T)
    return jnp.concatenate(outs, axis=1)


def _proj_body(x_ref, g_ref, sc_ref, sh_ref, w_ref, kn_ref,
               qn_ref, kv_ref, qd_ref, ckv_ref, qi_ref, sm_ref, gm_ref, ki_ref):
    x = x_ref[...]
    ms = jnp.mean(x * x, axis=-1, keepdims=True)
    h = x * lax.rsqrt(ms + RMS_EPS) * g_ref[...]
    h = h * (1.0 + sc_ref[...]) + sh_ref[...]
    hb = h.astype(BF16)

    def mm(k):
        return jnp.dot(hb, w_ref[:, PO[k]:PO[k + 1]], preferred_element_type=F32)

    qn_ref[...] = (mm(0) * (NSA_HEAD_DIM ** -0.5 * LOG2E)).astype(BF16)
    kv_ref[...] = mm(1).astype(BF16)
    qd_ref[...] = mm(2).astype(BF16)
    c = mm(3)
    cms = jnp.mean(c * c, axis=-1, keepdims=True)
    ckv_ref[...] = (c * lax.rsqrt(cms + RMS_EPS) * kn_ref[...]).astype(BF16)
    qi_ref[...] = (mm(4) * IDX_DIM ** -0.5).astype(BF16)
    sm_ref[...] = mm(5)
    gm_ref[...] = mm(6)
    ki_ref[...] = mm(7).astype(BF16)


def _input_proj(x, g_pre, sc, sh, w_in, kv_norm):
    B, T, D = x.shape
    TM = 512 if T % 512 == 0 else TQ
    o = PROJ_OFFS
    small = jnp.concatenate([w_in[:, o[2]:o[3]], w_in[:, o[6]:o[7]], w_in[:, o[7]:o[8]],
                             jnp.zeros((D, 32), F32)], axis=1)
    w_r = jnp.concatenate([w_in[:, o[0]:o[1]], w_in[:, o[1]:o[2]], w_in[:, o[3]:o[4]],
                           w_in[:, o[4]:o[5]], w_in[:, o[5]:o[6]], small,
                           w_in[:, o[8]:o[9]], w_in[:, o[6]:o[7]],
                           jnp.zeros((D, LANE - IDX_DIM), F32)], axis=1).astype(BF16)
    row = lambda w: pl.BlockSpec((None, TM, w), lambda b, i: (b, i, 0))
    vec = pl.BlockSpec((None, 1, D), lambda b, i: (b, 0, 0))
    outs = pl.pallas_call(
        _proj_body,
        grid=(B, T // TM),
        in_specs=[row(D),
                  pl.BlockSpec((1, D), lambda b, i: (0, 0)),
                  vec, vec,
                  pl.BlockSpec((D, PO[-1]), lambda b, i: (0, 0)),
                  pl.BlockSpec((1, DSA_KV_LATENT), lambda b, i: (0, 0))],
        out_specs=[row(w) for w in PW],
        out_shape=[jax.ShapeDtypeStruct((B, T, PW[0]), BF16),
                   jax.ShapeDtypeStruct((B, T, PW[1]), BF16),
                   jax.ShapeDtypeStruct((B, T, PW[2]), BF16),
                   jax.ShapeDtypeStruct((B, T, PW[3]), BF16),
                   jax.ShapeDtypeStruct((B, T, PW[4]), BF16),
                   jax.ShapeDtypeStruct((B, T, PW[5]), F32),
                   jax.ShapeDtypeStruct((B, T, PW[6]), F32),
                   jax.ShapeDtypeStruct((B, T, PW[7]), BF16)],
        compiler_params=_cparams(("arbitrary", "arbitrary")),
        name="prenorm_proj",
    )(x, g_pre.reshape(1, D), sc.reshape(B, 1, D), sh.reshape(B, 1, D), w_r,
      kv_norm.reshape(1, DSA_KV_LATENT))
    return outs


def _t5_bucket(dist):
    n = jnp.maximum(dist, 0)
    exact = REL_BUCKETS // 2
    nf = jnp.maximum(n, 1).astype(F32)
    large = exact + (jnp.log(nf / exact) / math.log(REL_MAX_DISTANCE / exact)
                     * (REL_BUCKETS - exact)).astype(I32)
    return jnp.where(n < exact, n, jnp.minimum(large, REL_BUCKETS - 1))


def _delta(tab, dist):
    onehot = (_t5_bucket(dist)[..., None] == jnp.arange(REL_BUCKETS)).astype(F32)
    d = jnp.einsum('...b,bh->...h', onehot, tab - tab[REL_BUCKETS - 1],
                   precision=lax.Precision.HIGHEST)
    return jnp.where((dist >= 0)[..., None], d * LOG2E, NEG_INF)


def _near_tables(tab):
    H = tab.shape[1]
    c = jnp.arange(TQ)[:, None]
    q = jnp.arange(TQ)[None, :]
    tabs = [jnp.full((TQ, TQ, H), NEG_INF, F32)]
    for diff in (0, 1):
        tabs.append(_delta(tab, diff * TQ + q - c))
    tabs += [jnp.zeros((TQ, TQ, H), F32)] * 2
    t = jnp.stack(tabs)
    return t.transpose(0, 1, 3, 2).reshape(5, TQ, H * TQ)


def _cmp_body(r_ref, w1_ref, c_ref, w2_ref, o_ref):
    y = jnp.dot(r_ref[...], w1_ref[...], preferred_element_type=F32)
    M = y.shape[0]
    z = y[:, :CMP_HIDDEN] + pltpu.roll(y[:, CMP_HIDDEN:], M - 1, 0) + c_ref[...]
    z = jax.nn.gelu(z)
    o_ref[...] = jnp.dot(z.astype(BF16), w2_ref[...], preferred_element_type=F32)


def _compress(kv6, cmp_pe, cmp_w1, cmp_b1, cmp_w2):
    B, T = kv6.shape[:2]
    G, DH = NSA_KV_GROUPS, NSA_HEAD_DIM
    M = T // CMP_STRIDE
    half = CMP_STRIDE * DH
    r = kv6[:, :, 0:2].transpose(0, 2, 3, 1, 4).reshape(B, 2, G, M, half)
    w1 = cmp_w1.reshape(2, 2, half, CMP_HIDDEN)
    w1c = jnp.concatenate([w1[:, 0], w1[:, 1]], axis=-1).astype(BF16)
    cvec = (jnp.einsum('wld,wldk->wk', cmp_pe, cmp_w1, precision=lax.Precision.HIGHEST)
            + cmp_b1).reshape(2, 1, CMP_HIDDEN)
    return pl.pallas_call(
        _cmp_body,
        grid=(B, 2, G),
        in_specs=[pl.BlockSpec((None, None, None, M, half), lambda b, w, g: (b, w, g, 0, 0)),
                  pl.BlockSpec((None, half, 2 * CMP_HIDDEN), lambda b, w, g: (w, 0, 0)),
                  pl.BlockSpec((None, 1, CMP_HIDDEN), lambda b, w, g: (w, 0, 0)),
                  pl.BlockSpec((None, CMP_HIDDEN, DH), lambda b, w, g: (w, 0, 0))],
        out_specs=pl.BlockSpec((None, None, None, M, DH), lambda b, w, g: (b, w, g, 0, 0)),
        out_shape=jax.ShapeDtypeStruct((B, 2, G, M, DH), F32),
        compiler_params=_cparams(("arbitrary",) * 3),
        name="kv_compress",
    )(r, w1c, cvec, cmp_w2.astype(BF16))


def _nsa_cmp_body(q_ref, kc_ref, vct_ref, ct_ref, ovl_ref, oc_ref, sel_ref, qt_ref, s_ref,
                  *, n_blk, topn):
    i = pl.program_id(2)
    L = NSA_HPG * TQ
    qt = jnp.concatenate([_heads_to_lanes(q_ref[...].astype(F32), NSA_HPG), jnp.ones((1, L), F32),
                          jnp.zeros((LANE - NSA_HEAD_DIM - 1, L), F32)], axis=0).astype(BF16)
    qt_ref[...] = qt
    s_ref[...] = jnp.dot(kc_ref[...], qt, preferred_element_type=F32)
    near = pl.ds(pl.multiple_of(i * 8, 8), CMP_NEAR)
    s_ref[near, :] = s_ref[near, :] + ct_ref[...]
    s = s_ref[...]
    MP = s.shape[0]
    row = lax.broadcasted_iota(I32, (MP, 1), 0)
    s = jnp.where(row < i * 8 + CMP_NEAR, s, NEG_INF)
    m = jnp.maximum(jnp.max(s, axis=0, keepdims=True), M_FLOOR)
    p = jnp.exp2(s - m)
    l = jnp.sum(p, axis=0, keepdims=True)
    p = p / jnp.maximum(l, 1e-30)
    pb = p.astype(BF16)
    oc_ref[...] = jnp.dot(vct_ref[...], pb, preferred_element_type=F32)[:NSA_HEAD_DIM]
    psum = p[:, 0:TQ]
    for hh in range(1, NSA_HPG):
        psum = psum + p[:, hh * TQ:(hh + 1) * TQ]
    imp = jnp.dot(ovl_ref[...], psum.astype(BF16), preferred_element_type=F32)

    blk = lax.broadcasted_iota(I32, (LANE, 1), 0)
    t = i * TQ + lax.broadcasted_iota(I32, (1, TQ), 1)
    cur = t >> int(math.log2(SLC_BLOCK))
    admiss = blk <= cur
    bonus = jnp.where(blk == 0, FORCE_BONUS, 0.0)
    bonus = jnp.where(blk == cur, FORCE_BONUS, bonus)
    bonus = jnp.where(blk == cur - 1, FORCE_BONUS, bonus)
    score = jnp.where(admiss, imp + bonus, NEG_INF)
    score = jnp.where(blk < n_blk, score, KNOCK)
    blk_b = jnp.broadcast_to(blk, (LANE, TQ))
    selneg = jnp.full((LANE, TQ), NEG_INF, F32)
    for _ in range(topn):
        mx = jnp.max(score, axis=0, keepdims=True)
        first = jnp.min(jnp.where(score == mx, blk_b, LANE), axis=0, keepdims=True)
        pick = blk_b == first
        selneg = jnp.where(pick, 0.0, selneg)
        score = jnp.where(pick, KNOCK, score)
    sel_ref[...] = jnp.where(admiss, selneg, NEG_INF).astype(BF16)


def _nsa_compressed(qn, kc_aug, vct, ct, ovl, n_blk):
    B, T = qn.shape[:2]
    G, nQ = NSA_KV_GROUPS, T // TQ
    MP = kc_aug.shape[2]
    body = functools.partial(_nsa_cmp_body, n_blk=n_blk, topn=min(SLC_TOPN, n_blk))
    qtile = pl.BlockSpec((None, None, None, LANE, NSA_HPG * TQ), lambda b, g, i: (b, g, i, 0, 0))
    return pl.pallas_call(
        body,
        grid=(B, G, nQ),
        in_specs=[pl.BlockSpec((None, TQ, NSA_HPG * NSA_HEAD_DIM), lambda b, g, i: (b, i, g)),
                  pl.BlockSpec((None, None, MP, LANE), lambda b, g, i: (b, g, 0, 0)),
                  pl.BlockSpec((None, None, LANE, MP), lambda b, g, i: (b, g, 0, 0)),
                  pl.BlockSpec((None, CMP_NEAR, NSA_HPG * TQ), lambda b, g, i: (g, 0, 0)),
                  pl.BlockSpec((LANE, MP), lambda b, g, i: (0, 0))],
        out_specs=[pl.BlockSpec((None, None, None, NSA_HEAD_DIM, NSA_HPG * TQ), lambda b, g, i: (b, g, i, 0, 0)),
                   pl.BlockSpec((None, None, None, LANE, TQ), lambda b, g, i: (b, g, i, 0, 0)),
                   qtile],
        out_shape=[jax.ShapeDtypeStruct((B, G, nQ, NSA_HEAD_DIM, NSA_HPG * TQ), F32),
                   jax.ShapeDtypeStruct((B, G, nQ, LANE, TQ), BF16),
                   jax.ShapeDtypeStruct((B, G, nQ, LANE, NSA_HPG * TQ), BF16)],
        scratch_shapes=[pltpu.VMEM((MP, NSA_HPG * TQ), F32)],
        compiler_params=_cparams(("arbitrary",) * 3),
        name="nsa_compressed",
    )(qn, kc_aug, vct, ct, ovl)


def _softmax_step(s, vt, m_old, acc):
    m_new = jnp.maximum(m_old, jnp.max(s, axis=0, keepdims=True))
    alpha = jnp.exp2(m_old - m_new)
    p = jnp.exp2(s - m_new).astype(BF16)
    return m_new, acc * alpha + jnp.dot(vt, p, preferred_element_type=F32)


AK = 256


def _pipeline(lo, hi, produce, consume, buf_a, buf_b):
    n = hi - lo

    @pl.when(n > 0)
    def _():
        produce(lo, buf_a)

        def body(p, c):
            j = lo + 2 * p
            produce(j + 1, buf_b)
            consume(j, buf_a)
            produce(j + 2, buf_a)
            consume(j + 1, buf_b)
            return c

        lax.fori_loop(0, (n - 1) // 2, body, 0)

        @pl.when(n % 2 == 1)
        def _():
            consume(hi - 1, buf_a)

        @pl.when(n % 2 == 0)
        def _():
            produce(hi - 1, buf_b)
            consume(hi - 2, buf_a)
            consume(hi - 1, buf_b)


def _attend(lo, hi, lhs_fn, rhs_ref, value_fn, m_ref, acc_ref, sa_ref, sb_ref, near_fn=None):
    def scores(j, buf):
        s = jnp.dot(lhs_fn(j), rhs_ref[...], preferred_element_type=F32)
        buf[...] = s if near_fn is None else near_fn(s, j)

    def consume(j, buf):
        m, acc = _softmax_step(buf[...], value_fn(j), m_ref[...], acc_ref[...])
        m_ref[...] = m
        acc_ref[...] = acc

    _pipeline(lo, hi, scores, consume, sa_ref, sb_ref)


def _near_add(dt_ref, i, s, j):
    parts = []
    for sub in range(AK // TQ):
        idx = jnp.clip(i - (j * (AK // TQ) + sub) + 1, 0, 4)
        parts.append(s[sub * TQ:(sub + 1) * TQ] + dt_ref[idx])
    return jnp.concatenate(parts, axis=0)


def _nsa_sel_body(qt_ref, sel_ref, ks_ref, vst_ref, dt_ref, o_ref, qa_ref, m_ref, acc_ref,
                  sa_ref, sb_ref):
    i = pl.program_id(2)
    L = NSA_HPG * TQ
    qa_ref[0:LANE, :] = qt_ref[...]
    selneg = sel_ref[...]
    qa_ref[LANE:2 * LANE, :] = jnp.concatenate([selneg] * NSA_HPG, axis=1)
    m_ref[...] = jnp.full((1, L), M_FLOOR, F32)
    acc_ref[...] = jnp.zeros(acc_ref.shape, F32)
    last = i // (AK // TQ)
    n_far = jnp.maximum(last - 1, 0)

    def keys(j):
        return ks_ref[pl.ds(pl.multiple_of(j * AK, AK), AK), :]

    def value(j):
        return vst_ref[:, pl.ds(pl.multiple_of(j * AK, AK), AK)]

    _attend(0, n_far, keys, qa_ref, value, m_ref, acc_ref, sa_ref, sb_ref)
    _attend(n_far, last + 1, keys, qa_ref, value, m_ref, acc_ref, sa_ref, sb_ref,
            near_fn=functools.partial(_near_add, dt_ref, i))
    acc = acc_ref[...]
    o_ref[...] = acc[:NSA_HEAD_DIM] / acc[NSA_HEAD_DIM:NSA_HEAD_DIM + 1]


def _nsa_selected(qt, selneg, ks_aug, vst, dt):
    B, G, nQ = qt.shape[:3]
    T = ks_aug.shape[2]
    R = vst.shape[2]
    L = NSA_HPG * TQ
    return pl.pallas_call(
        _nsa_sel_body,
        grid=(B, G, nQ),
        in_specs=[pl.BlockSpec((None, None, None, LANE, L), lambda b, g, i: (b, g, i, 0, 0)),
                  pl.BlockSpec((None, None, None, LANE, TQ), lambda b, g, i: (b, g, i, 0, 0)),
                  pl.BlockSpec((None, None, T, 2 * LANE), lambda b, g, i: (b, g, 0, 0)),
                  pl.BlockSpec((None, None, R, T), lambda b, g, i: (b, g, 0, 0)),
                  pl.BlockSpec((None, 5, TQ, L), lambda b, g, i: (g, 0, 0, 0))],
        out_specs=pl.BlockSpec((None, None, None, NSA_HEAD_DIM, L), lambda b, g, i: (b, g, i, 0, 0)),
        out_shape=jax.ShapeDtypeStruct((B, G, nQ, NSA_HEAD_DIM, L), F32),
        scratch_shapes=[pltpu.VMEM((2 * LANE, L), BF16),
                        pltpu.VMEM((1, L), F32),
                        pltpu.VMEM((R, L), F32),
                        pltpu.VMEM((AK, L), F32),
                        pltpu.VMEM((AK, L), F32)],
        compiler_params=_cparams(("arbitrary",) * 3),
        name="nsa_selected",
    )(qt, selneg, ks_aug, vst, dt)


def _nsa_win_body(qt_ref, kw_ref, vwt_ref, w0_ref, dt_ref, oc_ref, os_ref, gl_ref, o_ref):
    i = pl.program_id(2)
    span = WINDOW + TQ
    k0 = pl.multiple_of(i * TQ, TQ)
    s = jnp.dot(kw_ref[pl.ds(k0, span), :], qt_ref[...], preferred_element_type=F32)
    s = jnp.concatenate([s[0:TQ] + w0_ref[...], s[TQ:WINDOW - TQ],
                         s[WINDOW - TQ:WINDOW] + dt_ref[2], s[WINDOW:span] + dt_ref[1]], axis=0)
    m = jnp.maximum(jnp.max(s, axis=0, keepdims=True), M_FLOOR)
    p = jnp.exp2(s - m).astype(BF16)
    acc = jnp.dot(vwt_ref[:, pl.ds(k0, span)], p, preferred_element_type=F32)
    ow = acc[:NSA_HEAD_DIM] / acc[NSA_HEAD_DIM:NSA_HEAD_DIM + 1]
    g = jax.nn.sigmoid(gl_ref[...])
    o_t = g[0:1] * oc_ref[...] + g[1:2] * os_ref[...] + g[2:3] * ow
    o_ref[...] = _lanes_to_heads(o_t, NSA_HPG).astype(BF16)


def _nsa_window(qt, kw_aug, vwt, w0, dt, oc, osel, gl):
    B, G, nQ = qt.shape[:3]
    TP = kw_aug.shape[2]
    R = vwt.shape[2]
    L = NSA_HPG * TQ
    W = NSA_HPG * NSA_HEAD_DIM
    tile = lambda r: pl.BlockSpec((None, None, None, r, L), lambda b, g, i: (b, g, i, 0, 0))
    return pl.pallas_call(
        _nsa_win_body,
        grid=(B, G, nQ),
        in_specs=[tile(LANE),
                  pl.BlockSpec((None, None, TP, LANE), lambda b, g, i: (b, g, 0, 0)),
                  pl.BlockSpec((None, None, R, TP), lambda b, g, i: (b, g, 0, 0)),
                  pl.BlockSpec((TQ, L), lambda b, g, i: (0, 0)),
                  pl.BlockSpec((None, 5, TQ, L), lambda b, g, i: (g, 0, 0, 0)),
                  tile(NSA_HEAD_DIM), tile(NSA_HEAD_DIM), tile(8)],
        out_specs=pl.BlockSpec((None, TQ, W), lambda b, g, i: (b, i, g)),
        out_shape=jax.ShapeDtypeStruct((B, nQ * TQ, G * W), BF16),
        compiler_params=_cparams(("arbitrary",) * 3),
        name="nsa_window",
    )(qt, kw_aug, vwt, w0, dt, oc, osel, gl)


INT_MIN = -2 ** 31
CK = 2 * NK
KEY_BITS = 16


def _dsa_body(qd_ref, qi_ref, wt_ref, ki_ref, ckv_ref, ckvt_ref, wuk_ref, wuvt_ref, dt_ref,
              o_ref, sc_ref, qb_ref, m_ref, acc_ref, sa_ref, sb_ref, *, k_top):
    i = pl.program_id(1)
    H = DSA_HEADS
    L = H * TQ
    last = i // (NK // TQ)
    n_steps = last + 1
    t_row = i * TQ + lax.broadcasted_iota(I32, (1, TQ), 1)

    qit = jnp.concatenate([_heads_to_lanes(qi_ref[...].astype(F32), IDX_HEADS),
                           jnp.zeros((LANE - IDX_DIM, L), F32)], axis=0).astype(BF16)
    wt = wt_ref[...]

    def idx_scores(j, buf):
        k0 = pl.multiple_of(j * NK, NK)
        buf[...] = jnp.dot(ki_ref[pl.ds(k0, NK), :], qit, preferred_element_type=F32)

    def idx_reduce(j, buf):
        k0 = pl.multiple_of(j * NK, NK)
        r = jnp.maximum(buf[...], 0.0) * wt
        sc = r[:, 0:TQ]
        for h in range(1, H):
            sc = sc + r[:, h * TQ:(h + 1) * TQ]
        bits = pltpu.bitcast(sc, I32)
        ik = jnp.where(sc == 0.0, 0, bits ^ ((bits >> 31) & 0x7FFFFFFF))
        key = k0 + lax.broadcasted_iota(I32, (NK, 1), 0)
        sc_ref[pl.ds(k0, NK), :] = jnp.where(key <= t_row, jnp.maximum(ik, INT_MIN + 1), INT_MIN)

    _pipeline(0, n_steps, idx_scores, idx_reduce, sa_ref, sb_ref)

    @pl.when(n_steps % 2 == 1)
    def _():
        sc_ref[pl.ds(pl.multiple_of(n_steps * NK, NK), NK), :] = jnp.full((NK, TQ), INT_MIN, I32)

    n_chunks = (n_steps + 1) // 2

    def count(pred):
        def cbody(j, cnt):
            k0 = pl.multiple_of(j * CK, CK)
            ind = jnp.where(pred(sc_ref[pl.ds(k0, CK), :], k0), 1.0, 0.0)
            parts = [ind[r * 8:(r + 1) * 8] for r in range(CK // 8)]
            while len(parts) > 1:
                parts = [a + b for a, b in zip(parts[0::2], parts[1::2])]
            return cnt + parts[0]
        cnt8 = lax.fori_loop(0, n_chunks, cbody, jnp.zeros((8, TQ), F32))
        return jnp.sum(cnt8, axis=0, keepdims=True)

    def any_lane(cond):
        return jnp.max(jnp.where(cond, 1.0, 0.0)) > 0.0

    kf = float(k_top)
    need = t_row + 1 > k_top

    def bit_body(b, c):
        tu, ct = c
        cand = tu | jnp.left_shift(jnp.int32(1), 31 - b)
        thr_c = cand ^ INT_MIN
        cnt = count(lambda blk, k0: blk >= thr_c)
        take = cnt >= kf
        return jnp.where(take, cand, tu), jnp.where(take, cnt, ct)

    tu, ct = lax.fori_loop(0, 32, bit_body, (jnp.zeros((1, TQ), I32), (t_row + 1).astype(F32)))
    thr = jnp.maximum(tu ^ INT_MIN, INT_MIN + 1)

    @pl.when(any_lane(need & (ct > kf)))
    def _():
        keep = kf - count(lambda blk, k0: blk > thr)

        def tie_count(bound):
            def pred(blk, k0):
                key = k0 + lax.broadcasted_iota(I32, (CK, 1), 0)
                return jnp.where(key < bound, blk, INT_MIN) == thr
            return count(pred)

        def pos_body(b, pos):
            cand = pos | jnp.left_shift(jnp.int32(1), KEY_BITS - 1 - b)
            return jnp.where(tie_count(cand) < keep, cand, pos)

        pos = lax.fori_loop(0, KEY_BITS, pos_body, jnp.zeros((1, TQ), I32))

        def fix(j, c):
            k0 = pl.multiple_of(j * CK, CK)
            blk = sc_ref[pl.ds(k0, CK), :]
            key = k0 + lax.broadcasted_iota(I32, (CK, 1), 0)
            drop = jnp.where(key > pos, blk, INT_MIN) == thr
            sc_ref[pl.ds(k0, CK), :] = jnp.where(drop, INT_MIN, blk)
            return c

        lax.fori_loop(0, n_chunks, fix, 0)

    qdt = _heads_to_lanes(qd_ref[...].astype(F32), H).astype(BF16)
    for h in range(H):
        ql = jnp.dot(wuk_ref[h], qdt[:, h * TQ:(h + 1) * TQ], preferred_element_type=F32)
        qb_ref[0:LANE, h * TQ:(h + 1) * TQ] = (ql * (DSA_HEAD_DIM ** -0.5 * LOG2E)).astype(BF16)
    eye = (lax.broadcasted_iota(I32, (LANE, TQ), 0) == lax.broadcasted_iota(I32, (LANE, TQ), 1))
    eye = jnp.where(eye, 1.0, 0.0).astype(BF16)
    qb_ref[LANE:2 * LANE, :] = jnp.concatenate([eye] * H, axis=1)
    m_ref[...] = jnp.full((1, L), M_FLOOR, F32)
    acc_ref[...] = jnp.zeros(acc_ref.shape, F32)
    alast = i // (AK // TQ)
    n_far = jnp.maximum(alast - 1, 0)

    def keys(j):
        k0 = pl.multiple_of(j * AK, AK)
        nb = jnp.where(sc_ref[pl.ds(k0, AK), :] >= thr, 0.0, NEG_INF).astype(BF16)
        return jnp.concatenate([ckv_ref[pl.ds(k0, AK), :], nb], axis=1)

    def value(j):
        return ckvt_ref[:, pl.ds(pl.multiple_of(j * AK, AK), AK)]

    _attend(0, n_far, keys, qb_ref, value, m_ref, acc_ref, sa_ref, sb_ref)
    _attend(n_far, alast + 1, keys, qb_ref, value, m_ref, acc_ref, sa_ref, sb_ref,
            near_fn=functools.partial(_near_add, dt_ref, i))
    acc = acc_ref[...]
    olat = (acc[:DSA_KV_LATENT] / acc[DSA_KV_LATENT:DSA_KV_LATENT + 1]).astype(BF16)
    o_t = jnp.concatenate([jnp.dot(wuvt_ref[h], olat[:, h * TQ:(h + 1) * TQ],
                                   preferred_element_type=F32) for h in range(H)], axis=1)
    o_ref[...] = _lanes_to_heads(o_t, H).astype(BF16)


def _dsa(qd, qi, wt, ki, ckv, ckvt, wuk, wuvt, dt, k_top):
    B, T = ki.shape[:2]
    nQ = T // TQ
    H = DSA_HEADS
    L = H * TQ
    W = H * DSA_HEAD_DIM
    R = ckvt.shape[1]
    rows = pl.BlockSpec((None, TQ, W), lambda b, i: (b, i, 0))
    return pl.pallas_call(
        functools.partial(_dsa_body, k_top=k_top),
        grid=(B, nQ),
        in_specs=[rows, rows, pl.BlockSpec((None, None, 1, L), lambda b, i: (b, i, 0, 0)),
                  pl.BlockSpec((None, T, LANE), lambda b, i: (b, 0, 0)),
                  pl.BlockSpec((None, T, DSA_KV_LATENT), lambda b, i: (b, 0, 0)),
                  pl.BlockSpec((None, R, T), lambda b, i: (b, 0, 0)),
                  pl.BlockSpec((H, DSA_KV_LATENT, DSA_HEAD_DIM), lambda b, i: (0, 0, 0)),
                  pl.BlockSpec((H, DSA_HEAD_DIM, DSA_KV_LATENT), lambda b, i: (0, 0, 0)),
                  pl.BlockSpec((5, TQ, L), lambda b, i: (0, 0, 0))],
        out_specs=rows,
        out_shape=jax.ShapeDtypeStruct((B, T, W), BF16),
        scratch_shapes=[pltpu.VMEM((T, TQ), I32),
                        pltpu.VMEM((2 * LANE, L), BF16),
                        pltpu.VMEM((1, L), F32),
                        pltpu.VMEM((R, L), F32),
                        pltpu.VMEM((AK, L), F32),
                        pltpu.VMEM((AK, L), F32)],
        compiler_params=_cparams(("arbitrary",) * 2),
        name="dsa_attention",
    )(qd, qi, wt, ki, ckv, ckvt, wuk, wuvt, dt)


def _token_mixers(qn, kvn, qd, ckv, qi, small, ki, rel_bias, cmp_pe, cmp_w1, cmp_b1, cmp_w2,
                  dsa_w_uk, dsa_w_uv):
    B, T = qn.shape[:2]
    G, HPG, DH = NSA_KV_GROUPS, NSA_HPG, NSA_HEAD_DIM
    nQ = T // TQ
    n_blk = T // SLC_BLOCK
    L = HPG * TQ
    assert T % NK == 0 and n_blk <= LANE
    kv6 = kvn.reshape(B, T, 6, G, DH)

    tab_nsa = rel_bias[:, :NSA_HEADS]
    dt_nsa = _near_tables(tab_nsa).reshape(5, TQ, G, L).transpose(2, 0, 1, 3)
    dt_dsa = _near_tables(rel_bias[:, NSA_HEADS:])

    kvc = _compress(kv6, cmp_pe, cmp_w1, cmp_b1, cmp_w2)
    M = T // CMP_STRIDE
    MP = M + CMP_PAD
    kc = kvc[:, 0].astype(BF16)
    flag = jnp.concatenate([jnp.full((CMP_PAD, 1), NEG_INF, F32), jnp.zeros((M, 1), F32)]).astype(BF16)
    kc_aug = jnp.concatenate(
        [jnp.pad(kc, ((0, 0), (0, 0), (CMP_PAD, 0), (0, 0))),
         jnp.broadcast_to(flag, (B, G, MP, 1)),
         jnp.zeros((B, G, MP, LANE - DH - 1), BF16)], axis=-1)
    vct = jnp.pad(kvc[:, 1].astype(BF16).transpose(0, 1, 3, 2),
                  ((0, 0), (0, 0), (0, LANE - DH), (CMP_PAD, 0)))
    mrow = jnp.arange(CMP_NEAR)[:, None]
    qcol = jnp.arange(TQ)[None, :]
    ct = _delta(tab_nsa, qcol - CMP_STRIDE * mrow + (2 * TQ - CMP_BLOCK + 1))
    ct = ct.reshape(CMP_NEAR, TQ, G, HPG).transpose(2, 0, 3, 1).reshape(G, CMP_NEAR, L)
    n_cmp = (T - CMP_BLOCK) // CMP_STRIDE + 1
    cs = jnp.arange(M) * CMP_STRIDE
    ss = jnp.arange(LANE) * SLC_BLOCK
    ovl = (jnp.clip(jnp.minimum(cs[None, :] + CMP_BLOCK, ss[:, None] + SLC_BLOCK)
                    - jnp.maximum(cs[None, :], ss[:, None]), 0, None).astype(F32) / CMP_BLOCK)
    ovl = jnp.where((jnp.arange(M)[None, :] < n_cmp) & (jnp.arange(LANE)[:, None] < n_blk), ovl, 0.0)
    ovl = jnp.pad(ovl, ((0, 0), (CMP_PAD, 0))).astype(BF16)
    oc, selneg, qt = _nsa_compressed(qn, kc_aug, vct, ct, ovl, n_blk)

    ks = kv6[:, :, 2].transpose(0, 2, 1, 3)
    et = (jnp.arange(T)[:, None] // SLC_BLOCK == jnp.arange(LANE)[None, :]).astype(BF16)
    ks_aug = jnp.concatenate([ks, jnp.zeros((B, G, T, LANE - DH), BF16),
                              jnp.broadcast_to(et, (B, G, T, LANE))], axis=-1)
    vs = kv6[:, :, 3].transpose(0, 2, 3, 1)
    vst = jnp.concatenate([vs, jnp.ones((B, G, 1, T), BF16),
                           jnp.zeros((B, G, V_ROWS - DH - 1, T), BF16)], axis=2)
    osel = _nsa_selected(qt, selneg, ks_aug, vst, dt_nsa)

    TP = T + WINDOW
    kw = jnp.pad(kv6[:, :, 4].transpose(0, 2, 1, 3), ((0, 0), (0, 0), (WINDOW, 0), (0, 0)))
    wflag = jnp.concatenate([jnp.full((WINDOW, 1), NEG_INF, F32), jnp.zeros((T, 1), F32)]).astype(BF16)
    kw_aug = jnp.concatenate([kw, jnp.broadcast_to(wflag, (B, G, TP, 1)),
                              jnp.zeros((B, G, TP, LANE - DH - 1), BF16)], axis=-1)
    vw = jnp.pad(kv6[:, :, 5].transpose(0, 2, 3, 1), ((0, 0), (0, 0), (0, 0), (WINDOW, 0)))
    vwt = jnp.concatenate([vw, jnp.ones((B, G, 1, TP), BF16),
                           jnp.zeros((B, G, V_ROWS - DH - 1, TP), BF16)], axis=2)
    u = jnp.arange(TQ)[:, None]
    w0 = jnp.tile(jnp.where(u > jnp.arange(TQ)[None, :], 0.0, NEG_INF).astype(F32), (1, HPG))
    gl = small[..., 0:3 * NSA_HEADS].reshape(B, nQ, TQ, G, HPG, 3).transpose(0, 3, 1, 5, 4, 2)
    gl = jnp.pad(gl.reshape(B, G, nQ, 3, L), ((0, 0), (0, 0), (0, 0), (0, 5), (0, 0)))
    o_nsa = _nsa_window(qt, kw_aug, vwt, w0, dt_nsa, oc, osel, gl)

    H = DSA_HEADS
    w_idx = small[..., 24 + IDX_DIM:24 + IDX_DIM + IDX_HEADS] * IDX_HEADS ** -0.5
    wt = w_idx.reshape(B, nQ, TQ, IDX_HEADS).transpose(0, 1, 3, 2).reshape(B, nQ, 1, IDX_HEADS * TQ)
    ckvt = jnp.concatenate([ckv.transpose(0, 2, 1), jnp.ones((B, 1, T), BF16),
                            jnp.zeros((B, 7, T), BF16)], axis=1)
    wuk = dsa_w_uk.transpose(1, 0, 2).astype(BF16)
    wuvt = dsa_w_uv.transpose(1, 2, 0).astype(BF16)
    o_dsa = _dsa(qd, qi, wt, ki, ckv, ckvt, wuk, wuvt, dt_dsa, min(IDX_TOPK_MAX, T // 4))
    return o_nsa, o_dsa


def _rms(v, g):
    return v * lax.rsqrt(jnp.mean(v * v, axis=-1, keepdims=True) + RMS_EPS) * g


HI_MASK = -65536


def _pack_halves(v):
    w = v.shape[1] // 2
    lo = lax.shift_right_logical(pltpu.bitcast(v[:, :w], I32), 16)
    hi = pltpu.bitcast(v[:, w:], I32) & HI_MASK
    return lo | hi


def _unpack_halves(p):
    return pltpu.bitcast(p << 16, F32), pltpu.bitcast(p & HI_MASK, F32)


def _merge_body(on_ref, od_ref, gm_ref, x_ref, wn_ref, wd_ref, wo_ref, gpost_ref, gt_ref,
                gpre_ref, sc_ref, sh_ref, rwt_ref, x1_ref, h2_ref, hpa_ref, hpb_ref, lt_ref):
    D = D_MODEL
    ya = jnp.dot(on_ref[...], wn_ref[...], preferred_element_type=F32)
    yb = jnp.dot(od_ref[...], wd_ref[...], preferred_element_type=F32)
    y = jax.nn.sigmoid(gm_ref[:, 0:D]) * ya + jax.nn.sigmoid(gm_ref[:, D:2 * D]) * yb
    y2 = jnp.dot(y.astype(BF16), wo_ref[...], preferred_element_type=F32)
    x1 = x_ref[...] + gt_ref[...] * _rms(y2, gpost_ref[...])
    x1_ref[...] = x1
    h2 = (_rms(x1, gpre_ref[...]) * (1.0 + sc_ref[...]) + sh_ref[...]).astype(BF16)
    h2_ref[...] = h2
    packed = _pack_halves(h2.astype(F32))
    hpa_ref[...] = packed[:, :SC_WIDTH]
    hpb_ref[...] = packed[:, SC_WIDTH:]
    lt_ref[...] = lax.dot_general(rwt_ref[...], h2, (((1,), (1,)), ((), ())),
                                  preferred_element_type=F32)


def _merge(o_nsa, o_dsa, gm, x, wn, wd, wo, g_post, gt1, g_pre, sc2, sh2, router_w):
    B, T, D = x.shape
    TM = 256
    E = router_w.shape[1]
    row = lambda w: pl.BlockSpec((None, TM, w), lambda b, i: (b, i, 0))
    full = lambda a: pl.BlockSpec(a.shape, lambda b, i: (0,) * a.ndim)
    vecb = pl.BlockSpec((None, 1, D), lambda b, i: (b, 0, 0))
    vec = pl.BlockSpec((1, D), lambda b, i: (0, 0))
    wn, wd, wo = wn.astype(BF16), wd.astype(BF16), wo.astype(BF16)
    rwt = router_w.T.astype(BF16)
    return pl.pallas_call(
        _merge_body,
        grid=(B, T // TM),
        in_specs=[row(o_nsa.shape[-1]), row(o_dsa.shape[-1]), row(2 * D), row(D),
                  full(wn), full(wd), full(wo), vec, vecb, vec, vecb, vecb, full(rwt)],
        out_specs=[row(D), row(D), row(SC_WIDTH), row(SC_WIDTH),
                   pl.BlockSpec((None, E, TM), lambda b, i: (b, 0, i))],
        out_shape=[jax.ShapeDtypeStruct((B, T, D), F32),
                   jax.ShapeDtypeStruct((B, T, D), BF16),
                   jax.ShapeDtypeStruct((B, T, SC_WIDTH), I32),
                   jax.ShapeDtypeStruct((B, T, SC_WIDTH), I32),
                   jax.ShapeDtypeStruct((B, E, T), F32)],
        compiler_params=_cparams(("arbitrary",) * 2),
        name="merge_postnorm",
    )(o_nsa, o_dsa, gm, x, wn, wd, wo, g_post.reshape(1, D), gt1.reshape(B, 1, D),
      g_pre.reshape(1, D), sc2.reshape(B, 1, D), sh2.reshape(B, 1, D), rwt)


TR = 256
GSZ = N_EXPERTS // N_EXPERT_GROUPS


def _first_max(v, idx, big):
    mx = jnp.max(v, axis=0, keepdims=True)
    return jnp.min(jnp.where(v == mx, idx, big), axis=0, keepdims=True), mx


def _route_body(lt_ref, bias_ref, eid_ref, rk_ref, wt_ref, cnt_ref, carry_ref):
    first_step = (pl.program_id(0) == 0) & (pl.program_id(1) == 0)

    @pl.when(first_step)
    def _():
        carry_ref[...] = jnp.zeros(carry_ref.shape, F32)

    s = jax.nn.sigmoid(lt_ref[...])
    sel = s + bias_ref[:, 0:1]
    grow = lax.broadcasted_iota(I32, (GSZ, TR), 0)
    gs = []
    for g in range(N_EXPERT_GROUPS):
        blk = sel[g * GSZ:(g + 1) * GSZ]
        f1, m1 = _first_max(blk, grow, GSZ)
        m2 = jnp.max(jnp.where(grow == f1, KNOCK, blk), axis=0, keepdims=True)
        gs.append(m1 + m2)
    gsc = jnp.concatenate(gs, axis=0)
    gidx = lax.broadcasted_iota(I32, (N_EXPERT_GROUPS, TR), 0)
    gsel = jnp.zeros((N_EXPERT_GROUPS, TR), F32)
    for _ in range(TOPK_GROUPS):
        f, _mx = _first_max(gsc, gidx, N_EXPERT_GROUPS)
        pick = gidx == f
        gsel = jnp.where(pick, 1.0, gsel)
        gsc = jnp.where(pick, KNOCK, gsc)
    cur = jnp.concatenate(
        [jnp.where(gsel[g:g + 1] > 0.0, sel[g * GSZ:(g + 1) * GSZ], NEG_INF)
         for g in range(N_EXPERT_GROUPS)], axis=0)
    erow = lax.broadcasted_iota(I32, (N_EXPERTS, TR), 0)
    member = jnp.zeros((N_EXPERTS, TR), F32)
    picks, eids, ws = [], [], []
    for _ in range(TOP_K):
        f, _mx = _first_max(cur, erow, N_EXPERTS)
        pick = erow == f
        picks.append(pick)
        eids.append(f)
        ws.append(jnp.sum(jnp.where(pick, s, 0.0), axis=0, keepdims=True))
        member = jnp.where(pick, 1.0, member)
        cur = jnp.where(pick, KNOCK, cur)
    wsum = ws[0]
    for k in range(1, TOP_K):
        wsum = wsum + ws[k]
    mb = member.astype(BF16)
    upper = (lax.broadcasted_iota(I32, (TR, TR), 0) < lax.broadcasted_iota(I32, (TR, TR), 1))
    upper = jnp.where(upper, 1.0, 0.0).astype(BF16)
    rank = jnp.dot(mb, upper, preferred_element_type=F32) + carry_ref[:, 0:1]
    eid_ref[...] = jnp.concatenate(eids, axis=0)
    rk_ref[...] = jnp.concatenate(
        [jnp.sum(jnp.where(p, rank, 0.0), axis=0, keepdims=True) for p in picks], axis=0)
    wt_ref[...] = jnp.concatenate([w / wsum * ROUTED_SCALE for w in ws], axis=0)
    carry_ref[...] = carry_ref[...] + jnp.dot(mb, jnp.ones((TR, LANE), BF16),
                                              preferred_element_type=F32)
    cnt_ref[...] = carry_ref[...]


def _route(lt, router_bias):
    B, E, T = lt.shape
    out8 = lambda: pl.BlockSpec((None, TOP_K, TR), lambda b, i: (b, 0, i))
    return pl.pallas_call(
        _route_body,
        grid=(B, T // TR),
        in_specs=[pl.BlockSpec((None, E, TR), lambda b, i: (b, 0, i)),
                  pl.BlockSpec((E, 1), lambda b, i: (0, 0))],
        out_specs=[out8(), out8(), out8(), pl.BlockSpec((E, LANE), lambda b, i: (0, 0))],
        out_shape=[jax.ShapeDtypeStruct((B, TOP_K, T), I32),
                   jax.ShapeDtypeStruct((B, TOP_K, T), F32),
                   jax.ShapeDtypeStruct((B, TOP_K, T), F32),
                   jax.ShapeDtypeStruct((E, LANE), F32)],
        scratch_shapes=[pltpu.VMEM((E, LANE), F32)],
        compiler_params=_cparams(("arbitrary",) * 2),
        name="moe_router",
    )(lt, router_bias.reshape(E, 1))


BM = 256


def _gmm_body(be_ref, nv_ref, ea_ref, eb_ref, xa_ref, xb_ref, wga_ref, wua_ref, wda_ref,
              wgb_ref, wub_ref, wdb_ref, ya_ref, yb_ref, wgb, wub, wdb):
    b = pl.program_id(0)
    valid = nv_ref[b]

    @pl.when(valid > 0)
    def _():
        prev = be_ref[jnp.maximum(b - 1, 0)]
        changed = (b == 0) | (be_ref[b] != prev)

        @pl.when(changed & (ea_ref[b] == be_ref[b]))
        def _():
            wgb[...] = wga_ref[...].astype(BF16)
            wub[...] = wua_ref[...].astype(BF16)
            wdb[...] = wda_ref[...].astype(BF16)

        @pl.when(changed & (ea_ref[b] != be_ref[b]))
        def _():
            wgb[...] = wgb_ref[...].astype(BF16)
            wub[...] = wub_ref[...].astype(BF16)
            wdb[...] = wdb_ref[...].astype(BF16)

        live = lax.broadcasted_iota(I32, (BM, 1), 0) < valid
        lo_a, hi_a = _unpack_halves(jnp.where(live, xa_ref[...], 0))
        lo_b, hi_b = _unpack_halves(jnp.where(live, xb_ref[...], 0))
        xb = jnp.concatenate([lo_a, lo_b, hi_a, hi_b], axis=1).astype(BF16)
        gt = jnp.dot(xb, wgb[...], preferred_element_type=F32)
        up = jnp.dot(xb, wub[...], preferred_element_type=F32)
        a = (gt * jax.nn.sigmoid(gt) * up).astype(BF16)
        y = jnp.dot(a, wdb[...], preferred_element_type=F32)
        packed = _pack_halves(y.astype(BF16).astype(F32))
        ya_ref[...] = packed[:, :SC_WIDTH]
        yb_ref[...] = packed[:, SC_WIDTH:]

    @pl.when(valid == 0)
    def _():
        ya_ref[...] = jnp.zeros(ya_ref.shape, I32)
        yb_ref[...] = jnp.zeros(yb_ref.shape, I32)


def _grouped_mlp(xs_a, xs_b, blk_e, n_valid, exp_a, exp_b, w_g, w_u, w_d):
    P = xs_a.shape[0]
    D, De = w_g.shape[-2:]
    nb = P // BM
    rows = pl.BlockSpec((BM, SC_WIDTH), lambda b, be, nv, ea, eb: (b, 0))
    sel_a = lambda b, be, nv, ea, eb: (ea[b], 0, 0)
    sel_b = lambda b, be, nv, ea, eb: (eb[b], 0, 0)
    grid_spec = pltpu.PrefetchScalarGridSpec(
        num_scalar_prefetch=4,
        grid=(nb,),
        in_specs=[rows, rows,
                  pl.BlockSpec((None, D, De), sel_a),
                  pl.BlockSpec((None, D, De), sel_a),
                  pl.BlockSpec((None, De, D), sel_a),
                  pl.BlockSpec((None, D, De), sel_b),
                  pl.BlockSpec((None, D, De), sel_b),
                  pl.BlockSpec((None, De, D), sel_b)],
        out_specs=[rows, rows],
        scratch_shapes=[pltpu.VMEM((D, De), BF16), pltpu.VMEM((D, De), BF16),
                        pltpu.VMEM((De, D), BF16)])
    out = jax.ShapeDtypeStruct((P, SC_WIDTH), I32)
    return pl.pallas_call(
        _gmm_body,
        grid_spec=grid_spec,
        out_shape=[out, out],
        compiler_params=_cparams(("arbitrary",)),
        name="moe_experts",
    )(blk_e, n_valid, exp_a, exp_b, xs_a, xs_b, w_g, w_u, w_d, w_g, w_u, w_d)


def _sc_mesh():
    return plsc.VectorSubcoreMesh(core_axis_name="c", subcore_axis_name="s")


def _sc_gather(x, idx):
    n = idx.shape[0]

    @pl.kernel(out_type=jax.ShapeDtypeStruct((n, SC_WIDTH), x.dtype), mesh=_sc_mesh())
    def gather(x_hbm, i_hbm, o_hbm):
        def body(i_vmem, o_vmem):
            pltpu.sync_copy(x_hbm.at[i_vmem.at[0]], o_vmem)

        pltpu.emit_pipeline(
            body, grid=(n // SC_WINDOW,),
            in_specs=[pl.BlockSpec((1, SC_WINDOW), lambda i: (0, i))],
            out_specs=[pl.BlockSpec((SC_WINDOW, SC_WIDTH), lambda i: (i, 0))],
            core_axis_name=("c", "s"),
            dimension_semantics=(pltpu.PARALLEL,),
        )(i_hbm, o_hbm)

    return gather(x, idx.reshape(1, n))


def _sc_dispatch(x, slot, n_out):
    B, T, W = x.shape
    K = slot.shape[1]
    tb = T // SC_WINDOW
    n = B * K * T

    @pl.kernel(out_type=jax.ShapeDtypeStruct((n_out, W), x.dtype), mesh=_sc_mesh(), scratch_types=[])
    def scatter(x_hbm, i_hbm, o_hbm):
        def body(x_vmem, i_vmem):
            pltpu.sync_copy(x_vmem, o_hbm.at[i_vmem.at[0]])

        pltpu.emit_pipeline(
            body, grid=(n // SC_WINDOW,),
            in_specs=[pl.BlockSpec((SC_WINDOW, W), lambda i: ((i // (K * tb)) * tb + i % tb, 0)),
                      pl.BlockSpec((1, SC_WINDOW), lambda i: (0, i))],
            out_specs=[],
            core_axis_name=("c", "s"),
            dimension_semantics=(pltpu.PARALLEL,),
        )(x_hbm, i_hbm)

    return scatter(x.reshape(B * T, W), slot.reshape(1, n))


def _final_body(ya_ref, yb_ref, w_ref, h_ref, x1_ref, wg_ref, wu_ref, wd_ref, g_ref, gt_ref, o_ref):
    hb = h_ref[...]
    gt = jnp.dot(hb, wg_ref[...], preferred_element_type=F32)
    up = jnp.dot(hb, wu_ref[...], preferred_element_type=F32)
    a = (gt * jax.nn.sigmoid(gt) * up).astype(BF16)
    shared = jnp.dot(a, wd_ref[...], preferred_element_type=F32)
    w = w_ref[...]
    parts = None
    for k in range(TOP_K):
        wk = w[:, k:k + 1]
        lo_a, hi_a = _unpack_halves(ya_ref[k])
        lo_b, hi_b = _unpack_halves(yb_ref[k])
        terms = [lo_a * wk, lo_b * wk, hi_a * wk, hi_b * wk]
        parts = terms if parts is None else [p + t for p, t in zip(parts, terms)]
    y = jnp.concatenate(parts, axis=1) + shared
    o_ref[...] = x1_ref[...] + gt_ref[...] * _rms(y, g_ref[...])


def _final(yg_a, yg_b, w, h2, x1, ws_g, ws_u, ws_d, g_post, gt2):
    B, T, D = x1.shape
    TM = 256 if T % 256 == 0 else TQ
    row = lambda wd: pl.BlockSpec((None, TM, wd), lambda b, i: (b, i, 0))
    krows = pl.BlockSpec((None, TOP_K, TM, SC_WIDTH), lambda b, i: (b, 0, i, 0))
    full = lambda a: pl.BlockSpec(a.shape, lambda b, i: (0,) * a.ndim)
    ws_g, ws_u, ws_d = ws_g.astype(BF16), ws_u.astype(BF16), ws_d.astype(BF16)
    return pl.pallas_call(
        _final_body,
        grid=(B, T // TM),
        in_specs=[krows, krows, row(TOP_K), row(D), row(D),
                  full(ws_g), full(ws_u), full(ws_d),
                  pl.BlockSpec((1, D), lambda b, i: (0, 0)),
                  pl.BlockSpec((None, 1, D), lambda b, i: (b, 0, 0))],
        out_specs=row(D),
        out_shape=jax.ShapeDtypeStruct((B, T, D), F32),
        compiler_params=_cparams(("arbitrary",) * 2),
        name="shared_final",
    )(yg_a, yg_b, w, h2, x1, ws_g, ws_u, ws_d, g_post.reshape(1, D), gt2.reshape(B, 1, D))


def _slot_body(eid_ref, rk_ref, ps_ref, o_ref):
    erow = lax.broadcasted_iota(I32, (N_EXPERTS, TR), 0)
    ps = ps_ref[:, 0:1]
    rows = []
    for k in range(TOP_K):
        base = jnp.sum(jnp.where(erow == eid_ref[k:k + 1, :], ps, 0.0), axis=0, keepdims=True)
        rows.append(base + rk_ref[k:k + 1, :])
    o_ref[...] = jnp.concatenate(rows, axis=0).astype(I32)


def _slots(eid, rk, pstart):
    B, K, T = eid.shape
    blk = pl.BlockSpec((None, K, TR), lambda b, i: (b, 0, i))
    return pl.pallas_call(
        _slot_body,
        grid=(B, T // TR),
        in_specs=[blk, blk, pl.BlockSpec((N_EXPERTS, 1), lambda b, i: (0, 0))],
        out_specs=blk,
        out_shape=jax.ShapeDtypeStruct((B, K, T), I32),
        compiler_params=_cparams(("arbitrary",) * 2),
        name="moe_slots",
    )(eid, rk, pstart.astype(F32).reshape(N_EXPERTS, 1))


def _moe(h2, hp_a, hp_b, lt, x1, router_bias, w_g, w_u, w_d, ws_g, ws_u, ws_d, g_post, gt2):
    B, T, D = x1.shape
    N = B * T
    eid, rk, wts, cnt = _route(lt, router_bias)
    counts = cnt[:, 0].astype(I32)
    padded = (counts + BM - 1) // BM * BM
    pend = jnp.cumsum(padded)
    pstart = pend - padded
    nb = (N * TOP_K + N_EXPERTS * (BM - 1) + BM - 1) // BM
    P = nb * BM
    row0 = jnp.arange(nb, dtype=I32) * BM
    blk_e = jnp.minimum(jnp.sum(pend[None, :] <= row0[:, None], axis=1), N_EXPERTS - 1).astype(I32)
    n_valid = jnp.clip(pstart[blk_e] + counts[blk_e] - row0, 0, BM).astype(I32)
    active = counts > 0
    rank = jnp.cumsum(active.astype(I32)) - 1
    n_act = rank[-1] + 1
    eids = jnp.arange(N_EXPERTS, dtype=I32)
    act_list = jnp.sum(jnp.where((rank[None, :] == eids[:, None]) & active[None, :], eids[None, :], 0),
                       axis=1)
    pos = jnp.where(n_valid > 0, rank[blk_e], n_act - 1)
    exp_a = act_list[jnp.minimum(pos + pos % 2, n_act - 1)].astype(I32)
    exp_b = act_list[jnp.minimum(pos + 1 - pos % 2, n_act - 1)].astype(I32)
    slot = _slots(eid, rk, pstart)
    xs_a = _sc_dispatch(hp_a, slot, P)
    xs_b = _sc_dispatch(hp_b, slot, P)
    ys_a, ys_b = _grouped_mlp(xs_a, xs_b, blk_e, n_valid, exp_a, exp_b, w_g, w_u, w_d)
    flat = slot.reshape(N * TOP_K)
    yg_a = _sc_gather(ys_a, flat).reshape(B, TOP_K, T, SC_WIDTH)
    yg_b = _sc_gather(ys_b, flat).reshape(B, TOP_K, T, SC_WIDTH)
    return _final(yg_a, yg_b, wts.transpose(0, 2, 1), h2, x1, ws_g, ws_u, ws_d, g_post, gt2)


def kernel(x, c, w_ada, b_ada, g_pre_mix, g_post_mix, g_pre_ffn, g_post_ffn, w_in, rel_bias, cmp_pe, cmp_w1, cmp_b1, cmp_w2, dsa_kv_norm, dsa_w_uk, dsa_w_uv, w_branch_nsa, w_branch_dsa, w_out, router_w, router_bias, w_exp_gate, w_exp_up, w_exp_down, w_sh_gate, w_sh_up, w_sh_down):
    depth = w_ada.shape[0]
    D = D_MODEL
    for l in range(depth):
        mod = _adaln(c, w_ada[l], b_ada[l])
        sh1, sc1, gt1, sh2, sc2, gt2 = [mod[:, k * D:(k + 1) * D] for k in range(N_MOD)]
        qn, kvn, qd, ckv, qi, small, gm, ki = _input_proj(x, g_pre_mix[l], sc1, sh1, w_in[l],
                                                          dsa_kv_norm[l])
        o_nsa, o_dsa = _token_mixers(qn, kvn, qd, ckv, qi, small, ki, rel_bias, cmp_pe[l], cmp_w1[l],
                                     cmp_b1[l], cmp_w2[l], dsa_w_uk[l], dsa_w_uv[l])
        x1, h2, hp_a, hp_b, lt = _merge(o_nsa, o_dsa, gm, x,
                                        w_branch_nsa[l], w_branch_dsa[l], w_out[l], g_post_mix[l],
                                        gt1, g_pre_ffn[l], sc2, sh2, router_w[l])
        x = _moe(h2, hp_a, hp_b, lt, x1, router_bias[l], w_exp_gate[l], w_exp_up[l],
                 w_exp_down[l], w_sh_gate[l], w_sh_up[l], w_sh_down[l], g_post_ffn[l], gt2)
    return x
```

```python
import functools
import math

import jax
import jax.numpy as jnp
import numpy as np
from jax import lax
from jax.experimental import pallas as pl
from jax.experimental.pallas import tpu as pltpu
from jax.experimental.pallas import tpu_sc as plsc

F32 = jnp.float32
BF16 = jnp.bfloat16
I32 = jnp.int32

D_MODEL = 1024
N_MOD = 6
NSA_HEADS = 8
NSA_KV_GROUPS = 2
NSA_HPG = NSA_HEADS // NSA_KV_GROUPS
NSA_HEAD_DIM = 64
CMP_BLOCK = 32
CMP_STRIDE = 16
CMP_HIDDEN = 128
SLC_BLOCK = 64
SLC_TOPN = 16
WINDOW = 512
FORCE_BONUS = 1e4
DSA_HEADS = 8
DSA_HEAD_DIM = 64
DSA_KV_LATENT = 128
IDX_HEADS = 8
IDX_DIM = 64
IDX_TOPK_MAX = 256
REL_BUCKETS = 32
REL_MAX_DISTANCE = 128
N_EXPERTS = 256
N_EXPERT_GROUPS = 8
TOPK_GROUPS = 4
TOP_K = 8
D_EXPERT = 256
ROUTED_SCALE = 2.5
RMS_EPS = 1e-6
NEG_INF = -1e30
LOG2E = math.log2(math.e)

LANE = 128
TQ = 128
NK = 256
M_FLOOR = -1e29
KNOCK = -3e38
CMP_PAD = 16
CMP_NEAR = 24
V_ROWS = 80
VMEM_LIMIT = 56 * 1024 * 1024
SC_WINDOW = 128
SC_WIDTH = 256

PROJ_WIDTHS = (512, 768, 24, 512, 128, 512, 64, 8, 2048)
PROJ_OFFS = tuple(int(v) for v in np.cumsum((0,) + PROJ_WIDTHS))


def _cparams(sem):
    return pltpu.CompilerParams(dimension_semantics=sem, vmem_limit_bytes=VMEM_LIMIT)


def _mod_body(c_ref, w_ref, b_ref, o_ref):
    c = c_ref[...]
    s = c * jax.nn.sigmoid(c)
    o_ref[...] = jnp.dot(s, w_ref[...], preferred_element_type=F32) + b_ref[...]


def _adaln(c, w_ada, b_ada):
    B, D = c.shape
    E = w_ada.shape[1]
    cp = jnp.zeros((8, D), F32).at[:B].set(c)
    out = pl.pallas_call(
        _mod_body,
        grid=(E // D,),
        in_specs=[pl.BlockSpec((8, D), lambda j: (0, 0)),
                  pl.BlockSpec((D, D), lambda j: (0, j)),
                  pl.BlockSpec((1, D), lambda j: (0, j))],
        out_specs=pl.BlockSpec((8, D), lambda j: (0, j)),
        out_shape=jax.ShapeDtypeStruct((8, E), F32),
        compiler_params=_cparams(("arbitrary",)),
        name="adaln_mod",
    )(cp, w_ada, b_ada.reshape(1, E))
    return out[:B]


PW = (512, 768, 512, 128, 512, 128, 2048, 128)
PO = tuple(int(v) for v in np.cumsum((0,) + PW))


def _heads_to_lanes(x, nh):
    cols = []
    for p in range(nh // 2):
        t = x[:, p * LANE:(p + 1) * LANE].T
        cols += [t[0:NSA_HEAD_DIM], t[NSA_HEAD_DIM:LANE]]
    return jnp.concatenate(cols, axis=1)


def _lanes_to_heads(y, nh):
    outs = []
    for p in range(nh // 2):
        pair = jnp.concatenate([y[:, (2 * p) * LANE:(2 * p + 1) * LANE],
                                y[:, (2 * p + 1) * LANE:(2 * p + 2) * LANE]], axis=0)
        outs.append(pair.T)
    return jnp.concatenate(outs, axis=1)


def _proj_body(x_ref, g_ref, sc_ref, sh_ref, w_ref, kn_ref,
               qn_ref, kv_ref, qd_ref, ckv_ref, qi_ref, sm_ref, gm_ref, ki_ref):
    x = x_ref[...]
    ms = jnp.mean(x * x, axis=-1, keepdims=True)
    h = x * lax.rsqrt(ms + RMS_EPS) * g_ref[...]
    h = h * (1.0 + sc_ref[...]) + sh_ref[...]
    hb = h.astype(BF16)

    def mm(k):
        return jnp.dot(hb, w_ref[:, PO[k]:PO[k + 1]], preferred_element_type=F32)

    qn_ref[...] = (mm(0) * (NSA_HEAD_DIM ** -0.5 * LOG2E)).astype(BF16)
    kv_ref[...] = mm(1).astype(BF16)
    qd_ref[...] = mm(2).astype(BF16)
    c = mm(3)
    cms = jnp.mean(c * c, axis=-1, keepdims=True)
    ckv_ref[...] = (c * lax.rsqrt(cms + RMS_EPS) * kn_ref[...]).astype(BF16)
    qi_ref[...] = (mm(4) * IDX_DIM ** -0.5).astype(BF16)
    sm_ref[...] = mm(5)
    gm_ref[...] = mm(6)
    ki_ref[...] = mm(7).astype(BF16)


def _input_proj(x, g_pre, sc, sh, w_in, kv_norm):
    B, T, D = x.shape
    TM = 512 if T % 512 == 0 else TQ
    o = PROJ_OFFS
    small = jnp.concatenate([w_in[:, o[2]:o[3]], w_in[:, o[6]:o[7]], w_in[:, o[7]:o[8]],
                             jnp.zeros((D, 32), F32)], axis=1)
    w_r = jnp.concatenate([w_in[:, o[0]:o[1]], w_in[:, o[1]:o[2]], w_in[:, o[3]:o[4]],
                           w_in[:, o[4]:o[5]], w_in[:, o[5]:o[6]], small,
                           w_in[:, o[8]:o[9]], w_in[:, o[6]:o[7]],
                           jnp.zeros((D, LANE - IDX_DIM), F32)], axis=1).astype(BF16)
    row = lambda w: pl.BlockSpec((None, TM, w), lambda b, i: (b, i, 0))
    vec = pl.BlockSpec((None, 1, D), lambda b, i: (b, 0, 0))
    outs = pl.pallas_call(
        _proj_body,
        grid=(B, T // TM),
        in_specs=[row(D),
                  pl.BlockSpec((1, D), lambda b, i: (0, 0)),
                  vec, vec,
                  pl.BlockSpec((D, PO[-1]), lambda b, i: (0, 0)),
                  pl.BlockSpec((1, DSA_KV_LATENT), lambda b, i: (0, 0))],
        out_specs=[row(w) for w in PW],
        out_shape=[jax.ShapeDtypeStruct((B, T, PW[0]), BF16),
                   jax.ShapeDtypeStruct((B, T, PW[1]), BF16),
                   jax.ShapeDtypeStruct((B, T, PW[2]), BF16),
                   jax.ShapeDtypeStruct((B, T, PW[3]), BF16),
                   jax.ShapeDtypeStruct((B, T, PW[4]), BF16),
                   jax.ShapeDtypeStruct((B, T, PW[5]), F32),
                   jax.ShapeDtypeStruct((B, T, PW[6]), F32),
                   jax.ShapeDtypeStruct((B, T, PW[7]), BF16)],
        compiler_params=_cparams(("arbitrary", "arbitrary")),
        name="prenorm_proj",
    )(x, g_pre.reshape(1, D), sc.reshape(B, 1, D), sh.reshape(B, 1, D), w_r,
      kv_norm.reshape(1, DSA_KV_LATENT))
    return outs


def _t5_bucket(dist):
    n = jnp.maximum(dist, 0)
    exact = REL_BUCKETS // 2
    nf = jnp.maximum(n, 1).astype(F32)
    large = exact + (jnp.log(nf / exact) / math.log(REL_MAX_DISTANCE / exact)
                     * (REL_BUCKETS - exact)).astype(I32)
    return jnp.where(n < exact, n, jnp.minimum(large, REL_BUCKETS - 1))


def _delta(tab, dist):
    onehot = (_t5_bucket(dist)[..., None] == jnp.arange(REL_BUCKETS)).astype(F32)
    d = jnp.einsum('...b,bh->...h', onehot, tab - tab[REL_BUCKETS - 1],
                   precision=lax.Precision.HIGHEST)
    return jnp.where((dist >= 0)[..., None], d * LOG2E, NEG_INF)


def _near_tables(tab):
    H = tab.shape[1]
    c = jnp.arange(TQ)[:, None]
    q = jnp.arange(TQ)[None, :]
    tabs = [jnp.full((TQ, TQ, H), NEG_INF, F32)]
    for diff in (0, 1):
        tabs.append(_delta(tab, diff * TQ + q - c))
    tabs += [jnp.zeros((TQ, TQ, H), F32)] * 2
    t = jnp.stack(tabs)
    return t.transpose(0, 1, 3, 2).reshape(5, TQ, H * TQ)


def _cmp_body(r_ref, w1_ref, c_ref, w2_ref, o_ref):
    y = jnp.dot(r_ref[...], w1_ref[...], preferred_element_type=F32)
    M = y.shape[0]
    z = y[:, :CMP_HIDDEN] + pltpu.roll(y[:, CMP_HIDDEN:], M - 1, 0) + c_ref[...]
    z = jax.nn.gelu(z)
    o_ref[...] = jnp.dot(z.astype(BF16), w2_ref[...], preferred_element_type=F32)


def _compress(kv6, cmp_pe, cmp_w1, cmp_b1, cmp_w2):
    B, T = kv6.shape[:2]
    G, DH = NSA_KV_GROUPS, NSA_HEAD_DIM
    M = T // CMP_STRIDE
    half = CMP_STRIDE * DH
    r = kv6[:, :, 0:2].transpose(0, 2, 3, 1, 4).reshape(B, 2, G, M, half)
    w1 = cmp_w1.reshape(2, 2, half, CMP_HIDDEN)
    w1c = jnp.concatenate([w1[:, 0], w1[:, 1]], axis=-1).astype(BF16)
    cvec = (jnp.einsum('wld,wldk->wk', cmp_pe, cmp_w1, precision=lax.Precision.HIGHEST)
            + cmp_b1).reshape(2, 1, CMP_HIDDEN)
    return pl.pallas_call(
        _cmp_body,
        grid=(B, 2, G),
        in_specs=[pl.BlockSpec((None, None, None, M, half), lambda b, w, g: (b, w, g, 0, 0)),
                  pl.BlockSpec((None, half, 2 * CMP_HIDDEN), lambda b, w, g: (w, 0, 0)),
                  pl.BlockSpec((None, 1, CMP_HIDDEN), lambda b, w, g: (w, 0, 0)),
                  pl.BlockSpec((None, CMP_HIDDEN, DH), lambda b, w, g: (w, 0, 0))],
        out_specs=pl.BlockSpec((None, None, None, M, DH), lambda b, w, g: (b, w, g, 0, 0)),
        out_shape=jax.ShapeDtypeStruct((B, 2, G, M, DH), F32),
        compiler_params=_cparams(("arbitrary",) * 3),
        name="kv_compress",
    )(r, w1c, cvec, cmp_w2.astype(BF16))


def _nsa_cmp_body(q_ref, kc_ref, vct_ref, ct_ref, ovl_ref, oc_ref, sel_ref, qt_ref, s_ref,
                  *, n_blk, topn):
    i = pl.program_id(2)
    L = NSA_HPG * TQ
    qt = jnp.concatenate([_heads_to_lanes(q_ref[...].astype(F32), NSA_HPG), jnp.ones((1, L), F32),
                          jnp.zeros((LANE - NSA_HEAD_DIM - 1, L), F32)], axis=0).astype(BF16)
    qt_ref[...] = qt
    s_ref[...] = jnp.dot(kc_ref[...], qt, preferred_element_type=F32)
    near = pl.ds(pl.multiple_of(i * 8, 8), CMP_NEAR)
    s_ref[near, :] = s_ref[near, :] + ct_ref[...]
    s = s_ref[...]
    MP = s.shape[0]
    row = lax.broadcasted_iota(I32, (MP, 1), 0)
    s = jnp.where(row < i * 8 + CMP_NEAR, s, NEG_INF)
    m = jnp.maximum(jnp.max(s, axis=0, keepdims=True), M_FLOOR)
    p = jnp.exp2(s - m)
    l = jnp.sum(p, axis=0, keepdims=True)
    p = p / jnp.maximum(l, 1e-30)
    pb = p.astype(BF16)
    oc_ref[...] = jnp.dot(vct_ref[...], pb, preferred_element_type=F32)[:NSA_HEAD_DIM]
    psum = p[:, 0:TQ]
    for hh in range(1, NSA_HPG):
        psum = psum + p[:, hh * TQ:(hh + 1) * TQ]
    imp = jnp.dot(ovl_ref[...], psum.astype(BF16), preferred_element_type=F32)

    blk = lax.broadcasted_iota(I32, (LANE, 1), 0)
    t = i * TQ + lax.broadcasted_iota(I32, (1, TQ), 1)
    cur = t >> int(math.log2(SLC_BLOCK))
    admiss = blk <= cur
    bonus = jnp.where(blk == 0, FORCE_BONUS, 0.0)
    bonus = jnp.where(blk == cur, FORCE_BONUS, bonus)
    bonus = jnp.where(blk == cur - 1, FORCE_BONUS, bonus)
    score = jnp.where(admiss, imp + bonus, NEG_INF)
    score = jnp.where(blk < n_blk, score, KNOCK)
    blk_b = jnp.broadcast_to(blk, (LANE, TQ))
    selneg = jnp.full((LANE, TQ), NEG_INF, F32)
    for _ in range(topn):
        mx = jnp.max(score, axis=0, keepdims=True)
        first = jnp.min(jnp.where(score == mx, blk_b, LANE), axis=0, keepdims=True)
        pick = blk_b == first
        selneg = jnp.where(pick, 0.0, selneg)
        score = jnp.where(pick, KNOCK, score)
    sel_ref[...] = jnp.where(admiss, selneg, NEG_INF).astype(BF16)


def _nsa_compressed(qn, kc_aug, vct, ct, ovl, n_blk):
    B, T = qn.shape[:2]
    G, nQ = NSA_KV_GROUPS, T // TQ
    MP = kc_aug.shape[2]
    body = functools.partial(_nsa_cmp_body, n_blk=n_blk, topn=min(SLC_TOPN, n_blk))
    qtile = pl.BlockSpec((None, None, None, LANE, NSA_HPG * TQ), lambda b, g, i: (b, g, i, 0, 0))
    return pl.pallas_call(
        body,
        grid=(B, G, nQ),
        in_specs=[pl.BlockSpec((None, TQ, NSA_HPG * NSA_HEAD_DIM), lambda b, g, i: (b, i, g)),
                  pl.BlockSpec((None, None, MP, LANE), lambda b, g, i: (b, g, 0, 0)),
                  pl.BlockSpec((None, None, LANE, MP), lambda b, g, i: (b, g, 0, 0)),
                  pl.BlockSpec((None, CMP_NEAR, NSA_HPG * TQ), lambda b, g, i: (g, 0, 0)),
                  pl.BlockSpec((LANE, MP), lambda b, g, i: (0, 0))],
        out_specs=[pl.BlockSpec((None, None, None, NSA_HEAD_DIM, NSA_HPG * TQ), lambda b, g, i: (b, g, i, 0, 0)),
                   pl.BlockSpec((None, None, None, LANE, TQ), lambda b, g, i: (b, g, i, 0, 0)),
                   qtile],
        out_shape=[jax.ShapeDtypeStruct((B, G, nQ, NSA_HEAD_DIM, NSA_HPG * TQ), F32),
                   jax.ShapeDtypeStruct((B, G, nQ, LANE, TQ), BF16),
                   jax.ShapeDtypeStruct((B, G, nQ, LANE, NSA_HPG * TQ), BF16)],
        scratch_shapes=[pltpu.VMEM((MP, NSA_HPG * TQ), F32)],
        compiler_params=_cparams(("arbitrary",) * 3),
        name="nsa_compressed",
    )(qn, kc_aug, vct, ct, ovl)


def _softmax_step(s, vt, m_old, acc):
    m_new = jnp.maximum(m_old, jnp.max(s, axis=0, keepdims=True))
    alpha = jnp.exp2(m_old - m_new)
    p = jnp.exp2(s - m_new).astype(BF16)
    return m_new, acc * alpha + jnp.dot(vt, p, preferred_element_type=F32)


AK = 256


def _pipeline(lo, hi, produce, consume, buf_a, buf_b):
    n = hi - lo

    @pl.when(n > 0)
    def _():
        produce(lo, buf_a)

        def body(p, c):
            j = lo + 2 * p
            produce(j + 1, buf_b)
            consume(j, buf_a)
            produce(j + 2, buf_a)
            consume(j + 1, buf_b)
            return c

        lax.fori_loop(0, (n - 1) // 2, body, 0)

        @pl.when(n % 2 == 1)
        def _():
            consume(hi - 1, buf_a)

        @pl.when(n % 2 == 0)
        def _():
            produce(hi - 1, buf_b)
            consume(hi - 2, buf_a)
            consume(hi - 1, buf_b)


def _attend(lo, hi, lhs_fn, rhs_ref, value_fn, m_ref, acc_ref, sa_ref, sb_ref, near_fn=None):
    def scores(j, buf):
        s = jnp.dot(lhs_fn(j), rhs_ref[...], preferred_element_type=F32)
        buf[...] = s if near_fn is None else near_fn(s, j)

    def consume(j, buf):
        m, acc = _softmax_step(buf[...], value_fn(j), m_ref[...], acc_ref[...])
        m_ref[...] = m
        acc_ref[...] = acc

    _pipeline(lo, hi, scores, consume, sa_ref, sb_ref)


def _near_add(dt_ref, i, s, j):
    parts = []
    for sub in range(AK // TQ):
        idx = jnp.clip(i - (j * (AK // TQ) + sub) + 1, 0, 4)
        parts.append(s[sub * TQ:(sub + 1) * TQ] + dt_ref[idx])
    return jnp.concatenate(parts, axis=0)


def _nsa_sel_body(qt_ref, sel_ref, ks_ref, vst_ref, dt_ref, o_ref, qa_ref, m_ref, acc_ref,
                  sa_ref, sb_ref):
    i = pl.program_id(2)
    L = NSA_HPG * TQ
    qa_ref[0:LANE, :] = qt_ref[...]
    selneg = sel_ref[...]
    qa_ref[LANE:2 * LANE, :] = jnp.concatenate([selneg] * NSA_HPG, axis=1)
    m_ref[...] = jnp.full((1, L), M_FLOOR, F32)
    acc_ref[...] = jnp.zeros(acc_ref.shape, F32)
    last = i // (AK // TQ)
    n_far = jnp.maximum(last - 1, 0)

    def keys(j):
        return ks_ref[pl.ds(pl.multiple_of(j * AK, AK), AK), :]

    def value(j):
        return vst_ref[:, pl.ds(pl.multiple_of(j * AK, AK), AK)]

    _attend(0, n_far, keys, qa_ref, value, m_ref, acc_ref, sa_ref, sb_ref)
    _attend(n_far, last + 1, keys, qa_ref, value, m_ref, acc_ref, sa_ref, sb_ref,
            near_fn=functools.partial(_near_add, dt_ref, i))
    acc = acc_ref[...]
    o_ref[...] = acc[:NSA_HEAD_DIM] / acc[NSA_HEAD_DIM:NSA_HEAD_DIM + 1]


def _nsa_selected(qt, selneg, ks_aug, vst, dt):
    B, G, nQ = qt.shape[:3]
    T = ks_aug.shape[2]
    R = vst.shape[2]
    L = NSA_HPG * TQ
    return pl.pallas_call(
        _nsa_sel_body,
        grid=(B, G, nQ),
        in_specs=[pl.BlockSpec((None, None, None, LANE, L), lambda b, g, i: (b, g, i, 0, 0)),
                  pl.BlockSpec((None, None, None, LANE, TQ), lambda b, g, i: (b, g, i, 0, 0)),
                  pl.BlockSpec((None, None, T, 2 * LANE), lambda b, g, i: (b, g, 0, 0)),
                  pl.BlockSpec((None, None, R, T), lambda b, g, i: (b, g, 0, 0)),
                  pl.BlockSpec((None, 5, TQ, L), lambda b, g, i: (g, 0, 0, 0))],
        out_specs=pl.BlockSpec((None, None, None, NSA_HEAD_DIM, L), lambda b, g, i: (b, g, i, 0, 0)),
        out_shape=jax.ShapeDtypeStruct((B, G, nQ, NSA_HEAD_DIM, L), F32),
        scratch_shapes=[pltpu.VMEM((2 * LANE, L), BF16),
                        pltpu.VMEM((1, L), F32),
                        pltpu.VMEM((R, L), F32),
                        pltpu.VMEM((AK, L), F32),
                        pltpu.VMEM((AK, L), F32)],
        compiler_params=_cparams(("arbitrary",) * 3),
        name="nsa_selected",
    )(qt, selneg, ks_aug, vst, dt)


def _nsa_win_body(qt_ref, kw_ref, vwt_ref, w0_ref, dt_ref, oc_ref, os_ref, gl_ref, o_ref):
    i = pl.program_id(2)
    span = WINDOW + TQ
    k0 = pl.multiple_of(i * TQ, TQ)
    s = jnp.dot(kw_ref[pl.ds(k0, span), :], qt_ref[...], preferred_element_type=F32)
    s = jnp.concatenate([s[0:TQ] + w0_ref[...], s[TQ:WINDOW - TQ],
                         s[WINDOW - TQ:WINDOW] + dt_ref[2], s[WINDOW:span] + dt_ref[1]], axis=0)
    m = jnp.maximum(jnp.max(s, axis=0, keepdims=True), M_FLOOR)
    p = jnp.exp2(s - m).astype(BF16)
    acc = jnp.dot(vwt_ref[:, pl.ds(k0, span)], p, preferred_element_type=F32)
    ow = acc[:NSA_HEAD_DIM] / acc[NSA_HEAD_DIM:NSA_HEAD_DIM + 1]
    g = jax.nn.sigmoid(gl_ref[...])
    o_t = g[0:1] * oc_ref[...] + g[1:2] * os_ref[...] + g[2:3] * ow
    o_ref[...] = _lanes_to_heads(o_t, NSA_HPG).astype(BF16)


def _nsa_window(qt, kw_aug, vwt, w0, dt, oc, osel, gl):
    B, G, nQ = qt.shape[:3]
    TP = kw_aug.shape[2]
    R = vwt.shape[2]
    L = NSA_HPG * TQ
    W = NSA_HPG * NSA_HEAD_DIM
    tile = lambda r: pl.BlockSpec((None, None, None, r, L), lambda b, g, i: (b, g, i, 0, 0))
    return pl.pallas_call(
        _nsa_win_body,
        grid=(B, G, nQ),
        in_specs=[tile(LANE),
                  pl.BlockSpec((None, None, TP, LANE), lambda b, g, i: (b, g, 0, 0)),
                  pl.BlockSpec((None, None, R, TP), lambda b, g, i: (b, g, 0, 0)),
                  pl.BlockSpec((TQ, L), lambda b, g, i: (0, 0)),
                  pl.BlockSpec((None, 5, TQ, L), lambda b, g, i: (g, 0, 0, 0)),
                  tile(NSA_HEAD_DIM), tile(NSA_HEAD_DIM), tile(8)],
        out_specs=pl.BlockSpec((None, TQ, W), lambda b, g, i: (b, i, g)),
        out_shape=jax.ShapeDtypeStruct((B, nQ * TQ, G * W), BF16),
        compiler_params=_cparams(("arbitrary",) * 3),
        name="nsa_window",
    )(qt, kw_aug, vwt, w0, dt, oc, osel, gl)


INT_MIN = -2 ** 31
CK = 2 * NK
KEY_BITS = 16


def _dsa_body(qd_ref, qi_ref, wt_ref, ki_ref, ckv_ref, ckvt_ref, wuk_ref, wuvt_ref, dt_ref,
              o_ref, sc_ref, qb_ref, m_ref, acc_ref, sa_ref, sb_ref, *, k_top):
    i = pl.program_id(1)
    H = DSA_HEADS
    L = H * TQ
    last = i // (NK // TQ)
    n_steps = last + 1
    t_row = i * TQ + lax.broadcasted_iota(I32, (1, TQ), 1)

    qit = jnp.concatenate([_heads_to_lanes(qi_ref[...].astype(F32), IDX_HEADS),
                           jnp.zeros((LANE - IDX_DIM, L), F32)], axis=0).astype(BF16)
    wt = wt_ref[...]

    def idx_scores(j, buf):
        k0 = pl.multiple_of(j * NK, NK)
        buf[...] = jnp.dot(ki_ref[pl.ds(k0, NK), :], qit, preferred_element_type=F32)

    def idx_reduce(j, buf):
        k0 = pl.multiple_of(j * NK, NK)
        r = jnp.maximum(buf[...], 0.0) * wt
        sc = r[:, 0:TQ]
        for h in range(1, H):
            sc = sc + r[:, h * TQ:(h + 1) * TQ]
        bits = pltpu.bitcast(sc, I32)
        ik = jnp.where(sc == 0.0, 0, bits ^ ((bits >> 31) & 0x7FFFFFFF))
        key = k0 + lax.broadcasted_iota(I32, (NK, 1), 0)
        sc_ref[pl.ds(k0, NK), :] = jnp.where(key <= t_row, jnp.maximum(ik, INT_MIN + 1), INT_MIN)

    _pipeline(0, n_steps, idx_scores, idx_reduce, sa_ref, sb_ref)

    @pl.when(n_steps % 2 == 1)
    def _():
        sc_ref[pl.ds(pl.multiple_of(n_steps * NK, NK), NK), :] = jnp.full((NK, TQ), INT_MIN, I32)

    n_chunks = (n_steps + 1) // 2

    def count(pred):
        def cbody(j, cnt):
            k0 = pl.multiple_of(j * CK, CK)
            ind = jnp.where(pred(sc_ref[pl.ds(k0, CK), :], k0), 1.0, 0.0)
            parts = [ind[r * 8:(r + 1) * 8] for r in range(CK // 8)]
            while len(parts) > 1:
                parts = [a + b for a, b in zip(parts[0::2], parts[1::2])]
            return cnt + parts[0]
        cnt8 = lax.fori_loop(0, n_chunks, cbody, jnp.zeros((8, TQ), F32))
        return jnp.sum(cnt8, axis=0, keepdims=True)

    def any_lane(cond):
        return jnp.max(jnp.where(cond, 1.0, 0.0)) > 0.0

    kf = float(k_top)
    need = t_row + 1 > k_top

    def bit_body(b, c):
        tu, ct = c
        cand = tu | jnp.left_shift(jnp.int32(1), 31 - b)
        thr_c = cand ^ INT_MIN
        cnt = count(lambda blk, k0: blk >= thr_c)
        take = cnt >= kf
        return jnp.where(take, cand, tu), jnp.where(take, cnt, ct)

    tu, ct = lax.fori_loop(0, 32, bit_body, (jnp.zeros((1, TQ), I32), (t_row + 1).astype(F32)))
    thr = jnp.maximum(tu ^ INT_MIN, INT_MIN + 1)

    @pl.when(any_lane(need & (ct > kf)))
    def _():
        keep = kf - count(lambda blk, k0: blk > thr)

        def tie_count(bound):
            def pred(blk, k0):
                key = k0 + lax.broadcasted_iota(I32, (CK, 1), 0)
                return jnp.where(key < bound, blk, INT_MIN) == thr
            return count(pred)

        def pos_body(b, pos):
            cand = pos | jnp.left_shift(jnp.int32(1), KEY_BITS - 1 - b)
            return jnp.where(tie_count(cand) < keep, cand, pos)

        pos = lax.fori_loop(0, KEY_BITS, pos_body, jnp.zeros((1, TQ), I32))

        def fix(j, c):
            k0 = pl.multiple_of(j * CK, CK)
            blk = sc_ref[pl.ds(k0, CK), :]
            key = k0 + lax.broadcasted_iota(I32, (CK, 1), 0)
            drop = jnp.where(key > pos, blk, INT_MIN) == thr
            sc_ref[pl.ds(k0, CK), :] = jnp.where(drop, INT_MIN, blk)
            return c

        lax.fori_loop(0, n_chunks, fix, 0)

    qdt = _heads_to_lanes(qd_ref[...].astype(F32), H).astype(BF16)
    for h in range(H):
        ql = jnp.dot(wuk_ref[h], qdt[:, h * TQ:(h + 1) * TQ], preferred_element_type=F32)
        qb_ref[0:LANE, h * TQ:(h + 1) * TQ] = (ql * (DSA_HEAD_DIM ** -0.5 * LOG2E)).astype(BF16)
    eye = (lax.broadcasted_iota(I32, (LANE, TQ), 0) == lax.broadcasted_iota(I32, (LANE, TQ), 1))
    eye = jnp.where(eye, 1.0, 0.0).astype(BF16)
    qb_ref[LANE:2 * LANE, :] = jnp.concatenate([eye] * H, axis=1)
    m_ref[...] = jnp.full((1, L), M_FLOOR, F32)
    acc_ref[...] = jnp.zeros(acc_ref.shape, F32)
    alast = i // (AK // TQ)
    n_far = jnp.maximum(alast - 1, 0)

    def keys(j):
        k0 = pl.multiple_of(j * AK, AK)
        nb = jnp.where(sc_ref[pl.ds(k0, AK), :] >= thr, 0.0, NEG_INF).astype(BF16)
        return jnp.concatenate([ckv_ref[pl.ds(k0, AK), :], nb], axis=1)

    def value(j):
        return ckvt_ref[:, pl.ds(pl.multiple_of(j * AK, AK), AK)]

    _attend(0, n_far, keys, qb_ref, value, m_ref, acc_ref, sa_ref, sb_ref)
    _attend(n_far, alast + 1, keys, qb_ref, value, m_ref, acc_ref, sa_ref, sb_ref,
            near_fn=functools.partial(_near_add, dt_ref, i))
    acc = acc_ref[...]
    olat = (acc[:DSA_KV_LATENT] / acc[DSA_KV_LATENT:DSA_KV_LATENT + 1]).astype(BF16)
    o_t = jnp.concatenate([jnp.dot(wuvt_ref[h], olat[:, h * TQ:(h + 1) * TQ],
                                   preferred_element_type=F32) for h in range(H)], axis=1)
    o_ref[...] = _lanes_to_heads(o_t, H).astype(BF16)


def _dsa(qd, qi, wt, ki, ckv, ckvt, wuk, wuvt, dt, k_top):
    B, T = ki.shape[:2]
    nQ = T // TQ
    H = DSA_HEADS
    L = H * TQ
    W = H * DSA_HEAD_DIM
    R = ckvt.shape[1]
    rows = pl.BlockSpec((None, TQ, W), lambda b, i: (b, i, 0))
    return pl.pallas_call(
        functools.partial(_dsa_body, k_top=k_top),
        grid=(B, nQ),
        in_specs=[rows, rows, pl.BlockSpec((None, None, 1, L), lambda b, i: (b, i, 0, 0)),
                  pl.BlockSpec((None, T, LANE), lambda b, i: (b, 0, 0)),
                  pl.BlockSpec((None, T, DSA_KV_LATENT), lambda b, i: (b, 0, 0)),
                  pl.BlockSpec((None, R, T), lambda b, i: (b, 0, 0)),
                  pl.BlockSpec((H, DSA_KV_LATENT, DSA_HEAD_DIM), lambda b, i: (0, 0, 0)),
                  pl.BlockSpec((H, DSA_HEAD_DIM, DSA_KV_LATENT), lambda b, i: (0, 0, 0)),
                  pl.BlockSpec((5, TQ, L), lambda b, i: (0, 0, 0))],
        out_specs=rows,
        out_shape=jax.ShapeDtypeStruct((B, T, W), BF16),
        scratch_shapes=[pltpu.VMEM((T, TQ), I32),
                        pltpu.VMEM((2 * LANE, L), BF16),
                        pltpu.VMEM((1, L), F32),
                        pltpu.VMEM((R, L), F32),
                        pltpu.VMEM((AK, L), F32),
                        pltpu.VMEM((AK, L), F32)],
        compiler_params=_cparams(("arbitrary",) * 2),
        name="dsa_attention",
    )(qd, qi, wt, ki, ckv, ckvt, wuk, wuvt, dt)


def _token_mixers(qn, kvn, qd, ckv, qi, small, ki, rel_bias, cmp_pe, cmp_w1, cmp_b1, cmp_w2,
                  dsa_w_uk, dsa_w_uv):
    B, T = qn.shape[:2]
    G, HPG, DH = NSA_KV_GROUPS, NSA_HPG, NSA_HEAD_DIM
    nQ = T // TQ
    n_blk = T // SLC_BLOCK
    L = HPG * TQ
    assert T % NK == 0 and n_blk <= LANE
    kv6 = kvn.reshape(B, T, 6, G, DH)

    tab_nsa = rel_bias[:, :NSA_HEADS]
    dt_nsa = _near_tables(tab_nsa).reshape(5, TQ, G, L).transpose(2, 0, 1, 3)
    dt_dsa = _near_tables(rel_bias[:, NSA_HEADS:])

    kvc = _compress(kv6, cmp_pe, cmp_w1, cmp_b1, cmp_w2)
    M = T // CMP_STRIDE
    MP = M + CMP_PAD
    kc = kvc[:, 0].astype(BF16)
    flag = jnp.concatenate([jnp.full((CMP_PAD, 1), NEG_INF, F32), jnp.zeros((M, 1), F32)]).astype(BF16)
    kc_aug = jnp.concatenate(
        [jnp.pad(kc, ((0, 0), (0, 0), (CMP_PAD, 0), (0, 0))),
         jnp.broadcast_to(flag, (B, G, MP, 1)),
         jnp.zeros((B, G, MP, LANE - DH - 1), BF16)], axis=-1)
    vct = jnp.pad(kvc[:, 1].astype(BF16).transpose(0, 1, 3, 2),
                  ((0, 0), (0, 0), (0, LANE - DH), (CMP_PAD, 0)))
    mrow = jnp.arange(CMP_NEAR)[:, None]
    qcol = jnp.arange(TQ)[None, :]
    ct = _delta(tab_nsa, qcol - CMP_STRIDE * mrow + (2 * TQ - CMP_BLOCK + 1))
    ct = ct.reshape(CMP_NEAR, TQ, G, HPG).transpose(2, 0, 3, 1).reshape(G, CMP_NEAR, L)
    n_cmp = (T - CMP_BLOCK) // CMP_STRIDE + 1
    cs = jnp.arange(M) * CMP_STRIDE
    ss = jnp.arange(LANE) * SLC_BLOCK
    ovl = (jnp.clip(jnp.minimum(cs[None, :] + CMP_BLOCK, ss[:, None] + SLC_BLOCK)
                    - jnp.maximum(cs[None, :], ss[:, None]), 0, None).astype(F32) / CMP_BLOCK)
    ovl = jnp.where((jnp.arange(M)[None, :] < n_cmp) & (jnp.arange(LANE)[:, None] < n_blk), ovl, 0.0)
    ovl = jnp.pad(ovl, ((0, 0), (CMP_PAD, 0))).astype(BF16)
    oc, selneg, qt = _nsa_compressed(qn, kc_aug, vct, ct, ovl, n_blk)

    ks = kv6[:, :, 2].transpose(0, 2, 1, 3)
    et = (jnp.arange(T)[:, None] // SLC_BLOCK == jnp.arange(LANE)[None, :]).astype(BF16)
    ks_aug = jnp.concatenate([ks, jnp.zeros((B, G, T, LANE - DH), BF16),
                              jnp.broadcast_to(et, (B, G, T, LANE))], axis=-1)
    vs = kv6[:, :, 3].transpose(0, 2, 3, 1)
    vst = jnp.concatenate([vs, jnp.ones((B, G, 1, T), BF16),
                           jnp.zeros((B, G, V_ROWS - DH - 1, T), BF16)], axis=2)
    osel = _nsa_selected(qt, selneg, ks_aug, vst, dt_nsa)

    TP = T + WINDOW
    kw = jnp.pad(kv6[:, :, 4].transpose(0, 2, 1, 3), ((0, 0), (0, 0), (WINDOW, 0), (0, 0)))
    wflag = jnp.concatenate([jnp.full((WINDOW, 1), NEG_INF, F32), jnp.zeros((T, 1), F32)]).astype(BF16)
    kw_aug = jnp.concatenate([kw, jnp.broadcast_to(wflag, (B, G, TP, 1)),
                              jnp.zeros((B, G, TP, LANE - DH - 1), BF16)], axis=-1)
    vw = jnp.pad(kv6[:, :, 5].transpose(0, 2, 3, 1), ((0, 0), (0, 0), (0, 0), (WINDOW, 0)))
    vwt = jnp.concatenate([vw, jnp.ones((B, G, 1, TP), BF16),
                           jnp.zeros((B, G, V_ROWS - DH - 1, TP), BF16)], axis=2)
    u = jnp.arange(TQ)[:, None]
    w0 = jnp.tile(jnp.where(u > jnp.arange(TQ)[None, :], 0.0, NEG_INF).astype(F32), (1, HPG))
    gl = small[..., 0:3 * NSA_HEADS].reshape(B, nQ, TQ, G, HPG, 3).transpose(0, 3, 1, 5, 4, 2)
    gl = jnp.pad(gl.reshape(B, G, nQ, 3, L), ((0, 0), (0, 0), (0, 0), (0, 5), (0, 0)))
    o_nsa = _nsa_window(qt, kw_aug, vwt, w0, dt_nsa, oc, osel, gl)

    H = DSA_HEADS
    w_idx = small[..., 24 + IDX_DIM:24 + IDX_DIM + IDX_HEADS] * IDX_HEADS ** -0.5
    wt = w_idx.reshape(B, nQ, TQ, IDX_HEADS).transpose(0, 1, 3, 2).reshape(B, nQ, 1, IDX_HEADS * TQ)
    ckvt = jnp.concatenate([ckv.transpose(0, 2, 1), jnp.ones((B, 1, T), BF16),
                            jnp.zeros((B, 7, T), BF16)], axis=1)
    wuk = dsa_w_uk.transpose(1, 0, 2).astype(BF16)
    wuvt = dsa_w_uv.transpose(1, 2, 0).astype(BF16)
    o_dsa = _dsa(qd, qi, wt, ki, ckv, ckvt, wuk, wuvt, dt_dsa, min(IDX_TOPK_MAX, T // 4))
    return o_nsa, o_dsa


def _rms(v, g):
    return v * lax.rsqrt(jnp.mean(v * v, axis=-1, keepdims=True) + RMS_EPS) * g


HI_MASK = -65536


def _pack_halves(v):
    w = v.shape[1] // 2
    lo = lax.shift_right_logical(pltpu.bitcast(v[:, :w], I32), 16)
    hi = pltpu.bitcast(v[:, w:], I32) & HI_MASK
    return lo | hi


def _unpack_halves(p):
    return pltpu.bitcast(p << 16, F32), pltpu.bitcast(p & HI_MASK, F32)


def _merge_body(on_ref, od_ref, gm_ref, x_ref, wn_ref, wd_ref, wo_ref, gpost_ref, gt_ref,
                gpre_ref, sc_ref, sh_ref, rwt_ref, x1_ref, h2_ref, hpa_ref, hpb_ref, lt_ref):
    D = D_MODEL
    ya = jnp.dot(on_ref[...], wn_ref[...], preferred_element_type=F32)
    yb = jnp.dot(od_ref[...], wd_ref[...], preferred_element_type=F32)
    y = jax.nn.sigmoid(gm_ref[:, 0:D]) * ya + jax.nn.sigmoid(gm_ref[:, D:2 * D]) * yb
    y2 = jnp.dot(y.astype(BF16), wo_ref[...], preferred_element_type=F32)
    x1 = x_ref[...] + gt_ref[...] * _rms(y2, gpost_ref[...])
    x1_ref[...] = x1
    h2 = (_rms(x1, gpre_ref[...]) * (1.0 + sc_ref[...]) + sh_ref[...]).astype(BF16)
    h2_ref[...] = h2
    packed = _pack_halves(h2.astype(F32))
    hpa_ref[...] = packed[:, :SC_WIDTH]
    hpb_ref[...] = packed[:, SC_WIDTH:]
    lt_ref[...] = lax.dot_general(rwt_ref[...], h2, (((1,), (1,)), ((), ())),
                                  preferred_element_type=F32)


def _merge(o_nsa, o_dsa, gm, x, wn, wd, wo, g_post, gt1, g_pre, sc2, sh2, router_w):
    B, T, D = x.shape
    TM = 256
    E = router_w.shape[1]
    row = lambda w: pl.BlockSpec((None, TM, w), lambda b, i: (b, i, 0))
    full = lambda a: pl.BlockSpec(a.shape, lambda b, i: (0,) * a.ndim)
    vecb = pl.BlockSpec((None, 1, D), lambda b, i: (b, 0, 0))
    vec = pl.BlockSpec((1, D), lambda b, i: (0, 0))
    wn, wd, wo = wn.astype(BF16), wd.astype(BF16), wo.astype(BF16)
    rwt = router_w.T.astype(BF16)
    return pl.pallas_call(
        _merge_body,
        grid=(B, T // TM),
        in_specs=[row(o_nsa.shape[-1]), row(o_dsa.shape[-1]), row(2 * D), row(D),
                  full(wn), full(wd), full(wo), vec, vecb, vec, vecb, vecb, full(rwt)],
        out_specs=[row(D), row(D), row(SC_WIDTH), row(SC_WIDTH),
                   pl.BlockSpec((None, E, TM), lambda b, i: (b, 0, i))],
        out_shape=[jax.ShapeDtypeStruct((B, T, D), F32),
                   jax.ShapeDtypeStruct((B, T, D), BF16),
                   jax.ShapeDtypeStruct((B, T, SC_WIDTH), I32),
                   jax.ShapeDtypeStruct((B, T, SC_WIDTH), I32),
                   jax.ShapeDtypeStruct((B, E, T), F32)],
        compiler_params=_cparams(("arbitrary",) * 2),
        name="merge_postnorm",
    )(o_nsa, o_dsa, gm, x, wn, wd, wo, g_post.reshape(1, D), gt1.reshape(B, 1, D),
      g_pre.reshape(1, D), sc2.reshape(B, 1, D), sh2.reshape(B, 1, D), rwt)


TR = 256
GSZ = N_EXPERTS // N_EXPERT_GROUPS


def _first_max(v, idx, big):
    mx = jnp.max(v, axis=0, keepdims=True)
    return jnp.min(jnp.where(v == mx, idx, big), axis=0, keepdims=True), mx


def _route_body(lt_ref, bias_ref, eid_ref, rk_ref, wt_ref, cnt_ref, carry_ref):
    first_step = (pl.program_id(0) == 0) & (pl.program_id(1) == 0)

    @pl.when(first_step)
    def _():
        carry_ref[...] = jnp.zeros(carry_ref.shape, F32)

    s = jax.nn.sigmoid(lt_ref[...])
    sel = s + bias_ref[:, 0:1]
    grow = lax.broadcasted_iota(I32, (GSZ, TR), 0)
    gs = []
    for g in range(N_EXPERT_GROUPS):
        blk = sel[g * GSZ:(g + 1) * GSZ]
        f1, m1 = _first_max(blk, grow, GSZ)
        m2 = jnp.max(jnp.where(grow == f1, KNOCK, blk), axis=0, keepdims=True)
        gs.append(m1 + m2)
    gsc = jnp.concatenate(gs, axis=0)
    gidx = lax.broadcasted_iota(I32, (N_EXPERT_GROUPS, TR), 0)
    gsel = jnp.zeros((N_EXPERT_GROUPS, TR), F32)
    for _ in range(TOPK_GROUPS):
        f, _mx = _first_max(gsc, gidx, N_EXPERT_GROUPS)
        pick = gidx == f
        gsel = jnp.where(pick, 1.0, gsel)
        gsc = jnp.where(pick, KNOCK, gsc)
    cur = jnp.concatenate(
        [jnp.where(gsel[g:g + 1] > 0.0, sel[g * GSZ:(g + 1) * GSZ], NEG_INF)
         for g in range(N_EXPERT_GROUPS)], axis=0)
    erow = lax.broadcasted_iota(I32, (N_EXPERTS, TR), 0)
    member = jnp.zeros((N_EXPERTS, TR), F32)
    picks, eids, ws = [], [], []
    for _ in range(TOP_K):
        f, _mx = _first_max(cur, erow, N_EXPERTS)
        pick = erow == f
        picks.append(pick)
        eids.append(f)
        ws.append(jnp.sum(jnp.where(pick, s, 0.0), axis=0, keepdims=True))
        member = jnp.where(pick, 1.0, member)
        cur = jnp.where(pick, KNOCK, cur)
    wsum = ws[0]
    for k in range(1, TOP_K):
        wsum = wsum + ws[k]
    mb = member.astype(BF16)
    upper = (lax.broadcasted_iota(I32, (TR, TR), 0) < lax.broadcasted_iota(I32, (TR, TR), 1))
    upper = jnp.where(upper, 1.0, 0.0).astype(BF16)
    rank = jnp.dot(mb, upper, preferred_element_type=F32) + carry_ref[:, 0:1]
    eid_ref[...] = jnp.concatenate(eids, axis=0)
    rk_ref[...] = jnp.concatenate(
        [jnp.sum(jnp.where(p, rank, 0.0), axis=0, keepdims=True) for p in picks], axis=0)
    wt_ref[...] = jnp.concatenate([w / wsum * ROUTED_SCALE for w in ws], axis=0)
    carry_ref[...] = carry_ref[...] + jnp.dot(mb, jnp.ones((TR, LANE), BF16),
                                              preferred_element_type=F32)
    cnt_ref[...] = carry_ref[...]


def _route(lt, router_bias):
    B, E, T = lt.shape
    out8 = lambda: pl.BlockSpec((None, TOP_K, TR), lambda b, i: (b, 0, i))
    return pl.pallas_call(
        _route_body,
        grid=(B, T // TR),
        in_specs=[pl.BlockSpec((None, E, TR), lambda b, i: (b, 0, i)),
                  pl.BlockSpec((E, 1), lambda b, i: (0, 0))],
        out_specs=[out8(), out8(), out8(), pl.BlockSpec((E, LANE), lambda b, i: (0, 0))],
        out_shape=[jax.ShapeDtypeStruct((B, TOP_K, T), I32),
                   jax.ShapeDtypeStruct((B, TOP_K, T), F32),
                   jax.ShapeDtypeStruct((B, TOP_K, T), F32),
                   jax.ShapeDtypeStruct((E, LANE), F32)],
        scratch_shapes=[pltpu.VMEM((E, LANE), F32)],
        compiler_params=_cparams(("arbitrary",) * 2),
        name="moe_router",
    )(lt, router_bias.reshape(E, 1))


BM = 256


def _gmm_body(be_ref, nv_ref, xa_ref, xb_ref, wg_ref, wu_ref, wd_ref, ya_ref, yb_ref,
              wgb, wub, wdb):
    b = pl.program_id(0)
    valid = nv_ref[b]

    @pl.when(valid > 0)
    def _():
        prev = be_ref[jnp.maximum(b - 1, 0)]

        @pl.when((b == 0) | (be_ref[b] != prev))
        def _():
            wgb[...] = wg_ref[...].astype(BF16)
            wub[...] = wu_ref[...].astype(BF16)
            wdb[...] = wd_ref[...].astype(BF16)

        live = lax.broadcasted_iota(I32, (BM, 1), 0) < valid
        lo_a, hi_a = _unpack_halves(jnp.where(live, xa_ref[...], 0))
        lo_b, hi_b = _unpack_halves(jnp.where(live, xb_ref[...], 0))
        xb = jnp.concatenate([lo_a, lo_b, hi_a, hi_b], axis=1).astype(BF16)
        gt = jnp.dot(xb, wgb[...], preferred_element_type=F32)
        up = jnp.dot(xb, wub[...], preferred_element_type=F32)
        a = (gt * jax.nn.sigmoid(gt) * up).astype(BF16)
        y = jnp.dot(a, wdb[...], preferred_element_type=F32)
        packed = _pack_halves(y.astype(BF16).astype(F32))
        ya_ref[...] = packed[:, :SC_WIDTH]
        yb_ref[...] = packed[:, SC_WIDTH:]

    @pl.when(valid == 0)
    def _():
        ya_ref[...] = jnp.zeros(ya_ref.shape, I32)
        yb_ref[...] = jnp.zeros(yb_ref.shape, I32)


def _grouped_mlp(xs_a, xs_b, blk_e, n_valid, w_g, w_u, w_d):
    P = xs_a.shape[0]
    D, De = w_g.shape[-2:]
    nb = P // BM
    rows = pl.BlockSpec((BM, SC_WIDTH), lambda b, be, nv: (b, 0))
    wsel = lambda b, be, nv: (be[b], 0, 0)
    grid_spec = pltpu.PrefetchScalarGridSpec(
        num_scalar_prefetch=2,
        grid=(nb,),
        in_specs=[rows, rows,
                  pl.BlockSpec((None, D, De), wsel),
                  pl.BlockSpec((None, D, De), wsel),
                  pl.BlockSpec((None, De, D), wsel)],
        out_specs=[rows, rows],
        scratch_shapes=[pltpu.VMEM((D, De), BF16), pltpu.VMEM((D, De), BF16),
                        pltpu.VMEM((De, D), BF16)])
    out = jax.ShapeDtypeStruct((P, SC_WIDTH), I32)
    return pl.pallas_call(
        _gmm_body,
        grid_spec=grid_spec,
        out_shape=[out, out],
        compiler_params=_cparams(("arbitrary",)),
        name="moe_experts",
    )(blk_e, n_valid, xs_a, xs_b, w_g, w_u, w_d)


def _sc_mesh():
    return plsc.VectorSubcoreMesh(core_axis_name="c", subcore_axis_name="s")


def _sc_gather(x, idx):
    n = idx.shape[0]

    @pl.kernel(out_type=jax.ShapeDtypeStruct((n, SC_WIDTH), x.dtype), mesh=_sc_mesh())
    def gather(x_hbm, i_hbm, o_hbm):
        def body(i_vmem, o_vmem):
            pltpu.sync_copy(x_hbm.at[i_vmem.at[0]], o_vmem)

        pltpu.emit_pipeline(
            body, grid=(n // SC_WINDOW,),
            in_specs=[pl.BlockSpec((1, SC_WINDOW), lambda i: (0, i))],
            out_specs=[pl.BlockSpec((SC_WINDOW, SC_WIDTH), lambda i: (i, 0))],
            core_axis_name=("c", "s"),
            dimension_semantics=(pltpu.PARALLEL,),
        )(i_hbm, o_hbm)

    return gather(x, idx.reshape(1, n))


def _sc_dispatch(x, slot, n_out):
    B, T, W = x.shape
    K = slot.shape[1]
    tb = T // SC_WINDOW
    n = B * K * T

    @pl.kernel(out_type=jax.ShapeDtypeStruct((n_out, W), x.dtype), mesh=_sc_mesh(), scratch_types=[])
    def scatter(x_hbm, i_hbm, o_hbm):
        def body(x_vmem, i_vmem):
            pltpu.sync_copy(x_vmem, o_hbm.at[i_vmem.at[0]])

        pltpu.emit_pipeline(
            body, grid=(n // SC_WINDOW,),
            in_specs=[pl.BlockSpec((SC_WINDOW, W), lambda i: ((i // (K * tb)) * tb + i % tb, 0)),
                      pl.BlockSpec((1, SC_WINDOW), lambda i: (0, i))],
            out_specs=[],
            core_axis_name=("c", "s"),
            dimension_semantics=(pltpu.PARALLEL,),
        )(x_hbm, i_hbm)

    return scatter(x.reshape(B * T, W), slot.reshape(1, n))


def _final_body(ya_ref, yb_ref, w_ref, h_ref, x1_ref, wg_ref, wu_ref, wd_ref, g_ref, gt_ref, o_ref):
    hb = h_ref[...]
    gt = jnp.dot(hb, wg_ref[...], preferred_element_type=F32)
    up = jnp.dot(hb, wu_ref[...], preferred_element_type=F32)
    a = (gt * jax.nn.sigmoid(gt) * up).astype(BF16)
    shared = jnp.dot(a, wd_ref[...], preferred_element_type=F32)
    w = w_ref[...]
    parts = None
    for k in range(TOP_K):
        wk = w[:, k:k + 1]
        lo_a, hi_a = _unpack_halves(ya_ref[k])
        lo_b, hi_b = _unpack_halves(yb_ref[k])
        terms = [lo_a * wk, lo_b * wk, hi_a * wk, hi_b * wk]
        parts = terms if parts is None else [p + t for p, t in zip(parts, terms)]
    y = jnp.concatenate(parts, axis=1) + shared
    o_ref[...] = x1_ref[...] + gt_ref[...] * _rms(y, g_ref[...])


def _final(yg_a, yg_b, w, h2, x1, ws_g, ws_u, ws_d, g_post, gt2):
    B, T, D = x1.shape
    TM = 256 if T % 256 == 0 else TQ
    row = lambda wd: pl.BlockSpec((None, TM, wd), lambda b, i: (b, i, 0))
    krows = pl.BlockSpec((None, TOP_K, TM, SC_WIDTH), lambda b, i: (b, 0, i, 0))
    full = lambda a: pl.BlockSpec(a.shape, lambda b, i: (0,) * a.ndim)
    ws_g, ws_u, ws_d = ws_g.astype(BF16), ws_u.astype(BF16), ws_d.astype(BF16)
    return pl.pallas_call(
        _final_body,
        grid=(B, T // TM),
        in_specs=[krows, krows, row(TOP_K), row(D), row(D),
                  full(ws_g), full(ws_u), full(ws_d),
                  pl.BlockSpec((1, D), lambda b, i: (0, 0)),
                  pl.BlockSpec((None, 1, D), lambda b, i: (b, 0, 0))],
        out_specs=row(D),
        out_shape=jax.ShapeDtypeStruct((B, T, D), F32),
        compiler_params=_cparams(("arbitrary",) * 2),
        name="shared_final",
    )(yg_a, yg_b, w, h2, x1, ws_g, ws_u, ws_d, g_post.reshape(1, D), gt2.reshape(B, 1, D))


def _slot_body(eid_ref, rk_ref, ps_ref, o_ref):
    erow = lax.broadcasted_iota(I32, (N_EXPERTS, TR), 0)
    ps = ps_ref[:, 0:1]
    rows = []
    for k in range(TOP_K):
        base = jnp.sum(jnp.where(erow == eid_ref[k:k + 1, :], ps, 0.0), axis=0, keepdims=True)
        rows.append(base + rk_ref[k:k + 1, :])
    o_ref[...] = jnp.concatenate(rows, axis=0).astype(I32)


def _slots(eid, rk, pstart):
    B, K, T = eid.shape
    blk = pl.BlockSpec((None, K, TR), lambda b, i: (b, 0, i))
    return pl.pallas_call(
        _slot_body,
        grid=(B, T // TR),
        in_specs=[blk, blk, pl.BlockSpec((N_EXPERTS, 1), lambda b, i: (0, 0))],
        out_specs=blk,
        out_shape=jax.ShapeDtypeStruct((B, K, T), I32),
        compiler_params=_cparams(("arbitrary",) * 2),
        name="moe_slots",
    )(eid, rk, pstart.astype(F32).reshape(N_EXPERTS, 1))


def _moe(h2, hp_a, hp_b, lt, x1, router_bias, w_g, w_u, w_d, ws_g, ws_u, ws_d, g_post, gt2):
    B, T, D = x1.shape
    N = B * T
    eid, rk, wts, cnt = _route(lt, router_bias)
    counts = cnt[:, 0].astype(I32)
    padded = (counts + BM - 1) // BM * BM
    pend = jnp.cumsum(padded)
    pstart = pend - padded
    nb = (N * TOP_K + N_EXPERTS * (BM - 1) + BM - 1) // BM
    P = nb * BM
    row0 = jnp.arange(nb, dtype=I32) * BM
    blk_e = jnp.minimum(jnp.sum(pend[None, :] <= row0[:, None], axis=1), N_EXPERTS - 1).astype(I32)
    n_valid = jnp.clip(pstart[blk_e] + counts[blk_e] - row0, 0, BM).astype(I32)
    slot = _slots(eid, rk, pstart)
    xs_a = _sc_dispatch(hp_a, slot, P)
    xs_b = _sc_dispatch(hp_b, slot, P)
    ys_a, ys_b = _grouped_mlp(xs_a, xs_b, blk_e, n_valid, w_g, w_u, w_d)
    flat = slot.reshape(N * TOP_K)
    yg_a = _sc_gather(ys_a, flat).reshape(B, TOP_K, T, SC_WIDTH)
    yg_b = _sc_gather(ys_b, flat).reshape(B, TOP_K, T, SC_WIDTH)
    return _final(yg_a, yg_b, wts.transpose(0, 2, 1), h2, x1, ws_g, ws_u, ws_d, g_post, gt2)


def kernel(x, c, w_ada, b_ada, g_pre_mix, g_post_mix, g_pre_ffn, g_post_ffn, w_in, rel_bias, cmp_pe, cmp_w1, cmp_b1, cmp_w2, dsa_kv_norm, dsa_w_uk, dsa_w_uv, w_branch_nsa, w_branch_dsa, w_out, router_w, router_bias, w_exp_gate, w_exp_up, w_exp_down, w_sh_gate, w_sh_up, w_sh_down):
    depth = w_ada.shape[0]
    D = D_MODEL
    for l in range(depth):
        mod = _adaln(c, w_ada[l], b_ada[l])
        sh1, sc1, gt1, sh2, sc2, gt2 = [mod[:, k * D:(k + 1) * D] for k in range(N_MOD)]
        qn, kvn, qd, ckv, qi, small, gm, ki = _input_proj(x, g_pre_mix[l], sc1, sh1, w_in[l],
                                                          dsa_kv_norm[l])
        o_nsa, o_dsa = _token_mixers(qn, kvn, qd, ckv, qi, small, ki, rel_bias, cmp_pe[l], cmp_w1[l],
                                     cmp_b1[l], cmp_w2[l], dsa_w_uk[l], dsa_w_uv[l])
        x1, h2, hp_a, hp_b, lt = _merge(o_nsa, o_dsa, gm, x,
                                        w_branch_nsa[l], w_branch_dsa[l], w_out[l], g_post_mix[l],
                                        gt1, g_pre_ffn[l], sc2, sh2, router_w[l])
        x = _moe(h2, hp_a, hp_b, lt, x1, router_bias[l], w_exp_gate[l], w_exp_up[l],
                 w_exp_down[l], w_sh_gate[l], w_sh_up[l], w_sh_down[l], g_post_ffn[l], gt2)
    return x
```

```python
import functools
import math

import jax
import jax.numpy as jnp
import numpy as np
from jax import lax
from jax.experimental import pallas as pl
from jax.experimental.pallas import tpu as pltpu
from jax.experimental.pallas import tpu_sc as plsc

F32 = jnp.float32
BF16 = jnp.bfloat16
I32 = jnp.int32

D_MODEL = 1024
N_MOD = 6
NSA_HEADS = 8
NSA_KV_GROUPS = 2
NSA_HPG = NSA_HEADS // NSA_KV_GROUPS
NSA_HEAD_DIM = 64
CMP_BLOCK = 32
CMP_STRIDE = 16
CMP_HIDDEN = 128
SLC_BLOCK = 64
SLC_TOPN = 16
WINDOW = 512
FORCE_BONUS = 1e4
DSA_HEADS = 8
DSA_HEAD_DIM = 64
DSA_KV_LATENT = 128
IDX_HEADS = 8
IDX_DIM = 64
IDX_TOPK_MAX = 256
REL_BUCKETS = 32
REL_MAX_DISTANCE = 128
N_EXPERTS = 256
N_EXPERT_GROUPS = 8
TOPK_GROUPS = 4
TOP_K = 8
D_EXPERT = 256
ROUTED_SCALE = 2.5
RMS_EPS = 1e-6
NEG_INF = -1e30
LOG2E = math.log2(math.e)

LANE = 128
TQ = 128
NK = 256
M_FLOOR = -1e29
KNOCK = -3e38
CMP_PAD = 16
CMP_NEAR = 24
V_ROWS = 80
VMEM_LIMIT = 56 * 1024 * 1024
SC_WINDOW = 128
SC_WIDTH = 256

PROJ_WIDTHS = (512, 768, 24, 512, 128, 512, 64, 8, 2048)
PROJ_OFFS = tuple(int(v) for v in np.cumsum((0,) + PROJ_WIDTHS))


def _cparams(sem):
    return pltpu.CompilerParams(dimension_semantics=sem, vmem_limit_bytes=VMEM_LIMIT)


def _mod_body(c_ref, w_ref, b_ref, o_ref):
    c = c_ref[...]
    s = c * jax.nn.sigmoid(c)
    o_ref[...] = jnp.dot(s, w_ref[...], preferred_element_type=F32) + b_ref[...]


def _adaln(c, w_ada, b_ada):
    B, D = c.shape
    E = w_ada.shape[1]
    cp = jnp.zeros((8, D), F32).at[:B].set(c)
    out = pl.pallas_call(
        _mod_body,
        grid=(E // D,),
        in_specs=[pl.BlockSpec((8, D), lambda j: (0, 0)),
                  pl.BlockSpec((D, D), lambda j: (0, j)),
                  pl.BlockSpec((1, D), lambda j: (0, j))],
        out_specs=pl.BlockSpec((8, D), lambda j: (0, j)),
        out_shape=jax.ShapeDtypeStruct((8, E), F32),
        compiler_params=_cparams(("arbitrary",)),
        name="adaln_mod",
    )(cp, w_ada, b_ada.reshape(1, E))
    return out[:B]


PW = (512, 768, 512, 128, 512, 128, 2048, 128)
PO = tuple(int(v) for v in np.cumsum((0,) + PW))


def _heads_to_lanes(x, nh):
    cols = []
    for p in range(nh // 2):
        t = x[:, p * LANE:(p + 1) * LANE].T
        cols += [t[0:NSA_HEAD_DIM], t[NSA_HEAD_DIM:LANE]]
    return jnp.concatenate(cols, axis=1)


def _lanes_to_heads(y, nh):
    outs = []
    for p in range(nh // 2):
        pair = jnp.concatenate([y[:, (2 * p) * LANE:(2 * p + 1) * LANE],
                                y[:, (2 * p + 1) * LANE:(2 * p + 2) * LANE]], axis=0)
        outs.append(pair.T)
    return jnp.concatenate(outs, axis=1)


def _proj_body(x_ref, g_ref, sc_ref, sh_ref, w_ref, kn_ref,
               qn_ref, kv_ref, qd_ref, ckv_ref, qi_ref, sm_ref, gm_ref, ki_ref):
    x = x_ref[...]
    ms = jnp.mean(x * x, axis=-1, keepdims=True)
    h = x * lax.rsqrt(ms + RMS_EPS) * g_ref[...]
    h = h * (1.0 + sc_ref[...]) + sh_ref[...]
    hb = h.astype(BF16)

    def mm(k):
        return jnp.dot(hb, w_ref[:, PO[k]:PO[k + 1]], preferred_element_type=F32)

    qn_ref[...] = (mm(0) * (NSA_HEAD_DIM ** -0.5 * LOG2E)).astype(BF16)
    kv_ref[...] = mm(1).astype(BF16)
    qd_ref[...] = mm(2).astype(BF16)
    c = mm(3)
    cms = jnp.mean(c * c, axis=-1, keepdims=True)
    ckv_ref[...] = (c * lax.rsqrt(cms + RMS_EPS) * kn_ref[...]).astype(BF16)
    qi_ref[...] = (mm(4) * IDX_DIM ** -0.5).astype(BF16)
    sm_ref[...] = mm(5)
    gm_ref[...] = mm(6)
    ki_ref[...] = mm(7).astype(BF16)


def _input_proj(x, g_pre, sc, sh, w_in, kv_norm):
    B, T, D = x.shape
    TM = 512 if T % 512 == 0 else TQ
    o = PROJ_OFFS
    small = jnp.concatenate([w_in[:, o[2]:o[3]], w_in[:, o[6]:o[7]], w_in[:, o[7]:o[8]],
                             jnp.zeros((D, 32), F32)], axis=1)
    w_r = jnp.concatenate([w_in[:, o[0]:o[1]], w_in[:, o[1]:o[2]], w_in[:, o[3]:o[4]],
                           w_in[:, o[4]:o[5]], w_in[:, o[5]:o[6]], small,
                           w_in[:, o[8]:o[9]], w_in[:, o[6]:o[7]],
                           jnp.zeros((D, LANE - IDX_DIM), F32)], axis=1).astype(BF16)
    row = lambda w: pl.BlockSpec((None, TM, w), lambda b, i: (b, i, 0))
    vec = pl.BlockSpec((None, 1, D), lambda b, i: (b, 0, 0))
    outs = pl.pallas_call(
        _proj_body,
        grid=(B, T // TM),
        in_specs=[row(D),
                  pl.BlockSpec((1, D), lambda b, i: (0, 0)),
                  vec, vec,
                  pl.BlockSpec((D, PO[-1]), lambda b, i: (0, 0)),
                  pl.BlockSpec((1, DSA_KV_LATENT), lambda b, i: (0, 0))],
        out_specs=[row(w) for w in PW],
        out_shape=[jax.ShapeDtypeStruct((B, T, PW[0]), BF16),
                   jax.ShapeDtypeStruct((B, T, PW[1]), BF16),
                   jax.ShapeDtypeStruct((B, T, PW[2]), BF16),
                   jax.ShapeDtypeStruct((B, T, PW[3]), BF16),
                   jax.ShapeDtypeStruct((B, T, PW[4]), BF16),
                   jax.ShapeDtypeStruct((B, T, PW[5]), F32),
                   jax.ShapeDtypeStruct((B, T, PW[6]), F32),
                   jax.ShapeDtypeStruct((B, T, PW[7]), BF16)],
        compiler_params=_cparams(("arbitrary", "arbitrary")),
        name="prenorm_proj",
    )(x, g_pre.reshape(1, D), sc.reshape(B, 1, D), sh.reshape(B, 1, D), w_r,
      kv_norm.reshape(1, DSA_KV_LATENT))
    return outs


def _t5_bucket(dist):
    n = jnp.maximum(dist, 0)
    exact = REL_BUCKETS // 2
    nf = jnp.maximum(n, 1).astype(F32)
    large = exact + (jnp.log(nf / exact) / math.log(REL_MAX_DISTANCE / exact)
                     * (REL_BUCKETS - exact)).astype(I32)
    return jnp.where(n < exact, n, jnp.minimum(large, REL_BUCKETS - 1))


def _delta(tab, dist):
    onehot = (_t5_bucket(dist)[..., None] == jnp.arange(REL_BUCKETS)).astype(F32)
    d = jnp.einsum('...b,bh->...h', onehot, tab - tab[REL_BUCKETS - 1],
                   precision=lax.Precision.HIGHEST)
    return jnp.where((dist >= 0)[..., None], d * LOG2E, NEG_INF)


def _near_tables(tab):
    H = tab.shape[1]
    c = jnp.arange(TQ)[:, None]
    q = jnp.arange(TQ)[None, :]
    tabs = [jnp.full((TQ, TQ, H), NEG_INF, F32)]
    for diff in (0, 1):
        tabs.append(_delta(tab, diff * TQ + q - c))
    tabs += [jnp.zeros((TQ, TQ, H), F32)] * 2
    t = jnp.stack(tabs)
    return t.transpose(0, 1, 3, 2).reshape(5, TQ, H * TQ)


def _cmp_body(r_ref, w1_ref, c_ref, w2_ref, o_ref):
    y = jnp.dot(r_ref[...], w1_ref[...], preferred_element_type=F32)
    M = y.shape[0]
    z = y[:, :CMP_HIDDEN] + pltpu.roll(y[:, CMP_HIDDEN:], M - 1, 0) + c_ref[...]
    z = jax.nn.gelu(z)
    o_ref[...] = jnp.dot(z.astype(BF16), w2_ref[...], preferred_element_type=F32)


def _compress(kv6, cmp_pe, cmp_w1, cmp_b1, cmp_w2):
    B, T = kv6.shape[:2]
    G, DH = NSA_KV_GROUPS, NSA_HEAD_DIM
    M = T // CMP_STRIDE
    half = CMP_STRIDE * DH
    r = kv6[:, :, 0:2].transpose(0, 2, 3, 1, 4).reshape(B, 2, G, M, half)
    w1 = cmp_w1.reshape(2, 2, half, CMP_HIDDEN)
    w1c = jnp.concatenate([w1[:, 0], w1[:, 1]], axis=-1).astype(BF16)
    cvec = (jnp.einsum('wld,wldk->wk', cmp_pe, cmp_w1, precision=lax.Precision.HIGHEST)
            + cmp_b1).reshape(2, 1, CMP_HIDDEN)
    return pl.pallas_call(
        _cmp_body,
        grid=(B, 2, G),
        in_specs=[pl.BlockSpec((None, None, None, M, half), lambda b, w, g: (b, w, g, 0, 0)),
                  pl.BlockSpec((None, half, 2 * CMP_HIDDEN), lambda b, w, g: (w, 0, 0)),
                  pl.BlockSpec((None, 1, CMP_HIDDEN), lambda b, w, g: (w, 0, 0)),
                  pl.BlockSpec((None, CMP_HIDDEN, DH), lambda b, w, g: (w, 0, 0))],
        out_specs=pl.BlockSpec((None, None, None, M, DH), lambda b, w, g: (b, w, g, 0, 0)),
        out_shape=jax.ShapeDtypeStruct((B, 2, G, M, DH), F32),
        compiler_params=_cparams(("arbitrary",) * 3),
        name="kv_compress",
    )(r, w1c, cvec, cmp_w2.astype(BF16))


def _nsa_cmp_body(q_ref, kc_ref, vct_ref, ct_ref, ovl_ref, oc_ref, sel_ref, qt_ref, s_ref,
                  *, n_blk, topn):
    i = pl.program_id(2)
    L = NSA_HPG * TQ
    qt = jnp.concatenate([_heads_to_lanes(q_ref[...].astype(F32), NSA_HPG), jnp.ones((1, L), F32),
                          jnp.zeros((LANE - NSA_HEAD_DIM - 1, L), F32)], axis=0).astype(BF16)
    qt_ref[...] = qt
    s_ref[...] = jnp.dot(kc_ref[...], qt, preferred_element_type=F32)
    near = pl.ds(pl.multiple_of(i * 8, 8), CMP_NEAR)
    s_ref[near, :] = s_ref[near, :] + ct_ref[...]
    s = s_ref[...]
    MP = s.shape[0]
    row = lax.broadcasted_iota(I32, (MP, 1), 0)
    s = jnp.where(row < i * 8 + CMP_NEAR, s, NEG_INF)
    m = jnp.maximum(jnp.max(s, axis=0, keepdims=True), M_FLOOR)
    p = jnp.exp2(s - m)
    l = jnp.sum(p, axis=0, keepdims=True)
    p = p / jnp.maximum(l, 1e-30)
    pb = p.astype(BF16)
    oc_ref[...] = jnp.dot(vct_ref[...], pb, preferred_element_type=F32)[:NSA_HEAD_DIM]
    psum = p[:, 0:TQ]
    for hh in range(1, NSA_HPG):
        psum = psum + p[:, hh * TQ:(hh + 1) * TQ]
    imp = jnp.dot(ovl_ref[...], psum.astype(BF16), preferred_element_type=F32)

    blk = lax.broadcasted_iota(I32, (LANE, 1), 0)
    t = i * TQ + lax.broadcasted_iota(I32, (1, TQ), 1)
    cur = t >> int(math.log2(SLC_BLOCK))
    admiss = blk <= cur
    bonus = jnp.where(blk == 0, FORCE_BONUS, 0.0)
    bonus = jnp.where(blk == cur, FORCE_BONUS, bonus)
    bonus = jnp.where(blk == cur - 1, FORCE_BONUS, bonus)
    score = jnp.where(admiss, imp + bonus, NEG_INF)
    score = jnp.where(blk < n_blk, score, KNOCK)
    blk_b = jnp.broadcast_to(blk, (LANE, TQ))
    selneg = jnp.full((LANE, TQ), NEG_INF, F32)
    for _ in range(topn):
        mx = jnp.max(score, axis=0, keepdims=True)
        first = jnp.min(jnp.where(score == mx, blk_b, LANE), axis=0, keepdims=True)
        pick = blk_b == first
        selneg = jnp.where(pick, 0.0, selneg)
        score = jnp.where(pick, KNOCK, score)
    sel_ref[...] = jnp.where(admiss, selneg, NEG_INF).astype(BF16)


def _nsa_compressed(qn, kc_aug, vct, ct, ovl, n_blk):
    B, T = qn.shape[:2]
    G, nQ = NSA_KV_GROUPS, T // TQ
    MP = kc_aug.shape[2]
    body = functools.partial(_nsa_cmp_body, n_blk=n_blk, topn=min(SLC_TOPN, n_blk))
    qtile = pl.BlockSpec((None, None, None, LANE, NSA_HPG * TQ), lambda b, g, i: (b, g, i, 0, 0))
    return pl.pallas_call(
        body,
        grid=(B, G, nQ),
        in_specs=[pl.BlockSpec((None, TQ, NSA_HPG * NSA_HEAD_DIM), lambda b, g, i: (b, i, g)),
                  pl.BlockSpec((None, None, MP, LANE), lambda b, g, i: (b, g, 0, 0)),
                  pl.BlockSpec((None, None, LANE, MP), lambda b, g, i: (b, g, 0, 0)),
                  pl.BlockSpec((None, CMP_NEAR, NSA_HPG * TQ), lambda b, g, i: (g, 0, 0)),
                  pl.BlockSpec((LANE, MP), lambda b, g, i: (0, 0))],
        out_specs=[pl.BlockSpec((None, None, None, NSA_HEAD_DIM, NSA_HPG * TQ), lambda b, g, i: (b, g, i, 0, 0)),
                   pl.BlockSpec((None, None, None, LANE, TQ), lambda b, g, i: (b, g, i, 0, 0)),
                   qtile],
        out_shape=[jax.ShapeDtypeStruct((B, G, nQ, NSA_HEAD_DIM, NSA_HPG * TQ), F32),
                   jax.ShapeDtypeStruct((B, G, nQ, LANE, TQ), BF16),
                   jax.ShapeDtypeStruct((B, G, nQ, LANE, NSA_HPG * TQ), BF16)],
        scratch_shapes=[pltpu.VMEM((MP, NSA_HPG * TQ), F32)],
        compiler_params=_cparams(("arbitrary",) * 3),
        name="nsa_compressed",
    )(qn, kc_aug, vct, ct, ovl)


def _softmax_step(s, vt, m_old, acc):
    m_new = jnp.maximum(m_old, jnp.max(s, axis=0, keepdims=True))
    alpha = jnp.exp2(m_old - m_new)
    p = jnp.exp2(s - m_new).astype(BF16)
    return m_new, acc * alpha + jnp.dot(vt, p, preferred_element_type=F32)


AK = 256


def _pipeline(lo, hi, produce, consume, buf_a, buf_b):
    n = hi - lo

    @pl.when(n > 0)
    def _():
        produce(lo, buf_a)

        def body(p, c):
            j = lo + 2 * p
            produce(j + 1, buf_b)
            consume(j, buf_a)
            produce(j + 2, buf_a)
            consume(j + 1, buf_b)
            return c

        lax.fori_loop(0, (n - 1) // 2, body, 0)

        @pl.when(n % 2 == 1)
        def _():
            consume(hi - 1, buf_a)

        @pl.when(n % 2 == 0)
        def _():
            produce(hi - 1, buf_b)
            consume(hi - 2, buf_a)
            consume(hi - 1, buf_b)


def _attend(lo, hi, lhs_fn, rhs_ref, value_fn, m_ref, acc_ref, sa_ref, sb_ref, near_fn=None):
    def scores(j, buf):
        s = jnp.dot(lhs_fn(j), rhs_ref[...], preferred_element_type=F32)
        buf[...] = s if near_fn is None else near_fn(s, j)

    def consume(j, buf):
        m, acc = _softmax_step(buf[...], value_fn(j), m_ref[...], acc_ref[...])
        m_ref[...] = m
        acc_ref[...] = acc

    _pipeline(lo, hi, scores, consume, sa_ref, sb_ref)


def _near_add(dt_ref, i, s, j):
    parts = []
    for sub in range(AK // TQ):
        idx = jnp.clip(i - (j * (AK // TQ) + sub) + 1, 0, 4)
        parts.append(s[sub * TQ:(sub + 1) * TQ] + dt_ref[idx])
    return jnp.concatenate(parts, axis=0)


def _nsa_sel_body(qt_ref, sel_ref, ks_ref, vst_ref, dt_ref, o_ref, qa_ref, m_ref, acc_ref,
                  sa_ref, sb_ref):
    i = pl.program_id(2)
    L = NSA_HPG * TQ
    qa_ref[0:LANE, :] = qt_ref[...]
    selneg = sel_ref[...]
    qa_ref[LANE:2 * LANE, :] = jnp.concatenate([selneg] * NSA_HPG, axis=1)
    m_ref[...] = jnp.full((1, L), M_FLOOR, F32)
    acc_ref[...] = jnp.zeros(acc_ref.shape, F32)
    last = i // (AK // TQ)
    n_far = jnp.maximum(last - 1, 0)

    def keys(j):
        return ks_ref[pl.ds(pl.multiple_of(j * AK, AK), AK), :]

    def value(j):
        return vst_ref[:, pl.ds(pl.multiple_of(j * AK, AK), AK)]

    _attend(0, n_far, keys, qa_ref, value, m_ref, acc_ref, sa_ref, sb_ref)
    _attend(n_far, last + 1, keys, qa_ref, value, m_ref, acc_ref, sa_ref, sb_ref,
            near_fn=functools.partial(_near_add, dt_ref, i))
    acc = acc_ref[...]
    o_ref[...] = acc[:NSA_HEAD_DIM] / acc[NSA_HEAD_DIM:NSA_HEAD_DIM + 1]


def _nsa_selected(qt, selneg, ks_aug, vst, dt):
    B, G, nQ = qt.shape[:3]
    T = ks_aug.shape[2]
    R = vst.shape[2]
    L = NSA_HPG * TQ
    return pl.pallas_call(
        _nsa_sel_body,
        grid=(B, G, nQ),
        in_specs=[pl.BlockSpec((None, None, None, LANE, L), lambda b, g, i: (b, g, i, 0, 0)),
                  pl.BlockSpec((None, None, None, LANE, TQ), lambda b, g, i: (b, g, i, 0, 0)),
                  pl.BlockSpec((None, None, T, 2 * LANE), lambda b, g, i: (b, g, 0, 0)),
                  pl.BlockSpec((None, None, R, T), lambda b, g, i: (b, g, 0, 0)),
                  pl.BlockSpec((None, 5, TQ, L), lambda b, g, i: (g, 0, 0, 0))],
        out_specs=pl.BlockSpec((None, None, None, NSA_HEAD_DIM, L), lambda b, g, i: (b, g, i, 0, 0)),
        out_shape=jax.ShapeDtypeStruct((B, G, nQ, NSA_HEAD_DIM, L), F32),
        scratch_shapes=[pltpu.VMEM((2 * LANE, L), BF16),
                        pltpu.VMEM((1, L), F32),
                        pltpu.VMEM((R, L), F32),
                        pltpu.VMEM((AK, L), F32),
                        pltpu.VMEM((AK, L), F32)],
        compiler_params=_cparams(("arbitrary",) * 3),
        name="nsa_selected",
    )(qt, selneg, ks_aug, vst, dt)


def _nsa_win_body(qt_ref, kw_ref, vwt_ref, w0_ref, dt_ref, oc_ref, os_ref, gl_ref, o_ref):
    i = pl.program_id(2)
    span = WINDOW + TQ
    k0 = pl.multiple_of(i * TQ, TQ)
    s = jnp.dot(kw_ref[pl.ds(k0, span), :], qt_ref[...], preferred_element_type=F32)
    s = jnp.concatenate([s[0:TQ] + w0_ref[...], s[TQ:WINDOW - TQ],
                         s[WINDOW - TQ:WINDOW] + dt_ref[2], s[WINDOW:span] + dt_ref[1]], axis=0)
    m = jnp.maximum(jnp.max(s, axis=0, keepdims=True), M_FLOOR)
    p = jnp.exp2(s - m).astype(BF16)
    acc = jnp.dot(vwt_ref[:, pl.ds(k0, span)], p, preferred_element_type=F32)
    ow = acc[:NSA_HEAD_DIM] / acc[NSA_HEAD_DIM:NSA_HEAD_DIM + 1]
    g = jax.nn.sigmoid(gl_ref[...])
    o_t = g[0:1] * oc_ref[...] + g[1:2] * os_ref[...] + g[2:3] * ow
    o_ref[...] = _lanes_to_heads(o_t, NSA_HPG).astype(BF16)


def _nsa_window(qt, kw_aug, vwt, w0, dt, oc, osel, gl):
    B, G, nQ = qt.shape[:3]
    TP = kw_aug.shape[2]
    R = vwt.shape[2]
    L = NSA_HPG * TQ
    W = NSA_HPG * NSA_HEAD_DIM
    tile = lambda r: pl.BlockSpec((None, None, None, r, L), lambda b, g, i: (b, g, i, 0, 0))
    return pl.pallas_call(
        _nsa_win_body,
        grid=(B, G, nQ),
        in_specs=[tile(LANE),
                  pl.BlockSpec((None, None, TP, LANE), lambda b, g, i: (b, g, 0, 0)),
                  pl.BlockSpec((None, None, R, TP), lambda b, g, i: (b, g, 0, 0)),
                  pl.BlockSpec((TQ, L), lambda b, g, i: (0, 0)),
                  pl.BlockSpec((None, 5, TQ, L), lambda b, g, i: (g, 0, 0, 0)),
                  tile(NSA_HEAD_DIM), tile(NSA_HEAD_DIM), tile(8)],
        out_specs=pl.BlockSpec((None, TQ, W), lambda b, g, i: (b, i, g)),
        out_shape=jax.ShapeDtypeStruct((B, nQ * TQ, G * W), BF16),
        compiler_params=_cparams(("arbitrary",) * 3),
        name="nsa_window",
    )(qt, kw_aug, vwt, w0, dt, oc, osel, gl)


INT_MIN = -2 ** 31
CK = 2 * NK
KEY_BITS = 16
LATE_BITS = 4


def _dsa_body(qd_ref, qi_ref, wt_ref, ki_ref, ckv_ref, ckvt_ref, wuk_ref, wuvt_ref, dt_ref,
              o_ref, sc_ref, qb_ref, m_ref, acc_ref, sa_ref, sb_ref, *, k_top):
    i = pl.program_id(1)
    H = DSA_HEADS
    L = H * TQ
    last = i // (NK // TQ)
    n_steps = last + 1
    t_row = i * TQ + lax.broadcasted_iota(I32, (1, TQ), 1)

    qit = jnp.concatenate([_heads_to_lanes(qi_ref[...].astype(F32), IDX_HEADS),
                           jnp.zeros((LANE - IDX_DIM, L), F32)], axis=0).astype(BF16)
    wt = wt_ref[...]

    def idx_scores(j, buf):
        k0 = pl.multiple_of(j * NK, NK)
        buf[...] = jnp.dot(ki_ref[pl.ds(k0, NK), :], qit, preferred_element_type=F32)

    def idx_reduce(j, buf):
        k0 = pl.multiple_of(j * NK, NK)
        r = jnp.maximum(buf[...], 0.0) * wt
        sc = r[:, 0:TQ]
        for h in range(1, H):
            sc = sc + r[:, h * TQ:(h + 1) * TQ]
        bits = pltpu.bitcast(sc, I32)
        ik = jnp.where(sc == 0.0, 0, bits ^ ((bits >> 31) & 0x7FFFFFFF))
        key = k0 + lax.broadcasted_iota(I32, (NK, 1), 0)
        sc_ref[pl.ds(k0, NK), :] = jnp.where(key <= t_row, jnp.maximum(ik, INT_MIN + 1), INT_MIN)

    _pipeline(0, n_steps, idx_scores, idx_reduce, sa_ref, sb_ref)

    @pl.when(n_steps % 2 == 1)
    def _():
        sc_ref[pl.ds(pl.multiple_of(n_steps * NK, NK), NK), :] = jnp.full((NK, TQ), INT_MIN, I32)

    n_chunks = (n_steps + 1) // 2

    def count(pred):
        def cbody(j, cnt):
            k0 = pl.multiple_of(j * CK, CK)
            ind = jnp.where(pred(sc_ref[pl.ds(k0, CK), :], k0), 1.0, 0.0)
            parts = [ind[r * 8:(r + 1) * 8] for r in range(CK // 8)]
            while len(parts) > 1:
                parts = [a + b for a, b in zip(parts[0::2], parts[1::2])]
            return cnt + parts[0]
        cnt8 = lax.fori_loop(0, n_chunks, cbody, jnp.zeros((8, TQ), F32))
        return jnp.sum(cnt8, axis=0, keepdims=True)

    def any_lane(cond):
        return jnp.max(jnp.where(cond, 1.0, 0.0)) > 0.0

    kf = float(k_top)
    need = t_row + 1 > k_top

    def bit_body(b, c):
        tu, ct = c
        cand = tu | jnp.left_shift(jnp.int32(1), 31 - b)
        thr_c = cand ^ INT_MIN
        cnt = count(lambda blk, k0: blk >= thr_c)
        take = cnt >= kf
        return jnp.where(take, cand, tu), jnp.where(take, cnt, ct)

    carry = lax.fori_loop(0, 32 - LATE_BITS, bit_body,
                          (jnp.zeros((1, TQ), I32), (t_row + 1).astype(F32)))
    tu, ct = lax.cond(any_lane(need & (carry[1] != kf)),
                      lambda c: lax.fori_loop(32 - LATE_BITS, 32, bit_body, c),
                      lambda c: c, carry)
    thr = jnp.maximum(tu ^ INT_MIN, INT_MIN + 1)

    @pl.when(any_lane(need & (ct > kf)))
    def _():
        keep = kf - count(lambda blk, k0: blk > thr)

        def tie_count(bound):
            def pred(blk, k0):
                key = k0 + lax.broadcasted_iota(I32, (CK, 1), 0)
                return jnp.where(key < bound, blk, INT_MIN) == thr
            return count(pred)

        def pos_body(b, pos):
            cand = pos | jnp.left_shift(jnp.int32(1), KEY_BITS - 1 - b)
            return jnp.where(tie_count(cand) < keep, cand, pos)

        pos = lax.fori_loop(0, KEY_BITS, pos_body, jnp.zeros((1, TQ), I32))

        def fix(j, c):
            k0 = pl.multiple_of(j * CK, CK)
            blk = sc_ref[pl.ds(k0, CK), :]
            key = k0 + lax.broadcasted_iota(I32, (CK, 1), 0)
            drop = jnp.where(key > pos, blk, INT_MIN) == thr
            sc_ref[pl.ds(k0, CK), :] = jnp.where(drop, INT_MIN, blk)
            return c

        lax.fori_loop(0, n_chunks, fix, 0)

    qdt = _heads_to_lanes(qd_ref[...].astype(F32), H).astype(BF16)
    for h in range(H):
        ql = jnp.dot(wuk_ref[h], qdt[:, h * TQ:(h + 1) * TQ], preferred_element_type=F32)
        qb_ref[0:LANE, h * TQ:(h + 1) * TQ] = (ql * (DSA_HEAD_DIM ** -0.5 * LOG2E)).astype(BF16)
    eye = (lax.broadcasted_iota(I32, (LANE, TQ), 0) == lax.broadcasted_iota(I32, (LANE, TQ), 1))
    eye = jnp.where(eye, 1.0, 0.0).astype(BF16)
    qb_ref[LANE:2 * LANE, :] = jnp.concatenate([eye] * H, axis=1)
    m_ref[...] = jnp.full((1, L), M_FLOOR, F32)
    acc_ref[...] = jnp.zeros(acc_ref.shape, F32)
    alast = i // (AK // TQ)
    n_far = jnp.maximum(alast - 1, 0)

    def keys(j):
        k0 = pl.multiple_of(j * AK, AK)
        nb = jnp.where(sc_ref[pl.ds(k0, AK), :] >= thr, 0.0, NEG_INF).astype(BF16)
        return jnp.concatenate([ckv_ref[pl.ds(k0, AK), :], nb], axis=1)

    def value(j):
        return ckvt_ref[:, pl.ds(pl.multiple_of(j * AK, AK), AK)]

    _attend(0, n_far, keys, qb_ref, value, m_ref, acc_ref, sa_ref, sb_ref)
    _attend(n_far, alast + 1, keys, qb_ref, value, m_ref, acc_ref, sa_ref, sb_ref,
            near_fn=functools.partial(_near_add, dt_ref, i))
    acc = acc_ref[...]
    olat = (acc[:DSA_KV_LATENT] / acc[DSA_KV_LATENT:DSA_KV_LATENT + 1]).astype(BF16)
    o_t = jnp.concatenate([jnp.dot(wuvt_ref[h], olat[:, h * TQ:(h + 1) * TQ],
                                   preferred_element_type=F32) for h in range(H)], axis=1)
    o_ref[...] = _lanes_to_heads(o_t, H).astype(BF16)


def _dsa(qd, qi, wt, ki, ckv, ckvt, wuk, wuvt, dt, k_top):
    B, T = ki.shape[:2]
    nQ = T // TQ
    H = DSA_HEADS
    L = H * TQ
    W = H * DSA_HEAD_DIM
    R = ckvt.shape[1]
    rows = pl.BlockSpec((None, TQ, W), lambda b, i: (b, i, 0))
    return pl.pallas_call(
        functools.partial(_dsa_body, k_top=k_top),
        grid=(B, nQ),
        in_specs=[rows, rows, pl.BlockSpec((None, None, 1, L), lambda b, i: (b, i, 0, 0)),
                  pl.BlockSpec((None, T, LANE), lambda b, i: (b, 0, 0)),
                  pl.BlockSpec((None, T, DSA_KV_LATENT), lambda b, i: (b, 0, 0)),
                  pl.BlockSpec((None, R, T), lambda b, i: (b, 0, 0)),
                  pl.BlockSpec((H, DSA_KV_LATENT, DSA_HEAD_DIM), lambda b, i: (0, 0, 0)),
                  pl.BlockSpec((H, DSA_HEAD_DIM, DSA_KV_LATENT), lambda b, i: (0, 0, 0)),
                  pl.BlockSpec((5, TQ, L), lambda b, i: (0, 0, 0))],
        out_specs=rows,
        out_shape=jax.ShapeDtypeStruct((B, T, W), BF16),
        scratch_shapes=[pltpu.VMEM((T, TQ), I32),
                        pltpu.VMEM((2 * LANE, L), BF16),
                        pltpu.VMEM((1, L), F32),
                        pltpu.VMEM((R, L), F32),
                        pltpu.VMEM((AK, L), F32),
                        pltpu.VMEM((AK, L), F32)],
        compiler_params=_cparams(("arbitrary",) * 2),
        name="dsa_attention",
    )(qd, qi, wt, ki, ckv, ckvt, wuk, wuvt, dt)


def _token_mixers(qn, kvn, qd, ckv, qi, small, ki, rel_bias, cmp_pe, cmp_w1, cmp_b1, cmp_w2,
                  dsa_w_uk, dsa_w_uv):
    B, T = qn.shape[:2]
    G, HPG, DH = NSA_KV_GROUPS, NSA_HPG, NSA_HEAD_DIM
    nQ = T // TQ
    n_blk = T // SLC_BLOCK
    L = HPG * TQ
    assert T % NK == 0 and n_blk <= LANE
    kv6 = kvn.reshape(B, T, 6, G, DH)

    tab_nsa = rel_bias[:, :NSA_HEADS]
    dt_nsa = _near_tables(tab_nsa).reshape(5, TQ, G, L).transpose(2, 0, 1, 3)
    dt_dsa = _near_tables(rel_bias[:, NSA_HEADS:])

    kvc = _compress(kv6, cmp_pe, cmp_w1, cmp_b1, cmp_w2)
    M = T // CMP_STRIDE
    MP = M + CMP_PAD
    kc = kvc[:, 0].astype(BF16)
    flag = jnp.concatenate([jnp.full((CMP_PAD, 1), NEG_INF, F32), jnp.zeros((M, 1), F32)]).astype(BF16)
    kc_aug = jnp.concatenate(
        [jnp.pad(kc, ((0, 0), (0, 0), (CMP_PAD, 0), (0, 0))),
         jnp.broadcast_to(flag, (B, G, MP, 1)),
         jnp.zeros((B, G, MP, LANE - DH - 1), BF16)], axis=-1)
    vct = jnp.pad(kvc[:, 1].astype(BF16).transpose(0, 1, 3, 2),
                  ((0, 0), (0, 0), (0, LANE - DH), (CMP_PAD, 0)))
    mrow = jnp.arange(CMP_NEAR)[:, None]
    qcol = jnp.arange(TQ)[None, :]
    ct = _delta(tab_nsa, qcol - CMP_STRIDE * mrow + (2 * TQ - CMP_BLOCK + 1))
    ct = ct.reshape(CMP_NEAR, TQ, G, HPG).transpose(2, 0, 3, 1).reshape(G, CMP_NEAR, L)
    n_cmp = (T - CMP_BLOCK) // CMP_STRIDE + 1
    cs = jnp.arange(M) * CMP_STRIDE
    ss = jnp.arange(LANE) * SLC_BLOCK
    ovl = (jnp.clip(jnp.minimum(cs[None, :] + CMP_BLOCK, ss[:, None] + SLC_BLOCK)
                    - jnp.maximum(cs[None, :], ss[:, None]), 0, None).astype(F32) / CMP_BLOCK)
    ovl = jnp.where((jnp.arange(M)[None, :] < n_cmp) & (jnp.arange(LANE)[:, None] < n_blk), ovl, 0.0)
    ovl = jnp.pad(ovl, ((0, 0), (CMP_PAD, 0))).astype(BF16)
    oc, selneg, qt = _nsa_compressed(qn, kc_aug, vct, ct, ovl, n_blk)

    ks = kv6[:, :, 2].transpose(0, 2, 1, 3)
    et = (jnp.arange(T)[:, None] // SLC_BLOCK == jnp.arange(LANE)[None, :]).astype(BF16)
    ks_aug = jnp.concatenate([ks, jnp.zeros((B, G, T, LANE - DH), BF16),
                              jnp.broadcast_to(et, (B, G, T, LANE))], axis=-1)
    vs = kv6[:, :, 3].transpose(0, 2, 3, 1)
    vst = jnp.concatenate([vs, jnp.ones((B, G, 1, T), BF16),
                           jnp.zeros((B, G, V_ROWS - DH - 1, T), BF16)], axis=2)
    osel = _nsa_selected(qt, selneg, ks_aug, vst, dt_nsa)

    TP = T + WINDOW
    kw = jnp.pad(kv6[:, :, 4].transpose(0, 2, 1, 3), ((0, 0), (0, 0), (WINDOW, 0), (0, 0)))
    wflag = jnp.concatenate([jnp.full((WINDOW, 1), NEG_INF, F32), jnp.zeros((T, 1), F32)]).astype(BF16)
    kw_aug = jnp.concatenate([kw, jnp.broadcast_to(wflag, (B, G, TP, 1)),
                              jnp.zeros((B, G, TP, LANE - DH - 1), BF16)], axis=-1)
    vw = jnp.pad(kv6[:, :, 5].transpose(0, 2, 3, 1), ((0, 0), (0, 0), (0, 0), (WINDOW, 0)))
    vwt = jnp.concatenate([vw, jnp.ones((B, G, 1, TP), BF16),
                           jnp.zeros((B, G, V_ROWS - DH - 1, TP), BF16)], axis=2)
    u = jnp.arange(TQ)[:, None]
    w0 = jnp.tile(jnp.where(u > jnp.arange(TQ)[None, :], 0.0, NEG_INF).astype(F32), (1, HPG))
    gl = small[..., 0:3 * NSA_HEADS].reshape(B, nQ, TQ, G, HPG, 3).transpose(0, 3, 1, 5, 4, 2)
    gl = jnp.pad(gl.reshape(B, G, nQ, 3, L), ((0, 0), (0, 0), (0, 0), (0, 5), (0, 0)))
    o_nsa = _nsa_window(qt, kw_aug, vwt, w0, dt_nsa, oc, osel, gl)

    H = DSA_HEADS
    w_idx = small[..., 24 + IDX_DIM:24 + IDX_DIM + IDX_HEADS] * IDX_HEADS ** -0.5
    wt = w_idx.reshape(B, nQ, TQ, IDX_HEADS).transpose(0, 1, 3, 2).reshape(B, nQ, 1, IDX_HEADS * TQ)
    ckvt = jnp.concatenate([ckv.transpose(0, 2, 1), jnp.ones((B, 1, T), BF16),
                            jnp.zeros((B, 7, T), BF16)], axis=1)
    wuk = dsa_w_uk.transpose(1, 0, 2).astype(BF16)
    wuvt = dsa_w_uv.transpose(1, 2, 0).astype(BF16)
    o_dsa = _dsa(qd, qi, wt, ki, ckv, ckvt, wuk, wuvt, dt_dsa, min(IDX_TOPK_MAX, T // 4))
    return o_nsa, o_dsa


def _rms(v, g):
    return v * lax.rsqrt(jnp.mean(v * v, axis=-1, keepdims=True) + RMS_EPS) * g


HI_MASK = -65536


def _pack_halves(v):
    w = v.shape[1] // 2
    lo = lax.shift_right_logical(pltpu.bitcast(v[:, :w], I32), 16)
    hi = pltpu.bitcast(v[:, w:], I32) & HI_MASK
    return lo | hi


def _unpack_halves(p):
    return pltpu.bitcast(p << 16, F32), pltpu.bitcast(p & HI_MASK, F32)


def _merge_body(on_ref, od_ref, gm_ref, x_ref, wn_ref, wd_ref, wo_ref, gpost_ref, gt_ref,
                gpre_ref, sc_ref, sh_ref, rwt_ref, x1_ref, h2_ref, hpa_ref, hpb_ref, lt_ref):
    D = D_MODEL
    ya = jnp.dot(on_ref[...], wn_ref[...], preferred_element_type=F32)
    yb = jnp.dot(od_ref[...], wd_ref[...], preferred_element_type=F32)
    y = jax.nn.sigmoid(gm_ref[:, 0:D]) * ya + jax.nn.sigmoid(gm_ref[:, D:2 * D]) * yb
    y2 = jnp.dot(y.astype(BF16), wo_ref[...], preferred_element_type=F32)
    x1 = x_ref[...] + gt_ref[...] * _rms(y2, gpost_ref[...])
    x1_ref[...] = x1
    h2 = (_rms(x1, gpre_ref[...]) * (1.0 + sc_ref[...]) + sh_ref[...]).astype(BF16)
    h2_ref[...] = h2
    packed = _pack_halves(h2.astype(F32))
    hpa_ref[...] = packed[:, :SC_WIDTH]
    hpb_ref[...] = packed[:, SC_WIDTH:]
    lt_ref[...] = lax.dot_general(rwt_ref[...], h2, (((1,), (1,)), ((), ())),
                                  preferred_element_type=F32)


def _merge(o_nsa, o_dsa, gm, x, wn, wd, wo, g_post, gt1, g_pre, sc2, sh2, router_w):
    B, T, D = x.shape
    TM = 256
    E = router_w.shape[1]
    row = lambda w: pl.BlockSpec((None, TM, w), lambda b, i: (b, i, 0))
    full = lambda a: pl.BlockSpec(a.shape, lambda b, i: (0,) * a.ndim)
    vecb = pl.BlockSpec((None, 1, D), lambda b, i: (b, 0, 0))
    vec = pl.BlockSpec((1, D), lambda b, i: (0, 0))
    wn, wd, wo = wn.astype(BF16), wd.astype(BF16), wo.astype(BF16)
    rwt = router_w.T.astype(BF16)
    return pl.pallas_call(
        _merge_body,
        grid=(B, T // TM),
        in_specs=[row(o_nsa.shape[-1]), row(o_dsa.shape[-1]), row(2 * D), row(D),
                  full(wn), full(wd), full(wo), vec, vecb, vec, vecb, vecb, full(rwt)],
        out_specs=[row(D), row(D), row(SC_WIDTH), row(SC_WIDTH),
                   pl.BlockSpec((None, E, TM), lambda b, i: (b, 0, i))],
        out_shape=[jax.ShapeDtypeStruct((B, T, D), F32),
                   jax.ShapeDtypeStruct((B, T, D), BF16),
                   jax.ShapeDtypeStruct((B, T, SC_WIDTH), I32),
                   jax.ShapeDtypeStruct((B, T, SC_WIDTH), I32),
                   jax.ShapeDtypeStruct((B, E, T), F32)],
        compiler_params=_cparams(("arbitrary",) * 2),
        name="merge_postnorm",
    )(o_nsa, o_dsa, gm, x, wn, wd, wo, g_post.reshape(1, D), gt1.reshape(B, 1, D),
      g_pre.reshape(1, D), sc2.reshape(B, 1, D), sh2.reshape(B, 1, D), rwt)


TR = 256
GSZ = N_EXPERTS // N_EXPERT_GROUPS


def _first_max(v, idx, big):
    mx = jnp.max(v, axis=0, keepdims=True)
    return jnp.min(jnp.where(v == mx, idx, big), axis=0, keepdims=True), mx


def _route_body(lt_ref, bias_ref, eid_ref, rk_ref, wt_ref, cnt_ref, carry_ref):
    first_step = (pl.program_id(0) == 0) & (pl.program_id(1) == 0)

    @pl.when(first_step)
    def _():
        carry_ref[...] = jnp.zeros(carry_ref.shape, F32)

    s = jax.nn.sigmoid(lt_ref[...])
    sel = s + bias_ref[:, 0:1]
    grow = lax.broadcasted_iota(I32, (GSZ, TR), 0)
    gs = []
    for g in range(N_EXPERT_GROUPS):
        blk = sel[g * GSZ:(g + 1) * GSZ]
        f1, m1 = _first_max(blk, grow, GSZ)
        m2 = jnp.max(jnp.where(grow == f1, KNOCK, blk), axis=0, keepdims=True)
        gs.append(m1 + m2)
    gsc = jnp.concatenate(gs, axis=0)
    gidx = lax.broadcasted_iota(I32, (N_EXPERT_GROUPS, TR), 0)
    gsel = jnp.zeros((N_EXPERT_GROUPS, TR), F32)
    for _ in range(TOPK_GROUPS):
        f, _mx = _first_max(gsc, gidx, N_EXPERT_GROUPS)
        pick = gidx == f
        gsel = jnp.where(pick, 1.0, gsel)
        gsc = jnp.where(pick, KNOCK, gsc)
    cur = jnp.concatenate(
        [jnp.where(gsel[g:g + 1] > 0.0, sel[g * GSZ:(g + 1) * GSZ], NEG_INF)
         for g in range(N_EXPERT_GROUPS)], axis=0)
    erow = lax.broadcasted_iota(I32, (N_EXPERTS, TR), 0)
    member = jnp.zeros((N_EXPERTS, TR), F32)
    picks, eids, ws = [], [], []
    for _ in range(TOP_K):
        f, _mx = _first_max(cur, erow, N_EXPERTS)
        pick = erow == f
        picks.append(pick)
        eids.append(f)
        ws.append(jnp.sum(jnp.where(pick, s, 0.0), axis=0, keepdims=True))
        member = jnp.where(pick, 1.0, member)
        cur = jnp.where(pick, KNOCK, cur)
    wsum = ws[0]
    for k in range(1, TOP_K):
        wsum = wsum + ws[k]
    mb = member.astype(BF16)
    upper = (lax.broadcasted_iota(I32, (TR, TR), 0) < lax.broadcasted_iota(I32, (TR, TR), 1))
    upper = jnp.where(upper, 1.0, 0.0).astype(BF16)
    rank = jnp.dot(mb, upper, preferred_element_type=F32) + carry_ref[:, 0:1]
    eid_ref[...] = jnp.concatenate(eids, axis=0)
    rk_ref[...] = jnp.concatenate(
        [jnp.sum(jnp.where(p, rank, 0.0), axis=0, keepdims=True) for p in picks], axis=0)
    wt_ref[...] = jnp.concatenate([w / wsum * ROUTED_SCALE for w in ws], axis=0)
    carry_ref[...] = carry_ref[...] + jnp.dot(mb, jnp.ones((TR, LANE), BF16),
                                              preferred_element_type=F32)
    cnt_ref[...] = carry_ref[...]


def _route(lt, router_bias):
    B, E, T = lt.shape
    out8 = lambda: pl.BlockSpec((None, TOP_K, TR), lambda b, i: (b, 0, i))
    return pl.pallas_call(
        _route_body,
        grid=(B, T // TR),
        in_specs=[pl.BlockSpec((None, E, TR), lambda b, i: (b, 0, i)),
                  pl.BlockSpec((E, 1), lambda b, i: (0, 0))],
        out_specs=[out8(), out8(), out8(), pl.BlockSpec((E, LANE), lambda b, i: (0, 0))],
        out_shape=[jax.ShapeDtypeStruct((B, TOP_K, T), I32),
                   jax.ShapeDtypeStruct((B, TOP_K, T), F32),
                   jax.ShapeDtypeStruct((B, TOP_K, T), F32),
                   jax.ShapeDtypeStruct((E, LANE), F32)],
        scratch_shapes=[pltpu.VMEM((E, LANE), F32)],
        compiler_params=_cparams(("arbitrary",) * 2),
        name="moe_router",
    )(lt, router_bias.reshape(E, 1))


BM = 256


def _gmm_body(be_ref, nv_ref, xa_ref, xb_ref, wg_ref, wu_ref, wd_ref, ya_ref, yb_ref,
              wgb, wub, wdb):
    b = pl.program_id(0)
    valid = nv_ref[b]

    @pl.when(valid > 0)
    def _():
        prev = be_ref[jnp.maximum(b - 1, 0)]

        @pl.when((b == 0) | (be_ref[b] != prev))
        def _():
            wgb[...] = wg_ref[...].astype(BF16)
            wub[...] = wu_ref[...].astype(BF16)
            wdb[...] = wd_ref[...].astype(BF16)

        live = lax.broadcasted_iota(I32, (BM, 1), 0) < valid
        lo_a, hi_a = _unpack_halves(jnp.where(live, xa_ref[...], 0))
        lo_b, hi_b = _unpack_halves(jnp.where(live, xb_ref[...], 0))
        xb = jnp.concatenate([lo_a, lo_b, hi_a, hi_b], axis=1).astype(BF16)
        gt = jnp.dot(xb, wgb[...], preferred_element_type=F32)
        up = jnp.dot(xb, wub[...], preferred_element_type=F32)
        a = (gt * jax.nn.sigmoid(gt) * up).astype(BF16)
        y = jnp.dot(a, wdb[...], preferred_element_type=F32)
        packed = _pack_halves(y.astype(BF16).astype(F32))
        ya_ref[...] = packed[:, :SC_WIDTH]
        yb_ref[...] = packed[:, SC_WIDTH:]

    @pl.when(valid == 0)
    def _():
        ya_ref[...] = jnp.zeros(ya_ref.shape, I32)
        yb_ref[...] = jnp.zeros(yb_ref.shape, I32)


def _grouped_mlp(xs_a, xs_b, blk_e, n_valid, w_g, w_u, w_d):
    P = xs_a.shape[0]
    D, De = w_g.shape[-2:]
    nb = P // BM
    rows = pl.BlockSpec((BM, SC_WIDTH), lambda b, be, nv: (b, 0))
    wsel = lambda b, be, nv: (be[b], 0, 0)
    grid_spec = pltpu.PrefetchScalarGridSpec(
        num_scalar_prefetch=2,
        grid=(nb,),
        in_specs=[rows, rows,
                  pl.BlockSpec((None, D, De), wsel),
                  pl.BlockSpec((None, D, De), wsel),
                  pl.BlockSpec((None, De, D), wsel)],
        out_specs=[rows, rows],
        scratch_shapes=[pltpu.VMEM((D, De), BF16), pltpu.VMEM((D, De), BF16),
                        pltpu.VMEM((De, D), BF16)])
    out = jax.ShapeDtypeStruct((P, SC_WIDTH), I32)
    return pl.pallas_call(
        _gmm_body,
        grid_spec=grid_spec,
        out_shape=[out, out],
        compiler_params=_cparams(("arbitrary",)),
        name="moe_experts",
    )(blk_e, n_valid, xs_a, xs_b, w_g, w_u, w_d)


def _sc_mesh():
    return plsc.VectorSubcoreMesh(core_axis_name="c", subcore_axis_name="s")


def _sc_gather(x, idx):
    n = idx.shape[0]

    @pl.kernel(out_type=jax.ShapeDtypeStruct((n, SC_WIDTH), x.dtype), mesh=_sc_mesh())
    def gather(x_hbm, i_hbm, o_hbm):
        def body(i_vmem, o_vmem):
            pltpu.sync_copy(x_hbm.at[i_vmem.at[0]], o_vmem)

        pltpu.emit_pipeline(
            body, grid=(n // SC_WINDOW,),
            in_specs=[pl.BlockSpec((1, SC_WINDOW), lambda i: (0, i))],
            out_specs=[pl.BlockSpec((SC_WINDOW, SC_WIDTH), lambda i: (i, 0))],
            core_axis_name=("c", "s"),
            dimension_semantics=(pltpu.PARALLEL,),
        )(i_hbm, o_hbm)

    return gather(x, idx.reshape(1, n))


def _sc_dispatch(x, slot, n_out):
    B, T, W = x.shape
    K = slot.shape[1]
    tb = T // SC_WINDOW
    n = B * K * T

    @pl.kernel(out_type=jax.ShapeDtypeStruct((n_out, W), x.dtype), mesh=_sc_mesh(), scratch_types=[])
    def scatter(x_hbm, i_hbm, o_hbm):
        def body(x_vmem, i_vmem):
            pltpu.sync_copy(x_vmem, o_hbm.at[i_vmem.at[0]])

        pltpu.emit_pipeline(
            body, grid=(n // SC_WINDOW,),
            in_specs=[pl.BlockSpec((SC_WINDOW, W), lambda i: ((i // (K * tb)) * tb + i % tb, 0)),
                      pl.BlockSpec((1, SC_WINDOW), lambda i: (0, i))],
            out_specs=[],
            core_axis_name=("c", "s"),
            dimension_semantics=(pltpu.PARALLEL,),
        )(x_hbm, i_hbm)

    return scatter(x.reshape(B * T, W), slot.reshape(1, n))


def _final_body(ya_ref, yb_ref, w_ref, h_ref, x1_ref, wg_ref, wu_ref, wd_ref, g_ref, gt_ref, o_ref):
    hb = h_ref[...]
    gt = jnp.dot(hb, wg_ref[...], preferred_element_type=F32)
    up = jnp.dot(hb, wu_ref[...], preferred_element_type=F32)
    a = (gt * jax.nn.sigmoid(gt) * up).astype(BF16)
    shared = jnp.dot(a, wd_ref[...], preferred_element_type=F32)
    w = w_ref[...]
    parts = None
    for k in range(TOP_K):
        wk = w[:, k:k + 1]
        lo_a, hi_a = _unpack_halves(ya_ref[k])
        lo_b, hi_b = _unpack_halves(yb_ref[k])
        terms = [lo_a * wk, lo_b * wk, hi_a * wk, hi_b * wk]
        parts = terms if parts is None else [p + t for p, t in zip(parts, terms)]
    y = jnp.concatenate(parts, axis=1) + shared
    o_ref[...] = x1_ref[...] + gt_ref[...] * _rms(y, g_ref[...])


def _final(yg_a, yg_b, w, h2, x1, ws_g, ws_u, ws_d, g_post, gt2):
    B, T, D = x1.shape
    TM = 256 if T % 256 == 0 else TQ
    row = lambda wd: pl.BlockSpec((None, TM, wd), lambda b, i: (b, i, 0))
    krows = pl.BlockSpec((None, TOP_K, TM, SC_WIDTH), lambda b, i: (b, 0, i, 0))
    full = lambda a: pl.BlockSpec(a.shape, lambda b, i: (0,) * a.ndim)
    ws_g, ws_u, ws_d = ws_g.astype(BF16), ws_u.astype(BF16), ws_d.astype(BF16)
    return pl.pallas_call(
        _final_body,
        grid=(B, T // TM),
        in_specs=[krows, krows, row(TOP_K), row(D), row(D),
                  full(ws_g), full(ws_u), full(ws_d),
                  pl.BlockSpec((1, D), lambda b, i: (0, 0)),
                  pl.BlockSpec((None, 1, D), lambda b, i: (b, 0, 0))],
        out_specs=row(D),
        out_shape=jax.ShapeDtypeStruct((B, T, D), F32),
        compiler_params=_cparams(("arbitrary",) * 2),
        name="shared_final",
    )(yg_a, yg_b, w, h2, x1, ws_g, ws_u, ws_d, g_post.reshape(1, D), gt2.reshape(B, 1, D))


def _slot_body(eid_ref, rk_ref, ps_ref, o_ref):
    erow = lax.broadcasted_iota(I32, (N_EXPERTS, TR), 0)
    ps = ps_ref[:, 0:1]
    rows = []
    for k in range(TOP_K):
        base = jnp.sum(jnp.where(erow == eid_ref[k:k + 1, :], ps, 0.0), axis=0, keepdims=True)
        rows.append(base + rk_ref[k:k + 1, :])
    o_ref[...] = jnp.concatenate(rows, axis=0).astype(I32)


def _slots(eid, rk, pstart):
    B, K, T = eid.shape
    blk = pl.BlockSpec((None, K, TR), lambda b, i: (b, 0, i))
    return pl.pallas_call(
        _slot_body,
        grid=(B, T // TR),
        in_specs=[blk, blk, pl.BlockSpec((N_EXPERTS, 1), lambda b, i: (0, 0))],
        out_specs=blk,
        out_shape=jax.ShapeDtypeStruct((B, K, T), I32),
        compiler_params=_cparams(("arbitrary",) * 2),
        name="moe_slots",
    )(eid, rk, pstart.astype(F32).reshape(N_EXPERTS, 1))


def _moe(h2, hp_a, hp_b, lt, x1, router_bias, w_g, w_u, w_d, ws_g, ws_u, ws_d, g_post, gt2):
    B, T, D = x1.shape
    N = B * T
    eid, rk, wts, cnt = _route(lt, router_bias)
    counts = cnt[:, 0].astype(I32)
    padded = (counts + BM - 1) // BM * BM
    pend = jnp.cumsum(padded)
    pstart = pend - padded
    nb = (N * TOP_K + N_EXPERTS * (BM - 1) + BM - 1) // BM
    P = nb * BM
    row0 = jnp.arange(nb, dtype=I32) * BM
    blk_e = jnp.minimum(jnp.sum(pend[None, :] <= row0[:, None], axis=1), N_EXPERTS - 1).astype(I32)
    n_valid = jnp.clip(pstart[blk_e] + counts[blk_e] - row0, 0, BM).astype(I32)
    slot = _slots(eid, rk, pstart)
    xs_a = _sc_dispatch(hp_a, slot, P)
    xs_b = _sc_dispatch(hp_b, slot, P)
    ys_a, ys_b = _grouped_mlp(xs_a, xs_b, blk_e, n_valid, w_g, w_u, w_d)
    flat = slot.reshape(N * TOP_K)
    yg_a = _sc_gather(ys_a, flat).reshape(B, TOP_K, T, SC_WIDTH)
    yg_b = _sc_gather(ys_b, flat).reshape(B, TOP_K, T, SC_WIDTH)
    return _final(yg_a, yg_b, wts.transpose(0, 2, 1), h2, x1, ws_g, ws_u, ws_d, g_post, gt2)


def kernel(x, c, w_ada, b_ada, g_pre_mix, g_post_mix, g_pre_ffn, g_post_ffn, w_in, rel_bias, cmp_pe, cmp_w1, cmp_b1, cmp_w2, dsa_kv_norm, dsa_w_uk, dsa_w_uv, w_branch_nsa, w_branch_dsa, w_out, router_w, router_bias, w_exp_gate, w_exp_up, w_exp_down, w_sh_gate, w_sh_up, w_sh_down):
    depth = w_ada.shape[0]
    D = D_MODEL
    for l in range(depth):
        mod = _adaln(c, w_ada[l], b_ada[l])
        sh1, sc1, gt1, sh2, sc2, gt2 = [mod[:, k * D:(k + 1) * D] for k in range(N_MOD)]
        qn, kvn, qd, ckv, qi, small, gm, ki = _input_proj(x, g_pre_mix[l], sc1, sh1, w_in[l],
                                                          dsa_kv_norm[l])
        o_nsa, o_dsa = _token_mixers(qn, kvn, qd, ckv, qi, small, ki, rel_bias, cmp_pe[l], cmp_w1[l],
                                     cmp_b1[l], cmp_w2[l], dsa_w_uk[l], dsa_w_uv[l])
        x1, h2, hp_a, hp_b, lt = _merge(o_nsa, o_dsa, gm, x,
                                        w_branch_nsa[l], w_branch_dsa[l], w_out[l], g_post_mix[l],
                                        gt1, g_pre_ffn[l], sc2, sh2, router_w[l])
        x = _moe(h2, hp_a, hp_b, lt, x1, router_bias[l], w_exp_gate[l], w_exp_up[l],
                 w_exp_down[l], w_sh_gate[l], w_sh_up[l], w_sh_down[l], g_post_ffn[l], gt2)
    return x
```
